```python
import math
import jax, jax.numpy as jnp
from jax import lax
import numpy as np

D_MODEL = 2048
BATCH = 2
SEQ = 4096
DEPTH = 1

HEAD_DIM = 128
N_HEAD_SLOTS = 8
DILATED_GROUPS = ((128, 1), (512, 4), (2048, 16))
N_GROUPS = len(DILATED_GROUPS)
ATTN_WIDTH = N_HEAD_SLOTS * HEAD_DIM
POOL_WINDOWS = (2, 4, 8, 16)
N_POOL_GROUPS = len(POOL_WINDOWS)
POOL_WIDTH = D_MODEL // 2
POOL_GROUP = POOL_WIDTH // N_POOL_GROUPS
N_BRANCHES = 2
SPLIT_SIZES = (N_GROUPS * ATTN_WIDTH,
               N_GROUPS * ATTN_WIDTH,
               N_GROUPS * ATTN_WIDTH,
               ATTN_WIDTH,
               POOL_WIDTH,
               POOL_WIDTH,
               N_BRANCHES * D_MODEL)
IN_WIDTH = sum(SPLIT_SIZES)
DEEPNORM_ALPHA = (2.0 * DEPTH) ** 0.25
DEEPNORM_BETA = (8.0 * DEPTH) ** -0.25
LN_EPS = 1e-5
NEG_INF = -1e30

kernel_name = "hybrid_dilated_attn_pool_gated_deepnorm"


def layer_norm(x, gamma, beta):
    xf = x.astype(jnp.float32)
    mu = jnp.mean(xf, axis=-1, keepdims=True)
    var = jnp.mean(jnp.square(xf - mu), axis=-1, keepdims=True)
    y = (xf - mu) * lax.rsqrt(var + LN_EPS) * gamma.astype(jnp.float32) + beta.astype(jnp.float32)
    return y.astype(x.dtype)


def dilated_window_attention(q, k, v, window, dilation):
    B, S, H, hd = q.shape
    steps = window // dilation
    span = steps * dilation
    Sp = -(-S // span) * span
    L = Sp // dilation
    nb = L // steps

    def to_blocks(t):
        t = jnp.pad(t, ((0, 0), (0, Sp - S), (0, 0), (0, 0)))
        t = t.reshape(B, L, dilation, H, hd).transpose(0, 2, 1, 3, 4)
        return t.reshape(B, dilation, nb, steps, H, hd)

    def with_prev(t):
        prev = jnp.pad(t[:, :, :-1], ((0, 0), (0, 0), (1, 0), (0, 0), (0, 0), (0, 0)))
        return jnp.concatenate([prev, t], axis=3)

    qb = to_blocks(q)
    kb = with_prev(to_blocks(k))
    vb = with_prev(to_blocks(v))
    scores = jnp.einsum('brnqhd,brnkhd->brnhqk', qb, kb,
                        preferred_element_type=jnp.float32) * (hd ** -0.5)
    qi = np.arange(steps)[:, None]
    kj = np.arange(2 * steps)[None, :]
    dist = steps + qi - kj
    blk = np.arange(nb)[:, None, None]
    mask = (dist >= 0) & (dist <= steps) & ((blk > 0) | (kj >= steps))[...]
    scores = jnp.where(jnp.asarray(mask)[None, None, :, None], scores, NEG_INF)
    m = jnp.max(scores, axis=-1, keepdims=True)
    e = jnp.exp(scores - m)
    den = jnp.sum(e, axis=-1)
    o = jnp.einsum('brnhqk,brnkhd->brnqhd', e.astype(v.dtype), vb,
                   preferred_element_type=jnp.float32)
    o = o / jnp.transpose(den, (0, 1, 2, 4, 3))[..., None]
    lse = m[..., 0] + jnp.log(den)
    o = o.reshape(B, dilation, L, H, hd).transpose(0, 2, 1, 3, 4).reshape(B, Sp, H, hd)[:, :S]
    lse = jnp.transpose(lse, (0, 1, 2, 4, 3)).reshape(B, dilation, L, H)
    lse = lse.transpose(0, 2, 1, 3).reshape(B, Sp, H)[:, :S]
    return o, lse


def causal_pool_mixer(u, w_pool, pool_scale):
    B, S, _ = u.shape
    uf = u.astype(jnp.float32)
    c = jnp.concatenate([jnp.zeros((B, 1, POOL_WIDTH), jnp.float32), jnp.cumsum(uf, axis=1)], axis=1)
    hi = np.arange(1, S + 1)
    outs = []
    for g, w in enumerate(POOL_WINDOWS):
        lo = np.maximum(hi - w, 0)
        cnt = np.minimum(hi, w).astype(np.float32)
        sl = slice(g * POOL_GROUP, (g + 1) * POOL_GROUP)
        window_sum = jnp.take(c[..., sl], hi, axis=1) - jnp.take(c[..., sl], lo, axis=1)
        outs.append(window_sum / jnp.asarray(cnt)[None, :, None] - uf[..., sl])
    p = jnp.stack(outs, axis=2).astype(u.dtype)
    y = jnp.einsum('bsgc,gcd->bsgd', p, w_pool).reshape(B, S, POOL_WIDTH)
    return y * pool_scale


def hybrid_layer(x, w_in, b_gate, w_pool, pool_scale, w_proj_attn, w_proj_pool, w_out, ln_gamma, ln_beta):
    B, S, _ = x.shape
    h = jnp.einsum('bsd,de->bse', x, w_in)
    idx = list(np.cumsum(SPLIT_SIZES)[:-1])
    q, k, v, z_attn, u_pool, z_pool, g_pre = jnp.split(h, idx, axis=-1)
    q = q.reshape(B, S, N_GROUPS, N_HEAD_SLOTS, HEAD_DIM)
    k = k.reshape(B, S, N_GROUPS, N_HEAD_SLOTS, HEAD_DIM)
    v = v.reshape(B, S, N_GROUPS, N_HEAD_SLOTS, HEAD_DIM)
    outs, lses = [], []
    for g, (window, dilation) in enumerate(DILATED_GROUPS):
        o_g, lse_g = dilated_window_attention(q[:, :, g], k[:, :, g], v[:, :, g], window, dilation)
        outs.append(o_g)
        lses.append(lse_g)
    wts = jax.nn.softmax(jnp.stack(lses, axis=0), axis=0)
    o = jnp.sum(wts[..., None] * jnp.stack(outs, axis=0), axis=0).reshape(B, S, ATTN_WIDTH)
    y_attn = o.astype(x.dtype) * jax.nn.silu(z_attn)
    y_pool = causal_pool_mixer(u_pool, w_pool, pool_scale) * jax.nn.silu(z_pool)
    gates = jax.nn.sigmoid((g_pre + b_gate).astype(jnp.float32)).astype(x.dtype)
    g_attn, g_pool = jnp.split(gates, 2, axis=-1)
    merged = g_attn * jnp.einsum('bsc,cd->bsd', y_attn, w_proj_attn) \
        + g_pool * jnp.einsum('bsc,cd->bsd', y_pool, w_proj_pool)
    out = jnp.einsum('bsd,de->bse', merged, w_out)
    return layer_norm(DEEPNORM_ALPHA * x + out, ln_gamma, ln_beta)


def setup_inputs(seed: int = 0) -> dict:
    key = jax.random.key(seed)
    ks = jax.random.split(key, 11)
    f32 = jnp.float32
    x = jax.random.normal(ks[0], (BATCH, SEQ, D_MODEL), f32)
    w_in = jax.random.normal(ks[1], (DEPTH, D_MODEL, IN_WIDTH), f32) * D_MODEL ** -0.5
    b_gate = jax.random.normal(ks[2], (DEPTH, N_BRANCHES * D_MODEL), f32) * 0.02
    w_pool = jax.random.normal(ks[3], (DEPTH, N_POOL_GROUPS, POOL_GROUP, POOL_GROUP), f32) * POOL_GROUP ** -0.5
    pool_scale = 1.0 + 0.02 * jax.random.normal(ks[4], (DEPTH, POOL_WIDTH), f32)
    w_proj_attn = jax.random.normal(ks[5], (DEPTH, ATTN_WIDTH, D_MODEL), f32) * ATTN_WIDTH ** -0.5 * DEEPNORM_BETA
    w_proj_pool = jax.random.normal(ks[6], (DEPTH, POOL_WIDTH, D_MODEL), f32) * POOL_WIDTH ** -0.5 * DEEPNORM_BETA
    w_out = jax.random.normal(ks[7], (DEPTH, D_MODEL, D_MODEL), f32) * D_MODEL ** -0.5 * DEEPNORM_BETA
    ln_gamma = 1.0 + 0.02 * jax.random.normal(ks[8], (DEPTH, D_MODEL), f32)
    ln_beta = 0.02 * jax.random.normal(ks[9], (DEPTH, D_MODEL), f32)
    return {"x": x, "w_in": w_in, "b_gate": b_gate, "w_pool": w_pool, "pool_scale": pool_scale,
            "w_proj_attn": w_proj_attn, "w_proj_pool": w_proj_pool, "w_out": w_out,
            "ln_gamma": ln_gamma, "ln_beta": ln_beta}


def reference(x, w_in, b_gate, w_pool, pool_scale, w_proj_attn, w_proj_pool, w_out, ln_gamma, ln_beta):
    for layer in range(DEPTH):
        x = hybrid_layer(x, w_in[layer], b_gate[layer], w_pool[layer], pool_scale[layer],
                         w_proj_attn[layer], w_proj_pool[layer], w_out[layer],
                         ln_gamma[layer], ln_beta[layer])
    return x
```

```python
import functools

import jax
import jax.numpy as jnp
from jax import lax
from jax.experimental import pallas as pl
from jax.experimental.pallas import tpu as pltpu

D_MODEL = 2048
HEAD_DIM = 128
N_HEADS = 8
ATTN_WIDTH = N_HEADS * HEAD_DIM
DILATED_GROUPS = ((128, 1), (512, 4), (2048, 16))
N_GROUPS = len(DILATED_GROUPS)
POOL_WINDOWS = (2, 4, 8, 16)
POOL_WIDTH = D_MODEL // 2
POOL_GROUP = POOL_WIDTH // len(POOL_WINDOWS)
DEEPNORM_ALPHA = 2.0 ** 0.25
LN_EPS = 1e-5
NEG_INF = -1e30

LANES = 128
MIB = 1024 * 1024

COL_TILE = 1024
W_TILE_U = 10
H_TILES = 15
H_WIDTH = H_TILES * COL_TILE
H_TILE_GATES = 0
H_TILE_QKV = 4
H_TILE_ZATTN = 13
H_TILE_ZPOOL = 14

ATTN_BLOCK = 128
HALO = 16


def _w_tile_of_h_tile(j):
    return jnp.where(j < 4, j + 12, jnp.where(j < 13, j - 4, jnp.where(j == 13, 9, 11)))


def _proj_kernel(x_ref, w_ref, b_ref, o_ref):
    j = pl.program_id(0)
    acc = jnp.dot(x_ref[...], w_ref[...], preferred_element_type=jnp.float32)

    @pl.when(j < 4)
    def _():
        o_ref[...] = jax.nn.sigmoid(acc + b_ref[...]).astype(o_ref.dtype)

    @pl.when(jnp.logical_and(j >= 4, j < 13))
    def _():
        o_ref[...] = acc.astype(o_ref.dtype)

    @pl.when(j >= 13)
    def _():
        o_ref[...] = (acc * jax.nn.sigmoid(acc)).astype(o_ref.dtype)


def _proj_u_kernel(x_ref, w_ref, o_ref):
    o_ref[...] = jnp.dot(x_ref[...], w_ref[...], preferred_element_type=jnp.float32)


def _input_projection(x2, w_bf, b_gate2, tm=1024):
    m = x2.shape[0]
    grid = (H_TILES, m // tm)
    h = pl.pallas_call(
        _proj_kernel,
        grid=grid,
        in_specs=[
            pl.BlockSpec((tm, D_MODEL), lambda j, i: (i, 0)),
            pl.BlockSpec((D_MODEL, COL_TILE), lambda j, i: (0, _w_tile_of_h_tile(j))),
            pl.BlockSpec((1, COL_TILE), lambda j, i: (0, jnp.minimum(j, 3))),
        ],
        out_specs=pl.BlockSpec((tm, COL_TILE), lambda j, i: (i, j)),
        out_shape=jax.ShapeDtypeStruct((m, H_WIDTH), jnp.bfloat16),
        compiler_params=pltpu.CompilerParams(
            dimension_semantics=("arbitrary", "arbitrary"),
            vmem_limit_bytes=48 * MIB),
        name="proj_main",
    )(x2, w_bf, b_gate2)
    u = pl.pallas_call(
        _proj_u_kernel,
        grid=(m // tm,),
        in_specs=[
            pl.BlockSpec((tm, D_MODEL), lambda i: (i, 0)),
            pl.BlockSpec((D_MODEL, COL_TILE), lambda i: (0, W_TILE_U)),
        ],
        out_specs=pl.BlockSpec((tm, COL_TILE), lambda i: (i, 0)),
        out_shape=jax.ShapeDtypeStruct((m, POOL_WIDTH), jnp.float32),
        compiler_params=pltpu.CompilerParams(
            dimension_semantics=("arbitrary",),
            vmem_limit_bytes=48 * MIB),
        name="proj_u",
    )(x2, w_bf)
    return h, u


def _attn_kernel(q_ref, k_ref, v_ref, o_ref, lse_ref, kc_ref, vc_ref, *, n_blocks, n_chunks):
    chunk = pl.program_id(2)
    bq = ATTN_BLOCK
    scale = HEAD_DIM ** -0.5

    @pl.when(chunk == 0)
    def _():
        kc_ref[...] = jnp.zeros_like(kc_ref)
        vc_ref[...] = jnp.zeros_like(vc_ref)

    row = lax.broadcasted_iota(jnp.int32, (bq, 2 * bq), 0)
    col = lax.broadcasted_iota(jnp.int32, (bq, 2 * bq), 1)
    dist = bq + row - col
    band = jnp.logical_and(dist >= 0, dist <= bq)
    band_first = jnp.logical_and(band, col >= jnp.where(chunk > 0, 0, bq))
    lane = lax.broadcasted_iota(jnp.int32, (bq, LANES), 1)

    def block(r0, kcat_of, vcat_of, mask):
        lse_tile = jnp.zeros((bq, LANES), jnp.float32)
        for h in range(N_HEADS):
            hs = slice(h * HEAD_DIM, (h + 1) * HEAD_DIM)
            q = q_ref[pl.ds(r0, bq), hs]
            kcat = kcat_of(hs)
            vcat = vcat_of(hs)
            s = lax.dot_general(q, kcat, (((1,), (1,)), ((), ())),
                                preferred_element_type=jnp.float32) * scale
            s = jnp.where(mask, s, NEG_INF)
            m = jnp.max(s, axis=-1, keepdims=True)
            e = jnp.exp(s - m)
            den = jnp.sum(e, axis=-1, keepdims=True)
            o = jnp.dot(e.astype(vcat.dtype), vcat, preferred_element_type=jnp.float32)
            o_ref[pl.ds(r0, bq), hs] = (o / den).astype(o_ref.dtype)
            lse_tile = jnp.where(lane == h, m + jnp.log(den), lse_tile)
        lse_ref[pl.ds(r0, bq), :] = lse_tile

    block(0,
          lambda hs: jnp.concatenate([kc_ref[:, hs], k_ref[0:bq, hs]], axis=0),
          lambda hs: jnp.concatenate([vc_ref[:, hs], v_ref[0:bq, hs]], axis=0),
          band_first)

    def body(blk, carry):
        r0 = pl.multiple_of(blk * bq, bq)
        rk = pl.multiple_of(blk * bq - bq, bq)
        block(r0,
              lambda hs: k_ref[pl.ds(rk, 2 * bq), hs],
              lambda hs: v_ref[pl.ds(rk, 2 * bq), hs],
              band)
        return carry

    if n_blocks > 1:
        lax.fori_loop(1, n_blocks, body, 0)

    if n_chunks > 1:
        rl = (n_blocks - 1) * bq
        kc_ref[...] = k_ref[rl:rl + bq, :]
        vc_ref[...] = v_ref[rl:rl + bq, :]


def _dilated_attention(h3, group, dilation, n_blocks):
    b, s, _ = h3.shape
    sub_len = s // dilation
    rows = n_blocks * ATTN_BLOCK
    n_chunks = sub_len // rows
    hv = h3.reshape(b, sub_len, dilation * H_WIDTH)
    q_tile = H_TILE_QKV + group
    k_tile = H_TILE_QKV + N_GROUPS + group
    v_tile = H_TILE_QKV + 2 * N_GROUPS + group

    def in_spec(tile):
        return pl.BlockSpec((None, rows, COL_TILE), lambda bb, r, c: (bb, c, r * H_TILES + tile))

    o, lse = pl.pallas_call(
        functools.partial(_attn_kernel, n_blocks=n_blocks, n_chunks=n_chunks),
        grid=(b, dilation, n_chunks),
        in_specs=[in_spec(q_tile), in_spec(k_tile), in_spec(v_tile)],
        out_specs=[
            pl.BlockSpec((None, rows, ATTN_WIDTH), lambda bb, r, c: (bb, c, r)),
            pl.BlockSpec((None, rows, LANES), lambda bb, r, c: (bb, c, r)),
        ],
        out_shape=[
            jax.ShapeDtypeStruct((b, sub_len, dilation * ATTN_WIDTH), jnp.bfloat16),
            jax.ShapeDtypeStruct((b, sub_len, dilation * LANES), jnp.float32),
        ],
        scratch_shapes=[
            pltpu.VMEM((ATTN_BLOCK, ATTN_WIDTH), jnp.bfloat16),
            pltpu.VMEM((ATTN_BLOCK, ATTN_WIDTH), jnp.bfloat16),
        ],
        compiler_params=pltpu.CompilerParams(
            dimension_semantics=("arbitrary", "arbitrary", "arbitrary"),
            vmem_limit_bytes=40 * MIB),
        name=f"dilated_attn_g{group}",
    )(hv, hv, hv)
    return o.reshape(b, s, ATTN_WIDTH), lse.reshape(b, s, LANES)


def _merge_kernel(o1_ref, o2_ref, o3_ref, l1_ref, l2_ref, l3_ref, za_ref, zp_ref, g_attn_ref, g_pool_ref,
                  u_ref, halo_ref, x_ref, w_pool_ref, pool_scale_ref, wpa_ref, wpp_ref, w_out_ref,
                  gamma_ref, beta_ref, out_ref, ext_ref, ya_ref, yp_ref, *, tm):
    i = pl.program_id(1)

    l1 = l1_ref[...]
    l2 = l2_ref[...]
    l3 = l3_ref[...]
    mx = jnp.maximum(jnp.maximum(l1, l2), l3)
    e1 = jnp.exp(l1 - mx)
    e2 = jnp.exp(l2 - mx)
    e3 = jnp.exp(l3 - mx)
    tot = e1 + e2 + e3
    w1 = e1 / tot
    w2 = e2 / tot
    w3 = e3 / tot
    for h in range(N_HEADS):
        hs = slice(h * HEAD_DIM, (h + 1) * HEAD_DIM)
        o = (w1[:, h:h + 1] * o1_ref[:, hs].astype(jnp.float32)
             + w2[:, h:h + 1] * o2_ref[:, hs].astype(jnp.float32)
             + w3[:, h:h + 1] * o3_ref[:, hs].astype(jnp.float32))
        ya_ref[:, hs] = (o * za_ref[:, hs].astype(jnp.float32)).astype(ya_ref.dtype)

    ext_ref[HALO:, :] = u_ref[...]

    @pl.when(i == 0)
    def _():
        ext_ref[0:HALO, :] = jnp.zeros((HALO, POOL_WIDTH), jnp.float32)

    @pl.when(i > 0)
    def _():
        ext_ref[0:HALO, :] = halo_ref[...]

    pos = i * tm + lax.broadcasted_iota(jnp.int32, (tm, 1), 0)
    for g, w in enumerate(POOL_WINDOWS):
        cs = slice(g * POOL_GROUP, (g + 1) * POOL_GROUP)
        tok = ext_ref[HALO:HALO + tm, cs]
        acc = tok
        for k in range(1, w):
            acc = acc + ext_ref[HALO - k:HALO - k + tm, cs]
        cnt = jnp.minimum(pos + 1, w).astype(jnp.float32)
        p = acc / cnt - tok
        y = jnp.dot(p.astype(jnp.bfloat16), w_pool_ref[g], preferred_element_type=jnp.float32)
        y = y * pool_scale_ref[:, cs] * zp_ref[:, cs].astype(jnp.float32)
        yp_ref[:, cs] = y.astype(yp_ref.dtype)

    pa = jnp.dot(ya_ref[...], wpa_ref[...], preferred_element_type=jnp.float32)
    pp = jnp.dot(yp_ref[...], wpp_ref[...], preferred_element_type=jnp.float32)
    merged = (g_attn_ref[...].astype(jnp.float32) * pa + g_pool_ref[...].astype(jnp.float32) * pp)
    out = jnp.dot(merged.astype(jnp.bfloat16), w_out_ref[...], preferred_element_type=jnp.float32)
    r = DEEPNORM_ALPHA * x_ref[...] + out
    mu = jnp.mean(r, axis=-1, keepdims=True)
    rc = r - mu
    var = jnp.mean(rc * rc, axis=-1, keepdims=True)
    out_ref[...] = rc * lax.rsqrt(var + LN_EPS) * gamma_ref[...] + beta_ref[...]


def _merge(os_, lses, h3, u3, x, w_pool_bf, pool_scale2, wpa_bf, wpp_bf, w_out_bf, gamma2, beta2, tm=256):
    b, s, _ = x.shape
    grid = (b, s // tm)

    def tok(width, tile=0):
        return pl.BlockSpec((None, tm, width), lambda bb, i: (bb, i, tile))

    def const(shape):
        nd = len(shape)
        return pl.BlockSpec(shape, lambda bb, i: (0,) * nd, pipeline_mode=pl.Buffered(1))

    halo_blocks = tm // HALO
    in_specs = [
        tok(ATTN_WIDTH), tok(ATTN_WIDTH), tok(ATTN_WIDTH),
        tok(LANES), tok(LANES), tok(LANES),
        tok(COL_TILE, H_TILE_ZATTN), tok(COL_TILE, H_TILE_ZPOOL),
        tok(D_MODEL, 0), tok(D_MODEL, 1),
        tok(POOL_WIDTH),
        pl.BlockSpec((None, HALO, POOL_WIDTH),
                     lambda bb, i: (bb, jnp.maximum(i * halo_blocks - 1, 0), 0)),
        tok(D_MODEL),
        const((len(POOL_WINDOWS), POOL_GROUP, POOL_GROUP)),
        const((1, POOL_WIDTH)),
        const((ATTN_WIDTH, D_MODEL)),
        const((POOL_WIDTH, D_MODEL)),
        const((D_MODEL, D_MODEL)),
        const((1, D_MODEL)),
        const((1, D_MODEL)),
    ]
    return pl.pallas_call(
        functools.partial(_merge_kernel, tm=tm),
        grid=grid,
        in_specs=in_specs,
        out_specs=pl.BlockSpec((None, tm, D_MODEL), lambda bb, i: (bb, i, 0)),
        out_shape=jax.ShapeDtypeStruct((b, s, D_MODEL), jnp.float32),
        scratch_shapes=[
            pltpu.VMEM((HALO + tm, POOL_WIDTH), jnp.float32),
            pltpu.VMEM((tm, ATTN_WIDTH), jnp.bfloat16),
            pltpu.VMEM((tm, POOL_WIDTH), jnp.bfloat16),
        ],
        compiler_params=pltpu.CompilerParams(
            dimension_semantics=("arbitrary", "arbitrary"),
            vmem_limit_bytes=56 * MIB),
        name="merge_out_ln",
    )(os_[0], os_[1], os_[2], lses[0], lses[1], lses[2], h3, h3, h3, h3, u3, u3, x,
      w_pool_bf, pool_scale2, wpa_bf, wpp_bf, w_out_bf, gamma2, beta2)


def _layer(x, w_in, b_gate, w_pool, pool_scale, w_proj_attn, w_proj_pool, w_out, ln_gamma, ln_beta):
    b, s, d = x.shape
    bf = jnp.bfloat16
    x2 = x.reshape(b * s, d).astype(bf)
    h2, u2 = _input_projection(x2, w_in.astype(bf), b_gate.reshape(1, -1))
    h3 = h2.reshape(b, s, H_WIDTH)
    u3 = u2.reshape(b, s, POOL_WIDTH)
    os_, lses = [], []
    for g, (window, dilation) in enumerate(DILATED_GROUPS):
        assert window // dilation == ATTN_BLOCK
        sub_len = s // dilation
        n_blocks = min(8, sub_len // ATTN_BLOCK)
        o, lse = _dilated_attention(h3, g, dilation, n_blocks)
        os_.append(o)
        lses.append(lse)
    return _merge(os_, lses, h3, u3, x, w_pool.astype(bf), pool_scale.reshape(1, -1),
                  w_proj_attn.astype(bf), w_proj_pool.astype(bf), w_out.astype(bf),
                  ln_gamma.reshape(1, -1), ln_beta.reshape(1, -1))


def kernel(x, w_in, b_gate, w_pool, pool_scale, w_proj_attn, w_proj_pool, w_out, ln_gamma, ln_beta):
    depth = w_in.shape[0]
    for layer in range(depth):
        x = _layer(x, w_in[layer], b_gate[layer], w_pool[layer], pool_scale[layer],
                   w_proj_attn[layer], w_proj_pool[layer], w_out[layer],
                   ln_gamma[layer], ln_beta[layer])
    return x
```

```python
import functools

import jax
import jax.numpy as jnp
from jax import lax
from jax.experimental import pallas as pl
from jax.experimental.pallas import tpu as pltpu

D_MODEL = 2048
HEAD_DIM = 128
N_HEADS = 8
ATTN_WIDTH = N_HEADS * HEAD_DIM
DILATED_GROUPS = ((128, 1), (512, 4), (2048, 16))
N_GROUPS = len(DILATED_GROUPS)
POOL_WINDOWS = (2, 4, 8, 16)
POOL_WIDTH = D_MODEL // 2
POOL_GROUP = POOL_WIDTH // len(POOL_WINDOWS)
DEEPNORM_ALPHA = 2.0 ** 0.25
LN_EPS = 1e-5
NEG_INF = -1e30

LANES = 128
MIB = 1024 * 1024

COL_TILE = 1024
W_TILE_ZATTN = 9
W_TILE_U = 10
W_TILE_ZPOOL = 11
W_TILE_GATES = 12
GZ_TILES = 6
GZ_TILE_ZATTN = 4
GZ_TILE_ZPOOL = 5

ATTN_BLOCK = 128
HALO = 16


def _proj_qkv_kernel(x_ref, w_ref, o_ref, acc_ref, *, dilation):
    if dilation == 1:
        o_ref[0] = jnp.dot(x_ref[...], w_ref[...], preferred_element_type=jnp.float32).astype(o_ref.dtype)
        return
    acc = jnp.dot(x_ref[...], w_ref[...], preferred_element_type=jnp.float32)
    for c in range(acc_ref.shape[0]):
        acc_ref[c] = acc[:, c * LANES:(c + 1) * LANES]
    sub = o_ref.shape[1]
    for r in range(dilation):
        for c in range(acc_ref.shape[0]):
            o_ref[r, :, c * LANES:(c + 1) * LANES] = (
                acc_ref[c, pl.ds(r, sub, stride=dilation), :].astype(o_ref.dtype))


def _proj_gz_kernel(x_ref, w_ref, b_ref, o_ref):
    j = pl.program_id(0)
    acc = jnp.dot(x_ref[...], w_ref[...], preferred_element_type=jnp.float32)

    @pl.when(j < GZ_TILE_ZATTN)
    def _():
        o_ref[...] = jax.nn.sigmoid(acc + b_ref[...]).astype(o_ref.dtype)

    @pl.when(j >= GZ_TILE_ZATTN)
    def _():
        o_ref[...] = (acc * jax.nn.sigmoid(acc)).astype(o_ref.dtype)


def _proj_u_kernel(x_ref, w_ref, o_ref):
    o_ref[...] = jnp.dot(x_ref[...], w_ref[...], preferred_element_type=jnp.float32)


def _project_qkv(x2, w_bf, batch, group, dilation, tm=1024):
    m = x2.shape[0]
    tiles_per_batch = m // batch // tm
    sub = tm // dilation
    return pl.pallas_call(
        functools.partial(_proj_qkv_kernel, dilation=dilation),
        grid=(3, batch, tiles_per_batch),
        in_specs=[
            pl.BlockSpec((tm, D_MODEL), lambda j, b, i: (b * tiles_per_batch + i, 0)),
            pl.BlockSpec((D_MODEL, COL_TILE), lambda j, b, i: (0, N_GROUPS * j + group)),
        ],
        out_specs=pl.BlockSpec((None, dilation, sub, COL_TILE), lambda j, b, i: (b, 0, i, j)),
        out_shape=jax.ShapeDtypeStruct((batch, dilation, m // batch // dilation, 3 * ATTN_WIDTH), jnp.bfloat16),
        scratch_shapes=[pltpu.VMEM((COL_TILE // LANES, tm, LANES), jnp.float32)],
        compiler_params=pltpu.CompilerParams(
            dimension_semantics=("arbitrary", "arbitrary", "arbitrary"),
            vmem_limit_bytes=48 * MIB),
        name=f"proj_qkv_g{group}",
    )(x2, w_bf)


def _project_rest(x2, w_bf, b_gate2, tm=1024):
    m = x2.shape[0]

    def w_tile(j):
        return jnp.where(j < GZ_TILE_ZATTN, W_TILE_GATES + j,
                         jnp.where(j == GZ_TILE_ZATTN, W_TILE_ZATTN, W_TILE_ZPOOL))

    gz = pl.pallas_call(
        _proj_gz_kernel,
        grid=(GZ_TILES, m // tm),
        in_specs=[
            pl.BlockSpec((tm, D_MODEL), lambda j, i: (i, 0)),
            pl.BlockSpec((D_MODEL, COL_TILE), lambda j, i: (0, w_tile(j))),
            pl.BlockSpec((1, COL_TILE), lambda j, i: (0, jnp.minimum(j, GZ_TILE_ZATTN - 1))),
        ],
        out_specs=pl.BlockSpec((tm, COL_TILE), lambda j, i: (i, j)),
        out_shape=jax.ShapeDtypeStruct((m, GZ_TILES * COL_TILE), jnp.bfloat16),
        compiler_params=pltpu.CompilerParams(
            dimension_semantics=("arbitrary", "arbitrary"),
            vmem_limit_bytes=48 * MIB),
        name="proj_gates_z",
    )(x2, w_bf, b_gate2)
    u = pl.pallas_call(
        _proj_u_kernel,
        grid=(m // tm,),
        in_specs=[
            pl.BlockSpec((tm, D_MODEL), lambda i: (i, 0)),
            pl.BlockSpec((D_MODEL, COL_TILE), lambda i: (0, W_TILE_U)),
        ],
        out_specs=pl.BlockSpec((tm, COL_TILE), lambda i: (i, 0)),
        out_shape=jax.ShapeDtypeStruct((m, POOL_WIDTH), jnp.float32),
        compiler_params=pltpu.CompilerParams(
            dimension_semantics=("arbitrary",),
            vmem_limit_bytes=48 * MIB),
        name="proj_u",
    )(x2, w_bf)
    return gz, u


def _attn_kernel(q_ref, k_ref, v_ref, o_ref, lse_ref, kc_ref, vc_ref, *, n_blocks, n_chunks):
    chunk = pl.program_id(2)
    bq = ATTN_BLOCK
    scale = HEAD_DIM ** -0.5

    @pl.when(chunk == 0)
    def _():
        kc_ref[...] = jnp.zeros_like(kc_ref)
        vc_ref[...] = jnp.zeros_like(vc_ref)

    row = lax.broadcasted_iota(jnp.int32, (bq, 2 * bq), 0)
    col = lax.broadcasted_iota(jnp.int32, (bq, 2 * bq), 1)
    dist = bq + row - col
    band = jnp.logical_and(dist >= 0, dist <= bq)
    band_first = jnp.logical_and(band, col >= jnp.where(chunk > 0, 0, bq))
    lane = lax.broadcasted_iota(jnp.int32, (bq, LANES), 1)

    def block(r0, kcat_of, vcat_of, mask):
        lse_tile = jnp.zeros((bq, LANES), jnp.float32)
        for h in range(N_HEADS):
            hs = slice(h * HEAD_DIM, (h + 1) * HEAD_DIM)
            q = q_ref[pl.ds(r0, bq), hs]
            kcat = kcat_of(hs)
            vcat = vcat_of(hs)
            s = lax.dot_general(q, kcat, (((1,), (1,)), ((), ())),
                                preferred_element_type=jnp.float32) * scale
            s = jnp.where(mask, s, NEG_INF)
            m = jnp.max(s, axis=-1, keepdims=True)
            e = jnp.exp(s - m)
            den = jnp.sum(e, axis=-1, keepdims=True)
            o = jnp.dot(e.astype(vcat.dtype), vcat, preferred_element_type=jnp.float32)
            o_ref[pl.ds(r0, bq), hs] = (o / den).astype(o_ref.dtype)
            lse_tile = jnp.where(lane == h, m + jnp.log(den), lse_tile)
        lse_ref[pl.ds(r0, bq), :] = lse_tile

    block(0,
          lambda hs: jnp.concatenate([kc_ref[:, hs], k_ref[0:bq, hs]], axis=0),
          lambda hs: jnp.concatenate([vc_ref[:, hs], v_ref[0:bq, hs]], axis=0),
          band_first)

    def body(blk, carry):
        r0 = pl.multiple_of(blk * bq, bq)
        rk = pl.multiple_of(blk * bq - bq, bq)
        block(r0,
              lambda hs: k_ref[pl.ds(rk, 2 * bq), hs],
              lambda hs: v_ref[pl.ds(rk, 2 * bq), hs],
              band)
        return carry

    if n_blocks > 1:
        lax.fori_loop(1, n_blocks, body, 0)

    if n_chunks > 1:
        rl = (n_blocks - 1) * bq
        kc_ref[...] = k_ref[rl:rl + bq, :]
        vc_ref[...] = v_ref[rl:rl + bq, :]


def _dilated_attention(qkv, group, n_blocks):
    b, dilation, sub_len, _ = qkv.shape
    rows = n_blocks * ATTN_BLOCK
    n_chunks = sub_len // rows

    def in_spec(tile):
        return pl.BlockSpec((None, None, rows, COL_TILE), lambda bb, r, c: (bb, r, c, tile))

    return pl.pallas_call(
        functools.partial(_attn_kernel, n_blocks=n_blocks, n_chunks=n_chunks),
        grid=(b, dilation, n_chunks),
        in_specs=[in_spec(0), in_spec(1), in_spec(2)],
        out_specs=[
            pl.BlockSpec((None, None, rows, ATTN_WIDTH), lambda bb, r, c: (bb, r, c, 0)),
            pl.BlockSpec((None, None, rows, LANES), lambda bb, r, c: (bb, r, c, 0)),
        ],
        out_shape=[
            jax.ShapeDtypeStruct((b, dilation, sub_len, ATTN_WIDTH), jnp.bfloat16),
            jax.ShapeDtypeStruct((b, dilation, sub_len, LANES), jnp.float32),
        ],
        scratch_shapes=[
            pltpu.VMEM((ATTN_BLOCK, ATTN_WIDTH), jnp.bfloat16),
            pltpu.VMEM((ATTN_BLOCK, ATTN_WIDTH), jnp.bfloat16),
        ],
        compiler_params=pltpu.CompilerParams(
            dimension_semantics=("arbitrary", "arbitrary", "arbitrary"),
            vmem_limit_bytes=40 * MIB),
        name=f"dilated_attn_g{group}",
    )(qkv, qkv, qkv)


def _merge_kernel(o1_ref, o2_ref, o3_ref, l1_ref, l2_ref, l3_ref, za_ref, zp_ref, g_attn_ref, g_pool_ref,
                  u_ref, halo_ref, x_ref, w_pool_ref, pool_scale_ref, wpa_ref, wpp_ref, w_out_ref,
                  gamma_ref, beta_ref, out_ref, ext_ref, ya_ref, yp_ref, on2_ref, on3_ref, ln2_ref, ln3_ref,
                  *, tm):
    i = pl.program_id(1)

    def to_token_order(src_ref, dst_ref):
        dilation, sub = src_ref.shape[0], src_ref.shape[1]
        for r in range(dilation):
            for c in range(dst_ref.shape[0]):
                dst_ref[c, pl.ds(r, sub, stride=dilation), :] = (
                    src_ref[r, :, c * LANES:(c + 1) * LANES].astype(dst_ref.dtype))

    to_token_order(o2_ref, on2_ref)
    to_token_order(o3_ref, on3_ref)
    to_token_order(l2_ref, ln2_ref)
    to_token_order(l3_ref, ln3_ref)

    l1 = l1_ref[...]
    l2 = ln2_ref[0]
    l3 = ln3_ref[0]
    mx = jnp.maximum(jnp.maximum(l1, l2), l3)
    e1 = jnp.exp(l1 - mx)
    e2 = jnp.exp(l2 - mx)
    e3 = jnp.exp(l3 - mx)
    tot = e1 + e2 + e3
    w1 = e1 / tot
    w2 = e2 / tot
    w3 = e3 / tot
    for h in range(N_HEADS):
        hs = slice(h * HEAD_DIM, (h + 1) * HEAD_DIM)
        o = (w1[:, h:h + 1] * o1_ref[:, hs].astype(jnp.float32)
             + w2[:, h:h + 1] * on2_ref[h]
             + w3[:, h:h + 1] * on3_ref[h])
        ya_ref[:, hs] = (o * za_ref[:, hs].astype(jnp.float32)).astype(ya_ref.dtype)

    ext_ref[HALO:, :] = u_ref[...]

    @pl.when(i == 0)
    def _():
        ext_ref[0:HALO, :] = jnp.zeros((HALO, POOL_WIDTH), jnp.float32)

    @pl.when(i > 0)
    def _():
        ext_ref[0:HALO, :] = halo_ref[...]

    pos = i * tm + lax.broadcasted_iota(jnp.int32, (tm, 1), 0)
    for g, w in enumerate(POOL_WINDOWS):
        cs = slice(g * POOL_GROUP, (g + 1) * POOL_GROUP)
        tok = ext_ref[HALO:HALO + tm, cs]
        acc = tok
        for k in range(1, w):
            acc = acc + ext_ref[HALO - k:HALO - k + tm, cs]
        cnt = jnp.minimum(pos + 1, w).astype(jnp.float32)
        p = acc / cnt - tok
        y = jnp.dot(p.astype(jnp.bfloat16), w_pool_ref[g], preferred_element_type=jnp.float32)
        y = y * pool_scale_ref[:, cs] * zp_ref[:, cs].astype(jnp.float32)
        yp_ref[:, cs] = y.astype(yp_ref.dtype)

    pa = jnp.dot(ya_ref[...], wpa_ref[...], preferred_element_type=jnp.float32)
    pp = jnp.dot(yp_ref[...], wpp_ref[...], preferred_element_type=jnp.float32)
    merged = (g_attn_ref[...].astype(jnp.float32) * pa + g_pool_ref[...].astype(jnp.float32) * pp)
    out = jnp.dot(merged.astype(jnp.bfloat16), w_out_ref[...], preferred_element_type=jnp.float32)
    r = DEEPNORM_ALPHA * x_ref[...] + out
    mu = jnp.mean(r, axis=-1, keepdims=True)
    rc = r - mu
    var = jnp.mean(rc * rc, axis=-1, keepdims=True)
    out_ref[...] = rc * lax.rsqrt(var + LN_EPS) * gamma_ref[...] + beta_ref[...]


def _merge(os_, lses, gz3, u3, x, w_pool_bf, pool_scale2, wpa_bf, wpp_bf, w_out_bf, gamma2, beta2, tm=256):
    b, s, _ = x.shape
    grid = (b, s // tm)

    def tok(width, tile=0):
        return pl.BlockSpec((None, tm, width), lambda bb, i: (bb, i, tile))

    def grouped(arr):
        dilation, width = arr.shape[1], arr.shape[3]
        if dilation == 1:
            return pl.BlockSpec((None, None, tm, width), lambda bb, i: (bb, 0, i, 0))
        return pl.BlockSpec((None, dilation, tm // dilation, width), lambda bb, i: (bb, 0, i, 0))

    def const(shape):
        nd = len(shape)
        return pl.BlockSpec(shape, lambda bb, i: (0,) * nd, pipeline_mode=pl.Buffered(1))

    halo_blocks = tm // HALO
    in_specs = [
        grouped(os_[0]), grouped(os_[1]), grouped(os_[2]),
        grouped(lses[0]), grouped(lses[1]), grouped(lses[2]),
        tok(COL_TILE, GZ_TILE_ZATTN), tok(COL_TILE, GZ_TILE_ZPOOL),
        tok(D_MODEL, 0), tok(D_MODEL, 1),
        tok(POOL_WIDTH),
        pl.BlockSpec((None, HALO, POOL_WIDTH),
                     lambda bb, i: (bb, jnp.maximum(i * halo_blocks - 1, 0), 0)),
        tok(D_MODEL),
        const((len(POOL_WINDOWS), POOL_GROUP, POOL_GROUP)),
        const((1, POOL_WIDTH)),
        const((ATTN_WIDTH, D_MODEL)),
        const((POOL_WIDTH, D_MODEL)),
        const((D_MODEL, D_MODEL)),
        const((1, D_MODEL)),
        const((1, D_MODEL)),
    ]
    return pl.pallas_call(
        functools.partial(_merge_kernel, tm=tm),
        grid=grid,
        in_specs=in_specs,
        out_specs=pl.BlockSpec((None, tm, D_MODEL), lambda bb, i: (bb, i, 0)),
        out_shape=jax.ShapeDtypeStruct((b, s, D_MODEL), jnp.float32),
        scratch_shapes=[
            pltpu.VMEM((HALO + tm, POOL_WIDTH), jnp.float32),
            pltpu.VMEM((tm, ATTN_WIDTH), jnp.bfloat16),
            pltpu.VMEM((tm, POOL_WIDTH), jnp.bfloat16),
            pltpu.VMEM((N_HEADS, tm, HEAD_DIM), jnp.float32),
            pltpu.VMEM((N_HEADS, tm, HEAD_DIM), jnp.float32),
            pltpu.VMEM((1, tm, LANES), jnp.float32),
            pltpu.VMEM((1, tm, LANES), jnp.float32),
        ],
        compiler_params=pltpu.CompilerParams(
            dimension_semantics=("arbitrary", "arbitrary"),
            vmem_limit_bytes=56 * MIB),
        name="merge_out_ln",
    )(os_[0], os_[1], os_[2], lses[0], lses[1], lses[2], gz3, gz3, gz3, gz3, u3, u3, x,
      w_pool_bf, pool_scale2, wpa_bf, wpp_bf, w_out_bf, gamma2, beta2)


def _layer(x, w_in, b_gate, w_pool, pool_scale, w_proj_attn, w_proj_pool, w_out, ln_gamma, ln_beta):
    b, s, d = x.shape
    bf = jnp.bfloat16
    x2 = x.reshape(b * s, d).astype(bf)
    w_bf = w_in.astype(bf)
    os_, lses = [], []
    for g, (window, dilation) in enumerate(DILATED_GROUPS):
        assert window // dilation == ATTN_BLOCK
        qkv = _project_qkv(x2, w_bf, b, g, dilation)
        n_blocks = min(8, s // dilation // ATTN_BLOCK)
        o, lse = _dilated_attention(qkv, g, n_blocks)
        os_.append(o)
        lses.append(lse)
    gz2, u2 = _project_rest(x2, w_bf, b_gate.reshape(1, -1))
    gz3 = gz2.reshape(b, s, GZ_TILES * COL_TILE)
    u3 = u2.reshape(b, s, POOL_WIDTH)
    return _merge(os_, lses, gz3, u3, x, w_pool.astype(bf), pool_scale.reshape(1, -1),
                  w_proj_attn.astype(bf), w_proj_pool.astype(bf), w_out.astype(bf),
                  ln_gamma.reshape(1, -1), ln_beta.reshape(1, -1))


def kernel(x, w_in, b_gate, w_pool, pool_scale, w_proj_attn, w_proj_pool, w_out, ln_gamma, ln_beta):
    depth = w_in.shape[0]
    for layer in range(depth):
        x = _layer(x, w_in[layer], b_gate[layer], w_pool[layer], pool_scale[layer],
                   w_proj_attn[layer], w_proj_pool[layer], w_out[layer],
                   ln_gamma[layer], ln_beta[layer])
    return x
```

```python
import functools

import jax
import jax.numpy as jnp
from jax import lax
from jax.experimental import pallas as pl
from jax.experimental.pallas import tpu as pltpu

D_MODEL = 2048
HEAD_DIM = 128
N_HEADS = 8
ATTN_WIDTH = N_HEADS * HEAD_DIM
DILATED_GROUPS = ((128, 1), (512, 4), (2048, 16))
N_GROUPS = len(DILATED_GROUPS)
POOL_WINDOWS = (2, 4, 8, 16)
POOL_WIDTH = D_MODEL // 2
POOL_GROUP = POOL_WIDTH // len(POOL_WINDOWS)
DEEPNORM_ALPHA = 2.0 ** 0.25
LN_EPS = 1e-5
NEG_INF = -1e30
LOG2_E = 1.4426950408889634

LANES = 128
MIB = 1024 * 1024
SUBLANE_STRIDE = 4
PROJ_VMEM_LIMIT = 54 * MIB

COL_TILE = 1024
W_TILE_ZATTN = 9
W_TILE_U = 10
W_TILE_ZPOOL = 11
W_TILE_GATES = 12
GZ_TILES = 6
GZ_TILE_ZATTN = 4
GZ_TILE_ZPOOL = 5

ATTN_BLOCK = 128
HALO = 16


def _cast_weight_tile(first_step, w_ref, wbf_ref):
    @pl.when(first_step)
    def _():
        wbf_ref[...] = w_ref[...].astype(wbf_ref.dtype)


def _proj_qkv_kernel(x_ref, w_ref, o_ref, wbf_ref, *scratch, dilation):
    _cast_weight_tile(jnp.logical_and(pl.program_id(1) == 0, pl.program_id(2) == 0), w_ref, wbf_ref)
    acc = jnp.dot(x_ref[...], wbf_ref[...], preferred_element_type=jnp.float32)
    if dilation == 1:
        o_ref[0] = acc.astype(o_ref.dtype)
        return
    acc_ref = scratch[0]
    slabs, tm = acc_ref.shape[0], acc_ref.shape[1]
    for c in range(slabs):
        acc_ref[c] = acc[:, c * LANES:(c + 1) * LANES]
    if dilation == SUBLANE_STRIDE:
        for r in range(dilation):
            for c in range(slabs):
                o_ref[r, :, c * LANES:(c + 1) * LANES] = (
                    acc_ref[c, pl.ds(r, tm // dilation, stride=dilation), :].astype(o_ref.dtype))
        return
    assert dilation == SUBLANE_STRIDE * SUBLANE_STRIDE
    mid_ref = scratch[1]
    quarter = tm // SUBLANE_STRIDE
    for r0 in range(SUBLANE_STRIDE):
        for c in range(slabs):
            mid_ref[c, pl.ds(r0 * quarter, quarter), :] = acc_ref[c, pl.ds(r0, quarter, stride=SUBLANE_STRIDE), :]
    for r0 in range(SUBLANE_STRIDE):
        for r1 in range(SUBLANE_STRIDE):
            for c in range(slabs):
                o_ref[SUBLANE_STRIDE * r1 + r0, :, c * LANES:(c + 1) * LANES] = (
                    mid_ref[c, pl.ds(r0 * quarter + r1, tm // dilation, stride=SUBLANE_STRIDE), :]
                    .astype(o_ref.dtype))


def _proj_gz_kernel(x_ref, w_ref, b_ref, o_ref, wbf_ref):
    j = pl.program_id(0)
    _cast_weight_tile(pl.program_id(1) == 0, w_ref, wbf_ref)
    is_gate = j < GZ_TILE_ZATTN
    acc = jnp.dot(x_ref[...], wbf_ref[...], preferred_element_type=jnp.float32)
    t = acc + jnp.where(is_gate, b_ref[...], 0.0)
    sg = 0.5 * jnp.tanh(0.5 * t) + 0.5
    o_ref[...] = (sg * jnp.where(is_gate, 1.0, t)).astype(o_ref.dtype)


def _proj_u_kernel(x_ref, w_ref, u_ref, xbf_ref, wbf_ref):
    _cast_weight_tile(pl.program_id(0) == 0, w_ref, wbf_ref)
    xb = x_ref[...].astype(xbf_ref.dtype)
    xbf_ref[...] = xb
    u_ref[...] = jnp.dot(xb, wbf_ref[...], preferred_element_type=jnp.float32)


def _project_u(x2, w_in, tm=512):
    m = x2.shape[0]
    return pl.pallas_call(
        _proj_u_kernel,
        grid=(m // tm,),
        in_specs=[
            pl.BlockSpec((tm, D_MODEL), lambda i: (i, 0)),
            pl.BlockSpec((D_MODEL, COL_TILE), lambda i: (0, W_TILE_U), pipeline_mode=pl.Buffered(1)),
        ],
        out_specs=[
            pl.BlockSpec((tm, COL_TILE), lambda i: (i, 0)),
            pl.BlockSpec((tm, D_MODEL), lambda i: (i, 0)),
        ],
        out_shape=[
            jax.ShapeDtypeStruct((m, POOL_WIDTH), jnp.float32),
            jax.ShapeDtypeStruct((m, D_MODEL), jnp.bfloat16),
        ],
        scratch_shapes=[pltpu.VMEM((D_MODEL, COL_TILE), jnp.bfloat16)],
        compiler_params=pltpu.CompilerParams(
            dimension_semantics=("arbitrary",),
            vmem_limit_bytes=PROJ_VMEM_LIMIT),
        name="proj_u",
    )(x2, w_in)


def _project_qkv(x2, w_in, batch, group, dilation, tm=1024):
    m = x2.shape[0]
    tiles_per_batch = m // batch // tm
    sub = tm // dilation
    n_row_scratch = {1: 0, SUBLANE_STRIDE: 1, SUBLANE_STRIDE * SUBLANE_STRIDE: 2}[dilation]
    return pl.pallas_call(
        functools.partial(_proj_qkv_kernel, dilation=dilation),
        grid=(3, batch, tiles_per_batch),
        in_specs=[
            pl.BlockSpec((tm, D_MODEL), lambda j, b, i: (b * tiles_per_batch + i, 0)),
            pl.BlockSpec((D_MODEL, COL_TILE), lambda j, b, i: (0, N_GROUPS * j + group)),
        ],
        out_specs=pl.BlockSpec((None, dilation, sub, COL_TILE), lambda j, b, i: (b, 0, i, j)),
        out_shape=jax.ShapeDtypeStruct((batch, dilation, m // batch // dilation, 3 * ATTN_WIDTH), jnp.bfloat16),
        scratch_shapes=[pltpu.VMEM((D_MODEL, COL_TILE), jnp.bfloat16)]
        + [pltpu.VMEM((COL_TILE // LANES, tm, LANES), jnp.float32)] * n_row_scratch,
        compiler_params=pltpu.CompilerParams(
            dimension_semantics=("arbitrary", "arbitrary", "arbitrary"),
            vmem_limit_bytes=PROJ_VMEM_LIMIT),
        name=f"proj_qkv_g{group}",
    )(x2, w_in)


def _project_gates_z(x2, w_in, b_gate2, tm=1024):
    m = x2.shape[0]

    def w_tile(j):
        return jnp.where(j < GZ_TILE_ZATTN, W_TILE_GATES + j,
                         jnp.where(j == GZ_TILE_ZATTN, W_TILE_ZATTN, W_TILE_ZPOOL))

    return pl.pallas_call(
        _proj_gz_kernel,
        grid=(GZ_TILES, m // tm),
        in_specs=[
            pl.BlockSpec((tm, D_MODEL), lambda j, i: (i, 0)),
            pl.BlockSpec((D_MODEL, COL_TILE), lambda j, i: (0, w_tile(j))),
            pl.BlockSpec((1, COL_TILE), lambda j, i: (0, jnp.minimum(j, GZ_TILE_ZATTN - 1))),
        ],
        out_specs=pl.BlockSpec((tm, COL_TILE), lambda j, i: (i, j)),
        out_shape=jax.ShapeDtypeStruct((m, GZ_TILES * COL_TILE), jnp.bfloat16),
        scratch_shapes=[pltpu.VMEM((D_MODEL, COL_TILE), jnp.bfloat16)],
        compiler_params=pltpu.CompilerParams(
            dimension_semantics=("arbitrary", "arbitrary"),
            vmem_limit_bytes=PROJ_VMEM_LIMIT),
        name="proj_gates_z",
    )(x2, w_in, b_gate2)


def _attn_kernel(q_ref, k_ref, v_ref, o_ref, lse_ref, kc_ref, vc_ref, *, n_blocks, n_chunks):
    chunk = pl.program_id(2)
    bq = ATTN_BLOCK
    scale = HEAD_DIM ** -0.5

    @pl.when(chunk == 0)
    def _():
        kc_ref[...] = jnp.zeros_like(kc_ref)
        vc_ref[...] = jnp.zeros_like(vc_ref)

    row = lax.broadcasted_iota(jnp.int32, (bq, 2 * bq), 0)
    col = lax.broadcasted_iota(jnp.int32, (bq, 2 * bq), 1)
    dist = bq + row - col
    band = jnp.logical_and(dist >= 0, dist <= bq)
    band_first = jnp.logical_and(band, col >= jnp.where(chunk > 0, 0, bq))
    lane = lax.broadcasted_iota(jnp.int32, (bq, LANES), 1)

    def block(r0, kcat_of, vcat_of, mask):
        lse_tile = jnp.zeros((bq, LANES), jnp.float32)
        for h in range(N_HEADS):
            hs = slice(h * HEAD_DIM, (h + 1) * HEAD_DIM)
            q = q_ref[pl.ds(r0, bq), hs]
            kcat = kcat_of(hs)
            vcat = vcat_of(hs)
            s = lax.dot_general(q, kcat, (((1,), (1,)), ((), ())), preferred_element_type=jnp.float32)
            s = jnp.where(mask, s, NEG_INF)
            m = jnp.max(s, axis=-1, keepdims=True)
            e = jnp.exp2((s - m) * (scale * LOG2_E))
            den = jnp.sum(e, axis=-1, keepdims=True)
            o = jnp.dot(e.astype(vcat.dtype), vcat, preferred_element_type=jnp.float32)
            o_ref[pl.ds(r0, bq), hs] = (o / den).astype(o_ref.dtype)
            lse_tile = jnp.where(lane == h, m * scale + jnp.log(den), lse_tile)
        lse_ref[pl.ds(r0, bq), :] = lse_tile

    block(0,
          lambda hs: jnp.concatenate([kc_ref[:, hs], k_ref[0:bq, hs]], axis=0),
          lambda hs: jnp.concatenate([vc_ref[:, hs], v_ref[0:bq, hs]], axis=0),
          band_first)

    def body(blk, carry):
        r0 = pl.multiple_of(blk * bq, bq)
        rk = pl.multiple_of(blk * bq - bq, bq)
        block(r0,
              lambda hs: k_ref[pl.ds(rk, 2 * bq), hs],
              lambda hs: v_ref[pl.ds(rk, 2 * bq), hs],
              band)
        return carry

    if n_blocks > 1:
        lax.fori_loop(1, n_blocks, body, 0)

    if n_chunks > 1:
        rl = (n_blocks - 1) * bq
        kc_ref[...] = k_ref[rl:rl + bq, :]
        vc_ref[...] = v_ref[rl:rl + bq, :]


def _dilated_attention(qkv, group, n_blocks):
    b, dilation, sub_len, _ = qkv.shape
    rows = n_blocks * ATTN_BLOCK
    n_chunks = sub_len // rows

    def in_spec(tile):
        return pl.BlockSpec((None, None, rows, COL_TILE), lambda bb, r, c: (bb, r, c, tile))

    return pl.pallas_call(
        functools.partial(_attn_kernel, n_blocks=n_blocks, n_chunks=n_chunks),
        grid=(b, dilation, n_chunks),
        in_specs=[in_spec(0), in_spec(1), in_spec(2)],
        out_specs=[
            pl.BlockSpec((None, None, rows, ATTN_WIDTH), lambda bb, r, c: (bb, r, c, 0)),
            pl.BlockSpec((None, None, rows, LANES), lambda bb, r, c: (bb, r, c, 0)),
        ],
        out_shape=[
            jax.ShapeDtypeStruct((b, dilation, sub_len, ATTN_WIDTH), jnp.bfloat16),
            jax.ShapeDtypeStruct((b, dilation, sub_len, LANES), jnp.float32),
        ],
        scratch_shapes=[
            pltpu.VMEM((ATTN_BLOCK, ATTN_WIDTH), jnp.bfloat16),
            pltpu.VMEM((ATTN_BLOCK, ATTN_WIDTH), jnp.bfloat16),
        ],
        compiler_params=pltpu.CompilerParams(
            dimension_semantics=("arbitrary", "arbitrary", "arbitrary"),
            vmem_limit_bytes=40 * MIB),
        name=f"dilated_attn_g{group}",
    )(qkv, qkv, qkv)


def _merge_kernel(o1_ref, o2_ref, o3_ref, l1_ref, l2_ref, l3_ref, za_ref, zp_ref, g_attn_ref, g_pool_ref,
                  u_ref, halo_ref, x_ref, w_pool_ref, pool_scale_ref, wpa_ref, wpp_ref, w_out_ref,
                  gamma_ref, beta_ref, out_ref, ext_ref, ya_ref, yp_ref, on2_ref, on3_ref, ln2_ref, ln3_ref,
                  *, tm):
    i = pl.program_id(1)

    def to_token_order(src_ref, dst_ref):
        dilation, sub = src_ref.shape[0], src_ref.shape[1]
        for r in range(dilation):
            for c in range(dst_ref.shape[0]):
                dst_ref[c, pl.ds(r, sub, stride=dilation), :] = (
                    src_ref[r, :, c * LANES:(c + 1) * LANES].astype(dst_ref.dtype))

    to_token_order(o2_ref, on2_ref)
    to_token_order(o3_ref, on3_ref)
    to_token_order(l2_ref, ln2_ref)
    to_token_order(l3_ref, ln3_ref)

    l1 = l1_ref[...]
    l2 = ln2_ref[0]
    l3 = ln3_ref[0]
    mx = jnp.maximum(jnp.maximum(l1, l2), l3)
    e1 = jnp.exp(l1 - mx)
    e2 = jnp.exp(l2 - mx)
    e3 = jnp.exp(l3 - mx)
    tot = e1 + e2 + e3
    w1 = e1 / tot
    w2 = e2 / tot
    w3 = e3 / tot
    for h in range(N_HEADS):
        hs = slice(h * HEAD_DIM, (h + 1) * HEAD_DIM)
        o = (w1[:, h:h + 1] * o1_ref[:, hs].astype(jnp.float32)
             + w2[:, h:h + 1] * on2_ref[h]
             + w3[:, h:h + 1] * on3_ref[h])
        ya_ref[:, hs] = (o * za_ref[:, hs].astype(jnp.float32)).astype(ya_ref.dtype)

    ext_ref[HALO:, :] = u_ref[...]

    @pl.when(i == 0)
    def _():
        ext_ref[0:HALO, :] = jnp.zeros((HALO, POOL_WIDTH), jnp.float32)

    @pl.when(i > 0)
    def _():
        ext_ref[0:HALO, :] = halo_ref[...]

    pos = i * tm + lax.broadcasted_iota(jnp.int32, (tm, 1), 0)
    for g, w in enumerate(POOL_WINDOWS):
        cs = slice(g * POOL_GROUP, (g + 1) * POOL_GROUP)
        tok = ext_ref[HALO:HALO + tm, cs]
        acc = tok
        for k in range(1, w):
            acc = acc + ext_ref[HALO - k:HALO - k + tm, cs]
        cnt = jnp.minimum(pos + 1, w).astype(jnp.float32)
        p = acc / cnt - tok
        y = jnp.dot(p.astype(jnp.bfloat16), w_pool_ref[g], preferred_element_type=jnp.float32)
        y = y * pool_scale_ref[:, cs] * zp_ref[:, cs].astype(jnp.float32)
        yp_ref[:, cs] = y.astype(yp_ref.dtype)

    pa = jnp.dot(ya_ref[...], wpa_ref[...], preferred_element_type=jnp.float32)
    pp = jnp.dot(yp_ref[...], wpp_ref[...], preferred_element_type=jnp.float32)
    merged = (g_attn_ref[...].astype(jnp.float32) * pa + g_pool_ref[...].astype(jnp.float32) * pp)
    out = jnp.dot(merged.astype(jnp.bfloat16), w_out_ref[...], preferred_element_type=jnp.float32)
    r = DEEPNORM_ALPHA * x_ref[...] + out
    mu = jnp.mean(r, axis=-1, keepdims=True)
    rc = r - mu
    var = jnp.mean(rc * rc, axis=-1, keepdims=True)
    out_ref[...] = rc * lax.rsqrt(var + LN_EPS) * gamma_ref[...] + beta_ref[...]


def _merge(os_, lses, gz3, u3, x, w_pool_bf, pool_scale2, wpa_bf, wpp_bf, w_out_bf, gamma2, beta2, tm=256):
    b, s, _ = x.shape
    grid = (b, s // tm)

    def tok(width, tile=0):
        return pl.BlockSpec((None, tm, width), lambda bb, i: (bb, i, tile))

    def grouped(arr):
        dilation, width = arr.shape[1], arr.shape[3]
        if dilation == 1:
            return pl.BlockSpec((None, None, tm, width), lambda bb, i: (bb, 0, i, 0))
        return pl.BlockSpec((None, dilation, tm // dilation, width), lambda bb, i: (bb, 0, i, 0))

    def const(shape):
        nd = len(shape)
        return pl.BlockSpec(shape, lambda bb, i: (0,) * nd, pipeline_mode=pl.Buffered(1))

    halo_blocks = tm // HALO
    in_specs = [
        grouped(os_[0]), grouped(os_[1]), grouped(os_[2]),
        grouped(lses[0]), grouped(lses[1]), grouped(lses[2]),
        tok(COL_TILE, GZ_TILE_ZATTN), tok(COL_TILE, GZ_TILE_ZPOOL),
        tok(D_MODEL, 0), tok(D_MODEL, 1),
        tok(POOL_WIDTH),
        pl.BlockSpec((None, HALO, POOL_WIDTH),
                     lambda bb, i: (bb, jnp.maximum(i * halo_blocks - 1, 0), 0)),
        tok(D_MODEL),
        const((len(POOL_WINDOWS), POOL_GROUP, POOL_GROUP)),
        const((1, POOL_WIDTH)),
        const((ATTN_WIDTH, D_MODEL)),
        const((POOL_WIDTH, D_MODEL)),
        const((D_MODEL, D_MODEL)),
        const((1, D_MODEL)),
        const((1, D_MODEL)),
    ]
    return pl.pallas_call(
        functools.partial(_merge_kernel, tm=tm),
        grid=grid,
        in_specs=in_specs,
        out_specs=pl.BlockSpec((None, tm, D_MODEL), lambda bb, i: (bb, i, 0)),
        out_shape=jax.ShapeDtypeStruct((b, s, D_MODEL), jnp.float32),
        scratch_shapes=[
            pltpu.VMEM((HALO + tm, POOL_WIDTH), jnp.float32),
            pltpu.VMEM((tm, ATTN_WIDTH), jnp.bfloat16),
            pltpu.VMEM((tm, POOL_WIDTH), jnp.bfloat16),
            pltpu.VMEM((N_HEADS, tm, HEAD_DIM), jnp.float32),
            pltpu.VMEM((N_HEADS, tm, HEAD_DIM), jnp.float32),
            pltpu.VMEM((1, tm, LANES), jnp.float32),
            pltpu.VMEM((1, tm, LANES), jnp.float32),
        ],
        compiler_params=pltpu.CompilerParams(
            dimension_semantics=("arbitrary", "arbitrary"),
            vmem_limit_bytes=56 * MIB),
        name="merge_out_ln",
    )(os_[0], os_[1], os_[2], lses[0], lses[1], lses[2], gz3, gz3, gz3, gz3, u3, u3, x,
      w_pool_bf, pool_scale2, wpa_bf, wpp_bf, w_out_bf, gamma2, beta2)


def _layer(x, w_in, b_gate, w_pool, pool_scale, w_proj_attn, w_proj_pool, w_out, ln_gamma, ln_beta):
    b, s, d = x.shape
    bf = jnp.bfloat16
    u2, x2 = _project_u(x.reshape(b * s, d), w_in)
    os_, lses = [], []
    for g, (window, dilation) in enumerate(DILATED_GROUPS):
        assert window // dilation == ATTN_BLOCK
        qkv = _project_qkv(x2, w_in, b, g, dilation)
        n_blocks = min(8, s // dilation // ATTN_BLOCK)
        o, lse = _dilated_attention(qkv, g, n_blocks)
        os_.append(o)
        lses.append(lse)
    gz2 = _project_gates_z(x2, w_in, b_gate.reshape(1, -1))
    gz3 = gz2.reshape(b, s, GZ_TILES * COL_TILE)
    u3 = u2.reshape(b, s, POOL_WIDTH)
    return _merge(os_, lses, gz3, u3, x, w_pool.astype(bf), pool_scale.reshape(1, -1),
                  w_proj_attn.astype(bf), w_proj_pool.astype(bf), w_out.astype(bf),
                  ln_gamma.reshape(1, -1), ln_beta.reshape(1, -1))


def kernel(x, w_in, b_gate, w_pool, pool_scale, w_proj_attn, w_proj_pool, w_out, ln_gamma, ln_beta):
    depth = w_in.shape[0]
    for layer in range(depth):
        x = _layer(x, w_in[layer], b_gate[layer], w_pool[layer], pool_scale[layer],
                   w_proj_attn[layer], w_proj_pool[layer], w_out[layer],
                   ln_gamma[layer], ln_beta[layer])
    return x
```

```python
import functools

import jax
import jax.numpy as jnp
from jax import lax
from jax.experimental import pallas as pl
from jax.experimental.pallas import tpu as pltpu

D_MODEL = 2048
HEAD_DIM = 128
N_HEADS = 8
ATTN_WIDTH = N_HEADS * HEAD_DIM
DILATED_GROUPS = ((128, 1), (512, 4), (2048, 16))
N_GROUPS = len(DILATED_GROUPS)
POOL_WINDOWS = (2, 4, 8, 16)
POOL_WIDTH = D_MODEL // 2
POOL_GROUP = POOL_WIDTH // len(POOL_WINDOWS)
DEEPNORM_ALPHA = 2.0 ** 0.25
LN_EPS = 1e-5
NEG_INF = -1e30
LOG2_E = 1.4426950408889634

LANES = 128
MIB = 1024 * 1024
SUBLANE_STRIDE = 4
PROJ_VMEM_LIMIT = 54 * MIB

COL_TILE = 1024
W_TILE_ZATTN = 9
W_TILE_U = 10
W_TILE_ZPOOL = 11
W_TILE_GATES = 12
GZ_TILES = 6
GZ_TILE_ZATTN = 4
GZ_TILE_ZPOOL = 5

ATTN_BLOCK = 128
HALO = 32


def _cast_weight_tile(first_step, w_ref, wbf_ref):
    @pl.when(first_step)
    def _():
        wbf_ref[...] = w_ref[...].astype(wbf_ref.dtype)


def _proj_qkv_kernel(x_ref, w_ref, o_ref, wbf_ref, *scratch, dilation):
    _cast_weight_tile(jnp.logical_and(pl.program_id(1) == 0, pl.program_id(2) == 0), w_ref, wbf_ref)
    acc = jnp.dot(x_ref[...], wbf_ref[...], preferred_element_type=jnp.float32)
    if dilation == 1:
        o_ref[0] = acc.astype(o_ref.dtype)
        return
    acc_ref = scratch[0]
    slabs, tm = acc_ref.shape[0], acc_ref.shape[1]
    for c in range(slabs):
        acc_ref[c] = acc[:, c * LANES:(c + 1) * LANES]
    if dilation == SUBLANE_STRIDE:
        for r in range(dilation):
            for c in range(slabs):
                o_ref[r, :, c * LANES:(c + 1) * LANES] = (
                    acc_ref[c, pl.ds(r, tm // dilation, stride=dilation), :].astype(o_ref.dtype))
        return
    assert dilation == SUBLANE_STRIDE * SUBLANE_STRIDE
    mid_ref = scratch[1]
    quarter = tm // SUBLANE_STRIDE
    for r0 in range(SUBLANE_STRIDE):
        for c in range(slabs):
            mid_ref[c, pl.ds(r0 * quarter, quarter), :] = acc_ref[c, pl.ds(r0, quarter, stride=SUBLANE_STRIDE), :]
    for r0 in range(SUBLANE_STRIDE):
        for r1 in range(SUBLANE_STRIDE):
            for c in range(slabs):
                o_ref[SUBLANE_STRIDE * r1 + r0, :, c * LANES:(c + 1) * LANES] = (
                    mid_ref[c, pl.ds(r0 * quarter + r1, tm // dilation, stride=SUBLANE_STRIDE), :]
                    .astype(o_ref.dtype))


def _proj_gz_kernel(x_ref, w_ref, b_ref, o_ref, wbf_ref):
    j = pl.program_id(0)
    _cast_weight_tile(pl.program_id(1) == 0, w_ref, wbf_ref)
    is_gate = j < GZ_TILE_ZATTN
    acc = jnp.dot(x_ref[...], wbf_ref[...], preferred_element_type=jnp.float32)
    t = acc + jnp.where(is_gate, b_ref[...], 0.0)
    sg = 0.5 * jnp.tanh(0.5 * t) + 0.5
    o_ref[...] = (sg * jnp.where(is_gate, 1.0, t)).astype(o_ref.dtype)


def _proj_u_kernel(x_ref, w_ref, u_ref, xbf_ref, wbf_ref):
    _cast_weight_tile(pl.program_id(0) == 0, w_ref, wbf_ref)
    xb = x_ref[...].astype(xbf_ref.dtype)
    xbf_ref[...] = xb
    u_ref[...] = jnp.dot(xb, wbf_ref[...], preferred_element_type=jnp.float32)


def _project_u(x2, w_in, tm=512):
    m = x2.shape[0]
    return pl.pallas_call(
        _proj_u_kernel,
        grid=(m // tm,),
        in_specs=[
            pl.BlockSpec((tm, D_MODEL), lambda i: (i, 0)),
            pl.BlockSpec((D_MODEL, COL_TILE), lambda i: (0, W_TILE_U), pipeline_mode=pl.Buffered(1)),
        ],
        out_specs=[
            pl.BlockSpec((tm, COL_TILE), lambda i: (i, 0)),
            pl.BlockSpec((tm, D_MODEL), lambda i: (i, 0)),
        ],
        out_shape=[
            jax.ShapeDtypeStruct((m, POOL_WIDTH), jnp.float32),
            jax.ShapeDtypeStruct((m, D_MODEL), jnp.bfloat16),
        ],
        scratch_shapes=[pltpu.VMEM((D_MODEL, COL_TILE), jnp.bfloat16)],
        compiler_params=pltpu.CompilerParams(
            dimension_semantics=("arbitrary",),
            vmem_limit_bytes=PROJ_VMEM_LIMIT),
        name="proj_u",
    )(x2, w_in)


def _project_qkv(x2, w_in, batch, group, dilation, tm=1024):
    m = x2.shape[0]
    tiles_per_batch = m // batch // tm
    sub = tm // dilation
    n_row_scratch = {1: 0, SUBLANE_STRIDE: 1, SUBLANE_STRIDE * SUBLANE_STRIDE: 2}[dilation]
    return pl.pallas_call(
        functools.partial(_proj_qkv_kernel, dilation=dilation),
        grid=(3, batch, tiles_per_batch),
        in_specs=[
            pl.BlockSpec((tm, D_MODEL), lambda j, b, i: (b * tiles_per_batch + i, 0)),
            pl.BlockSpec((D_MODEL, COL_TILE), lambda j, b, i: (0, N_GROUPS * j + group)),
        ],
        out_specs=pl.BlockSpec((None, dilation, sub, COL_TILE), lambda j, b, i: (b, 0, i, j)),
        out_shape=jax.ShapeDtypeStruct((batch, dilation, m // batch // dilation, 3 * ATTN_WIDTH), jnp.bfloat16),
        scratch_shapes=[pltpu.VMEM((D_MODEL, COL_TILE), jnp.bfloat16)]
        + [pltpu.VMEM((COL_TILE // LANES, tm, LANES), jnp.float32)] * n_row_scratch,
        compiler_params=pltpu.CompilerParams(
            dimension_semantics=("arbitrary", "arbitrary", "arbitrary"),
            vmem_limit_bytes=PROJ_VMEM_LIMIT),
        name=f"proj_qkv_g{group}",
    )(x2, w_in)


def _project_gates_z(x2, w_in, b_gate2, tm=1024):
    m = x2.shape[0]

    def w_tile(j):
        return jnp.where(j < GZ_TILE_ZATTN, W_TILE_GATES + j,
                         jnp.where(j == GZ_TILE_ZATTN, W_TILE_ZATTN, W_TILE_ZPOOL))

    return pl.pallas_call(
        _proj_gz_kernel,
        grid=(GZ_TILES, m // tm),
        in_specs=[
            pl.BlockSpec((tm, D_MODEL), lambda j, i: (i, 0)),
            pl.BlockSpec((D_MODEL, COL_TILE), lambda j, i: (0, w_tile(j))),
            pl.BlockSpec((1, COL_TILE), lambda j, i: (0, jnp.minimum(j, GZ_TILE_ZATTN - 1))),
        ],
        out_specs=pl.BlockSpec((tm, COL_TILE), lambda j, i: (i, j)),
        out_shape=jax.ShapeDtypeStruct((m, GZ_TILES * COL_TILE), jnp.bfloat16),
        scratch_shapes=[pltpu.VMEM((D_MODEL, COL_TILE), jnp.bfloat16)],
        compiler_params=pltpu.CompilerParams(
            dimension_semantics=("arbitrary", "arbitrary"),
            vmem_limit_bytes=PROJ_VMEM_LIMIT),
        name="proj_gates_z",
    )(x2, w_in, b_gate2)


def _attn_kernel(q_ref, k_ref, v_ref, o_ref, lse_ref, kc_ref, vc_ref, *, n_blocks, n_chunks):
    chunk = pl.program_id(2)
    bq = ATTN_BLOCK
    scale = HEAD_DIM ** -0.5

    @pl.when(chunk == 0)
    def _():
        kc_ref[...] = jnp.zeros_like(kc_ref)
        vc_ref[...] = jnp.zeros_like(vc_ref)

    row = lax.broadcasted_iota(jnp.int32, (bq, 2 * bq), 0)
    col = lax.broadcasted_iota(jnp.int32, (bq, 2 * bq), 1)
    dist = bq + row - col
    band = jnp.logical_and(dist >= 0, dist <= bq)
    band_first = jnp.logical_and(band, col >= jnp.where(chunk > 0, 0, bq))
    lane = lax.broadcasted_iota(jnp.int32, (bq, LANES), 1)

    def block(r0, kcat_of, vcat_of, mask):
        lse_tile = jnp.zeros((bq, LANES), jnp.float32)
        for h in range(N_HEADS):
            hs = slice(h * HEAD_DIM, (h + 1) * HEAD_DIM)
            q = q_ref[pl.ds(r0, bq), hs]
            kcat = kcat_of(hs)
            vcat = vcat_of(hs)
            s = lax.dot_general(q, kcat, (((1,), (1,)), ((), ())), preferred_element_type=jnp.float32)
            s = jnp.where(mask, s, NEG_INF)
            m = jnp.max(s, axis=-1, keepdims=True)
            e = jnp.exp2((s - m) * (scale * LOG2_E))
            den = jnp.sum(e, axis=-1, keepdims=True)
            o = jnp.dot(e.astype(vcat.dtype), vcat, preferred_element_type=jnp.float32)
            o_ref[pl.ds(r0, bq), hs] = (o / den).astype(o_ref.dtype)
            lse_tile = jnp.where(lane == h, m * scale + jnp.log(den), lse_tile)
        lse_ref[pl.ds(r0, bq), :] = lse_tile

    block(0,
          lambda hs: jnp.concatenate([kc_ref[:, hs], k_ref[0:bq, hs]], axis=0),
          lambda hs: jnp.concatenate([vc_ref[:, hs], v_ref[0:bq, hs]], axis=0),
          band_first)

    def body(blk, carry):
        r0 = pl.multiple_of(blk * bq, bq)
        rk = pl.multiple_of(blk * bq - bq, bq)
        block(r0,
              lambda hs: k_ref[pl.ds(rk, 2 * bq), hs],
              lambda hs: v_ref[pl.ds(rk, 2 * bq), hs],
              band)
        return carry

    if n_blocks > 1:
        lax.fori_loop(1, n_blocks, body, 0)

    if n_chunks > 1:
        rl = (n_blocks - 1) * bq
        kc_ref[...] = k_ref[rl:rl + bq, :]
        vc_ref[...] = v_ref[rl:rl + bq, :]


def _dilated_attention(qkv, group, n_blocks):
    b, dilation, sub_len, _ = qkv.shape
    rows = n_blocks * ATTN_BLOCK
    n_chunks = sub_len // rows

    def in_spec(tile):
        return pl.BlockSpec((None, None, rows, COL_TILE), lambda bb, r, c: (bb, r, c, tile))

    return pl.pallas_call(
        functools.partial(_attn_kernel, n_blocks=n_blocks, n_chunks=n_chunks),
        grid=(b, dilation, n_chunks),
        in_specs=[in_spec(0), in_spec(1), in_spec(2)],
        out_specs=[
            pl.BlockSpec((None, None, rows, ATTN_WIDTH), lambda bb, r, c: (bb, r, c, 0)),
            pl.BlockSpec((None, None, rows, LANES), lambda bb, r, c: (bb, r, c, 0)),
        ],
        out_shape=[
            jax.ShapeDtypeStruct((b, dilation, sub_len, ATTN_WIDTH), jnp.bfloat16),
            jax.ShapeDtypeStruct((b, dilation, sub_len, LANES), jnp.float32),
        ],
        scratch_shapes=[
            pltpu.VMEM((ATTN_BLOCK, ATTN_WIDTH), jnp.bfloat16),
            pltpu.VMEM((ATTN_BLOCK, ATTN_WIDTH), jnp.bfloat16),
        ],
        compiler_params=pltpu.CompilerParams(
            dimension_semantics=("arbitrary", "arbitrary", "arbitrary"),
            vmem_limit_bytes=40 * MIB),
        name=f"dilated_attn_g{group}",
    )(qkv, qkv, qkv)


def _merge_combine(o1_ref, o2_ref, o3_ref, l1_ref, l2_ref, l3_ref, za_ref, on2_ref, on3_ref, ln2_ref, ln3_ref,
                   ya_ref):
    def to_token_order(src_ref, dst_ref):
        dilation, sub = src_ref.shape[0], src_ref.shape[1]
        for r in range(dilation):
            for c in range(dst_ref.shape[0]):
                dst_ref[c, pl.ds(r, sub, stride=dilation), :] = (
                    src_ref[r, :, c * LANES:(c + 1) * LANES].astype(dst_ref.dtype))

    to_token_order(o2_ref, on2_ref)
    to_token_order(o3_ref, on3_ref)
    to_token_order(l2_ref, ln2_ref)
    to_token_order(l3_ref, ln3_ref)

    l1 = l1_ref[...]
    l2 = ln2_ref[0]
    l3 = ln3_ref[0]
    mx = jnp.maximum(jnp.maximum(l1, l2), l3)
    e1 = jnp.exp(l1 - mx)
    e2 = jnp.exp(l2 - mx)
    e3 = jnp.exp(l3 - mx)
    tot = e1 + e2 + e3
    w1 = e1 / tot
    w2 = e2 / tot
    w3 = e3 / tot
    for h in range(N_HEADS):
        hs = slice(h * HEAD_DIM, (h + 1) * HEAD_DIM)
        o = (w1[:, h:h + 1] * o1_ref[:, hs].astype(jnp.float32)
             + w2[:, h:h + 1] * on2_ref[h]
             + w3[:, h:h + 1] * on3_ref[h])
        ya_ref[:, hs] = (o * za_ref[:, hs].astype(jnp.float32)).astype(ya_ref.dtype)


def _merge_pool(i, u_ref, halo_ref, zp_ref, w_pool_ref, pool_scale_ref, ext_ref, s2_ref, s4_ref, s8_ref, yp_ref,
                *, tm):
    rows = HALO + tm
    ext_ref[0:HALO, :] = jnp.where(i > 0, halo_ref[...], 0.0)
    ext_ref[HALO:, :] = u_ref[...]
    g = POOL_GROUP
    s2_ref[8:rows, :] = ext_ref[8:rows, :] + ext_ref[7:rows - 1, :]
    s4_ref[16:rows, :] = s2_ref[16:rows, g:] + s2_ref[14:rows - 2, g:]
    s8_ref[24:rows, :] = s4_ref[24:rows, g:] + s4_ref[20:rows - 4, g:]
    s16 = s8_ref[HALO:rows, g:] + s8_ref[HALO - 8:rows - 8, g:]
    window_sums = (s2_ref[HALO:rows, 0:g], s4_ref[HALO:rows, 0:g], s8_ref[HALO:rows, 0:g], s16)
    pos = i * tm + lax.broadcasted_iota(jnp.int32, (tm, 1), 0)
    for k, w in enumerate(POOL_WINDOWS):
        cs = slice(k * g, (k + 1) * g)
        cnt = jnp.minimum(pos + 1, w).astype(jnp.float32)
        p = window_sums[k] / cnt - ext_ref[HALO:rows, cs]
        y = jnp.dot(p.astype(jnp.bfloat16), w_pool_ref[k], preferred_element_type=jnp.float32)
        y = y * pool_scale_ref[:, cs] * zp_ref[:, cs].astype(jnp.float32)
        yp_ref[:, cs] = y.astype(yp_ref.dtype)


def _merge_finish(pa, pp, g_attn_ref, g_pool_ref, x_ref, w_out_ref, gamma_ref, beta_ref, out_ref):
    merged = (g_attn_ref[...].astype(jnp.float32) * pa + g_pool_ref[...].astype(jnp.float32) * pp)
    out = jnp.dot(merged.astype(jnp.bfloat16), w_out_ref[...], preferred_element_type=jnp.float32)
    r = DEEPNORM_ALPHA * x_ref[...] + out
    mu = jnp.mean(r, axis=-1, keepdims=True)
    rc = r - mu
    var = jnp.mean(rc * rc, axis=-1, keepdims=True)
    out_ref[...] = rc * lax.rsqrt(var + LN_EPS) * gamma_ref[...] + beta_ref[...]


def _merge_kernel(o1_ref, o2_ref, o3_ref, l1_ref, l2_ref, l3_ref, za_ref, zp_ref, u_ref, halo_ref,
                  g_attn_ref, g_pool_ref, x_ref, w_pool_ref, pool_scale_ref, wpa_ref, wpp_ref, w_out_ref,
                  gamma_ref, beta_ref, out_ref, ext_ref, s2_ref, s4_ref, s8_ref, on2_ref, on3_ref, ln2_ref, ln3_ref,
                  ya0_ref, yp0_ref, ya1_ref, yp1_ref, *, tm, n_tiles, tiles_per_seq):
    s = pl.program_id(0)
    i_prep = jnp.minimum(s, n_tiles - 1) % tiles_per_seq

    @pl.when(s == 0)
    def _():
        ya1_ref[...] = jnp.zeros_like(ya1_ref)
        yp1_ref[...] = jnp.zeros_like(yp1_ref)

    def step(ya_w, yp_w, ya_r, yp_r):
        pa = jnp.dot(ya_r[...], wpa_ref[...], preferred_element_type=jnp.float32)
        _merge_combine(o1_ref, o2_ref, o3_ref, l1_ref, l2_ref, l3_ref, za_ref, on2_ref, on3_ref, ln2_ref, ln3_ref,
                       ya_w)
        pp = jnp.dot(yp_r[...], wpp_ref[...], preferred_element_type=jnp.float32)
        _merge_pool(i_prep, u_ref, halo_ref, zp_ref, w_pool_ref, pool_scale_ref, ext_ref, s2_ref, s4_ref, s8_ref,
                    yp_w, tm=tm)
        _merge_finish(pa, pp, g_attn_ref, g_pool_ref, x_ref, w_out_ref, gamma_ref, beta_ref, out_ref)

    @pl.when(s % 2 == 0)
    def _():
        step(ya0_ref, yp0_ref, ya1_ref, yp1_ref)

    @pl.when(s % 2 == 1)
    def _():
        step(ya1_ref, yp1_ref, ya0_ref, yp0_ref)


def _merge(os_, lses, gz3, u3, x, w_pool_bf, pool_scale2, wpa_bf, wpp_bf, w_out_bf, gamma2, beta2, tm=256):
    b, s, _ = x.shape
    tiles_per_seq = s // tm
    n_tiles = b * tiles_per_seq

    def prep_tile(step):
        t = jnp.minimum(step, n_tiles - 1)
        return t // tiles_per_seq, t % tiles_per_seq

    def proj_tile(step):
        t = jnp.maximum(step - 1, 0)
        return t // tiles_per_seq, t % tiles_per_seq

    def tok(which, width, tile=0):
        return pl.BlockSpec((None, tm, width), lambda st: (*which(st), tile))

    def grouped(arr):
        dilation, width = arr.shape[1], arr.shape[3]

        def index(st):
            bb, i = prep_tile(st)
            return bb, 0, i, 0
        if dilation == 1:
            return pl.BlockSpec((None, None, tm, width), index)
        return pl.BlockSpec((None, dilation, tm // dilation, width), index)

    def halo_index(st):
        bb, i = prep_tile(st)
        return bb, jnp.maximum(i * (tm // HALO) - 1, 0), 0

    def const(shape):
        nd = len(shape)
        return pl.BlockSpec(shape, lambda st: (0,) * nd, pipeline_mode=pl.Buffered(1))

    in_specs = [
        grouped(os_[0]), grouped(os_[1]), grouped(os_[2]),
        grouped(lses[0]), grouped(lses[1]), grouped(lses[2]),
        tok(prep_tile, COL_TILE, GZ_TILE_ZATTN), tok(prep_tile, COL_TILE, GZ_TILE_ZPOOL),
        tok(prep_tile, POOL_WIDTH),
        pl.BlockSpec((None, HALO, POOL_WIDTH), halo_index),
        tok(proj_tile, D_MODEL, 0), tok(proj_tile, D_MODEL, 1),
        tok(proj_tile, D_MODEL),
        const((len(POOL_WINDOWS), POOL_GROUP, POOL_GROUP)),
        const((1, POOL_WIDTH)),
        const((ATTN_WIDTH, D_MODEL)),
        const((POOL_WIDTH, D_MODEL)),
        const((D_MODEL, D_MODEL)),
        const((1, D_MODEL)),
        const((1, D_MODEL)),
    ]
    f32 = jnp.float32
    return pl.pallas_call(
        functools.partial(_merge_kernel, tm=tm, n_tiles=n_tiles, tiles_per_seq=tiles_per_seq),
        grid=(n_tiles + 1,),
        in_specs=in_specs,
        out_specs=tok(proj_tile, D_MODEL),
        out_shape=jax.ShapeDtypeStruct((b, s, D_MODEL), f32),
        scratch_shapes=[
            pltpu.VMEM((HALO + tm, POOL_WIDTH), f32),
            pltpu.VMEM((HALO + tm, POOL_WIDTH), f32),
            pltpu.VMEM((HALO + tm, POOL_WIDTH - POOL_GROUP), f32),
            pltpu.VMEM((HALO + tm, POOL_WIDTH - 2 * POOL_GROUP), f32),
            pltpu.VMEM((N_HEADS, tm, HEAD_DIM), f32),
            pltpu.VMEM((N_HEADS, tm, HEAD_DIM), f32),
            pltpu.VMEM((1, tm, LANES), f32),
            pltpu.VMEM((1, tm, LANES), f32),
            pltpu.VMEM((tm, ATTN_WIDTH), jnp.bfloat16),
            pltpu.VMEM((tm, POOL_WIDTH), jnp.bfloat16),
            pltpu.VMEM((tm, ATTN_WIDTH), jnp.bfloat16),
            pltpu.VMEM((tm, POOL_WIDTH), jnp.bfloat16),
        ],
        compiler_params=pltpu.CompilerParams(
            dimension_semantics=("arbitrary",),
            vmem_limit_bytes=56 * MIB),
        name="merge_out_ln",
    )(os_[0], os_[1], os_[2], lses[0], lses[1], lses[2], gz3, gz3, u3, u3, gz3, gz3, x,
      w_pool_bf, pool_scale2, wpa_bf, wpp_bf, w_out_bf, gamma2, beta2)


def _layer(x, w_in, b_gate, w_pool, pool_scale, w_proj_attn, w_proj_pool, w_out, ln_gamma, ln_beta):
    b, s, d = x.shape
    bf = jnp.bfloat16
    u2, x2 = _project_u(x.reshape(b * s, d), w_in)
    os_, lses = [], []
    for g, (window, dilation) in enumerate(DILATED_GROUPS):
        assert window // dilation == ATTN_BLOCK
        qkv = _project_qkv(x2, w_in, b, g, dilation)
        n_blocks = min(8, s // dilation // ATTN_BLOCK)
        o, lse = _dilated_attention(qkv, g, n_blocks)
        os_.append(o)
        lses.append(lse)
    gz2 = _project_gates_z(x2, w_in, b_gate.reshape(1, -1))
    gz3 = gz2.reshape(b, s, GZ_TILES * COL_TILE)
    u3 = u2.reshape(b, s, POOL_WIDTH)
    return _merge(os_, lses, gz3, u3, x, w_pool.astype(bf), pool_scale.reshape(1, -1),
                  w_proj_attn.astype(bf), w_proj_pool.astype(bf), w_out.astype(bf),
                  ln_gamma.reshape(1, -1), ln_beta.reshape(1, -1))


def kernel(x, w_in, b_gate, w_pool, pool_scale, w_proj_attn, w_proj_pool, w_out, ln_gamma, ln_beta):
    depth = w_in.shape[0]
    for layer in range(depth):
        x = _layer(x, w_in[layer], b_gate[layer], w_pool[layer], pool_scale[layer],
                   w_proj_attn[layer], w_proj_pool[layer], w_out[layer],
                   ln_gamma[layer], ln_beta[layer])
    return x
```

```python
import functools

import jax
import jax.numpy as jnp
from jax import lax
from jax.experimental import pallas as pl
from jax.experimental.pallas import tpu as pltpu

D_MODEL = 2048
HEAD_DIM = 128
N_HEADS = 8
ATTN_WIDTH = N_HEADS * HEAD_DIM
DILATED_GROUPS = ((128, 1), (512, 4), (2048, 16))
N_GROUPS = len(DILATED_GROUPS)
POOL_WINDOWS = (2, 4, 8, 16)
POOL_WIDTH = D_MODEL // 2
POOL_GROUP = POOL_WIDTH // len(POOL_WINDOWS)
DEEPNORM_ALPHA = 2.0 ** 0.25
LN_EPS = 1e-5
NEG_INF = -1e30
LOG2_E = 1.4426950408889634

LANES = 128
MIB = 1024 * 1024
SUBLANE_STRIDE = 4
PROJ_VMEM_LIMIT = 54 * MIB

COL_TILE = 1024
W_TILE_ZATTN = 9
W_TILE_U = 10
W_TILE_ZPOOL = 11
W_TILE_GATES = 12
GZ_TILES = 6
GZ_TILE_ZATTN = 4
GZ_TILE_ZPOOL = 5

ATTN_BLOCK = 128
HALO = 32


def _cast_weight_tile(first_step, w_ref, wbf_ref):
    @pl.when(first_step)
    def _():
        wbf_ref[...] = w_ref[...].astype(wbf_ref.dtype)


def _proj_qkv_kernel(x_ref, w_ref, o_ref, wbf_ref, *scratch, dilation):
    _cast_weight_tile(jnp.logical_and(pl.program_id(1) == 0, pl.program_id(2) == 0), w_ref, wbf_ref)
    acc = jnp.dot(x_ref[...], wbf_ref[...], preferred_element_type=jnp.float32)
    if dilation == 1:
        o_ref[0] = acc.astype(o_ref.dtype)
        return
    acc_ref = scratch[0]
    slabs, tm = acc_ref.shape[0], acc_ref.shape[1]
    for c in range(slabs):
        acc_ref[c] = acc[:, c * LANES:(c + 1) * LANES]
    if dilation == SUBLANE_STRIDE:
        for r in range(dilation):
            for c in range(slabs):
                o_ref[r, :, c * LANES:(c + 1) * LANES] = (
                    acc_ref[c, pl.ds(r, tm // dilation, stride=dilation), :].astype(o_ref.dtype))
        return
    assert dilation == SUBLANE_STRIDE * SUBLANE_STRIDE
    mid_ref = scratch[1]
    quarter = tm // SUBLANE_STRIDE
    for r0 in range(SUBLANE_STRIDE):
        for c in range(slabs):
            mid_ref[c, pl.ds(r0 * quarter, quarter), :] = acc_ref[c, pl.ds(r0, quarter, stride=SUBLANE_STRIDE), :]
    for r0 in range(SUBLANE_STRIDE):
        for r1 in range(SUBLANE_STRIDE):
            for c in range(slabs):
                o_ref[SUBLANE_STRIDE * r1 + r0, :, c * LANES:(c + 1) * LANES] = (
                    mid_ref[c, pl.ds(r0 * quarter + r1, tm // dilation, stride=SUBLANE_STRIDE), :]
                    .astype(o_ref.dtype))


def _proj_gz_kernel(x_ref, w_ref, b_ref, o_ref, wbf_ref):
    j = pl.program_id(0)
    _cast_weight_tile(pl.program_id(1) == 0, w_ref, wbf_ref)
    is_gate = j < GZ_TILE_ZATTN
    acc = jnp.dot(x_ref[...], wbf_ref[...], preferred_element_type=jnp.float32)
    t = acc + jnp.where(is_gate, b_ref[...], 0.0)
    sg = 0.5 * jnp.tanh(0.5 * t) + 0.5
    o_ref[...] = (sg * jnp.where(is_gate, 1.0, t)).astype(o_ref.dtype)


def _proj_u_kernel(x_ref, w_ref, u_ref, xbf_ref, wbf_ref):
    _cast_weight_tile(pl.program_id(0) == 0, w_ref, wbf_ref)
    xb = x_ref[...].astype(xbf_ref.dtype)
    xbf_ref[...] = xb
    u_ref[...] = jnp.dot(xb, wbf_ref[...], preferred_element_type=jnp.float32)


def _project_u(x2, w_in, tm=512):
    m = x2.shape[0]
    return pl.pallas_call(
        _proj_u_kernel,
        grid=(m // tm,),
        in_specs=[
            pl.BlockSpec((tm, D_MODEL), lambda i: (i, 0)),
            pl.BlockSpec((D_MODEL, COL_TILE), lambda i: (0, W_TILE_U), pipeline_mode=pl.Buffered(1)),
        ],
        out_specs=[
            pl.BlockSpec((tm, COL_TILE), lambda i: (i, 0)),
            pl.BlockSpec((tm, D_MODEL), lambda i: (i, 0)),
        ],
        out_shape=[
            jax.ShapeDtypeStruct((m, POOL_WIDTH), jnp.float32),
            jax.ShapeDtypeStruct((m, D_MODEL), jnp.bfloat16),
        ],
        scratch_shapes=[pltpu.VMEM((D_MODEL, COL_TILE), jnp.bfloat16)],
        compiler_params=pltpu.CompilerParams(
            dimension_semantics=("arbitrary",),
            vmem_limit_bytes=PROJ_VMEM_LIMIT),
        name="proj_u",
    )(x2, w_in)


def _project_qkv(x2, w_in, batch, group, dilation, tm=1024):
    m = x2.shape[0]
    tiles_per_batch = m // batch // tm
    sub = tm // dilation
    n_row_scratch = {1: 0, SUBLANE_STRIDE: 1, SUBLANE_STRIDE * SUBLANE_STRIDE: 2}[dilation]
    return pl.pallas_call(
        functools.partial(_proj_qkv_kernel, dilation=dilation),
        grid=(3, batch, tiles_per_batch),
        in_specs=[
            pl.BlockSpec((tm, D_MODEL), lambda j, b, i: (b * tiles_per_batch + i, 0)),
            pl.BlockSpec((D_MODEL, COL_TILE), lambda j, b, i: (0, N_GROUPS * j + group)),
        ],
        out_specs=pl.BlockSpec((None, dilation, sub, COL_TILE), lambda j, b, i: (b, 0, i, j)),
        out_shape=jax.ShapeDtypeStruct((batch, dilation, m // batch // dilation, 3 * ATTN_WIDTH), jnp.bfloat16),
        scratch_shapes=[pltpu.VMEM((D_MODEL, COL_TILE), jnp.bfloat16)]
        + [pltpu.VMEM((COL_TILE // LANES, tm, LANES), jnp.float32)] * n_row_scratch,
        compiler_params=pltpu.CompilerParams(
            dimension_semantics=("arbitrary", "arbitrary", "arbitrary"),
            vmem_limit_bytes=PROJ_VMEM_LIMIT),
        name=f"proj_qkv_g{group}",
    )(x2, w_in)


def _project_gates_z(x2, w_in, b_gate2, tm=1024):
    m = x2.shape[0]

    def w_tile(j):
        return jnp.where(j < GZ_TILE_ZATTN, W_TILE_GATES + j,
                         jnp.where(j == GZ_TILE_ZATTN, W_TILE_ZATTN, W_TILE_ZPOOL))

    return pl.pallas_call(
        _proj_gz_kernel,
        grid=(GZ_TILES, m // tm),
        in_specs=[
            pl.BlockSpec((tm, D_MODEL), lambda j, i: (i, 0)),
            pl.BlockSpec((D_MODEL, COL_TILE), lambda j, i: (0, w_tile(j))),
            pl.BlockSpec((1, COL_TILE), lambda j, i: (0, jnp.minimum(j, GZ_TILE_ZATTN - 1))),
        ],
        out_specs=pl.BlockSpec((tm, COL_TILE), lambda j, i: (i, j)),
        out_shape=jax.ShapeDtypeStruct((m, GZ_TILES * COL_TILE), jnp.bfloat16),
        scratch_shapes=[pltpu.VMEM((D_MODEL, COL_TILE), jnp.bfloat16)],
        compiler_params=pltpu.CompilerParams(
            dimension_semantics=("arbitrary", "arbitrary"),
            vmem_limit_bytes=PROJ_VMEM_LIMIT),
        name="proj_gates_z",
    )(x2, w_in, b_gate2)


def _attn_kernel(q_ref, k_ref, v_ref, o_ref, lse_ref, kc_ref, vc_ref, *, n_blocks, n_chunks):
    chunk = pl.program_id(2)
    bq = ATTN_BLOCK
    scale = HEAD_DIM ** -0.5

    @pl.when(chunk == 0)
    def _():
        kc_ref[...] = jnp.zeros_like(kc_ref)
        vc_ref[...] = jnp.zeros_like(vc_ref)

    row = lax.broadcasted_iota(jnp.int32, (bq, 2 * bq), 0)
    col = lax.broadcasted_iota(jnp.int32, (bq, 2 * bq), 1)
    dist = bq + row - col
    band = jnp.logical_and(dist >= 0, dist <= bq)
    band_first = jnp.logical_and(band, col >= jnp.where(chunk > 0, 0, bq))
    lane = lax.broadcasted_iota(jnp.int32, (bq, LANES), 1)

    def block(r0, kcat_of, vcat_of, mask):
        heads = [slice(h * HEAD_DIM, (h + 1) * HEAD_DIM) for h in range(N_HEADS)]
        ss = [jnp.where(mask,
                        lax.dot_general(q_ref[pl.ds(r0, bq), hs], kcat_of(hs), (((1,), (1,)), ((), ())),
                                        preferred_element_type=jnp.float32),
                        NEG_INF) for hs in heads]
        ms = [jnp.max(s, axis=-1, keepdims=True) for s in ss]
        es = [jnp.exp2((s - m) * (scale * LOG2_E)) for s, m in zip(ss, ms)]
        dens = [jnp.sum(e, axis=-1, keepdims=True) for e in es]
        outs = [jnp.dot(e.astype(o_ref.dtype), vcat_of(hs), preferred_element_type=jnp.float32)
                for e, hs in zip(es, heads)]
        lse_tile = jnp.zeros((bq, LANES), jnp.float32)
        for h, hs in enumerate(heads):
            o_ref[pl.ds(r0, bq), hs] = (outs[h] / dens[h]).astype(o_ref.dtype)
            lse_tile = jnp.where(lane == h, ms[h] * scale + jnp.log(dens[h]), lse_tile)
        lse_ref[pl.ds(r0, bq), :] = lse_tile

    block(0,
          lambda hs: jnp.concatenate([kc_ref[:, hs], k_ref[0:bq, hs]], axis=0),
          lambda hs: jnp.concatenate([vc_ref[:, hs], v_ref[0:bq, hs]], axis=0),
          band_first)

    def body(blk, carry):
        r0 = pl.multiple_of(blk * bq, bq)
        rk = pl.multiple_of(blk * bq - bq, bq)
        block(r0,
              lambda hs: k_ref[pl.ds(rk, 2 * bq), hs],
              lambda hs: v_ref[pl.ds(rk, 2 * bq), hs],
              band)
        return carry

    if n_blocks > 1:
        lax.fori_loop(1, n_blocks, body, 0)

    if n_chunks > 1:
        rl = (n_blocks - 1) * bq
        kc_ref[...] = k_ref[rl:rl + bq, :]
        vc_ref[...] = v_ref[rl:rl + bq, :]


def _dilated_attention(qkv, group, n_blocks):
    b, dilation, sub_len, _ = qkv.shape
    rows = n_blocks * ATTN_BLOCK
    n_chunks = sub_len // rows

    def in_spec(tile):
        return pl.BlockSpec((None, None, rows, COL_TILE), lambda bb, r, c: (bb, r, c, tile))

    return pl.pallas_call(
        functools.partial(_attn_kernel, n_blocks=n_blocks, n_chunks=n_chunks),
        grid=(b, dilation, n_chunks),
        in_specs=[in_spec(0), in_spec(1), in_spec(2)],
        out_specs=[
            pl.BlockSpec((None, None, rows, ATTN_WIDTH), lambda bb, r, c: (bb, r, c, 0)),
            pl.BlockSpec((None, None, rows, LANES), lambda bb, r, c: (bb, r, c, 0)),
        ],
        out_shape=[
            jax.ShapeDtypeStruct((b, dilation, sub_len, ATTN_WIDTH), jnp.bfloat16),
            jax.ShapeDtypeStruct((b, dilation, sub_len, LANES), jnp.float32),
        ],
        scratch_shapes=[
            pltpu.VMEM((ATTN_BLOCK, ATTN_WIDTH), jnp.bfloat16),
            pltpu.VMEM((ATTN_BLOCK, ATTN_WIDTH), jnp.bfloat16),
        ],
        compiler_params=pltpu.CompilerParams(
            dimension_semantics=("arbitrary", "arbitrary", "arbitrary"),
            vmem_limit_bytes=40 * MIB),
        name=f"dilated_attn_g{group}",
    )(qkv, qkv, qkv)


def _merge_combine(o1_ref, o2_ref, o3_ref, l1_ref, l2_ref, l3_ref, za_ref, on2_ref, on3_ref, ln2_ref, ln3_ref,
                   ya_ref):
    def to_token_order(src_ref, dst_ref):
        dilation, sub = src_ref.shape[0], src_ref.shape[1]
        for r in range(dilation):
            for c in range(dst_ref.shape[0]):
                dst_ref[c, pl.ds(r, sub, stride=dilation), :] = (
                    src_ref[r, :, c * LANES:(c + 1) * LANES].astype(dst_ref.dtype))

    to_token_order(o2_ref, on2_ref)
    to_token_order(o3_ref, on3_ref)
    to_token_order(l2_ref, ln2_ref)
    to_token_order(l3_ref, ln3_ref)

    l1 = l1_ref[...]
    l2 = ln2_ref[0]
    l3 = ln3_ref[0]
    mx = jnp.maximum(jnp.maximum(l1, l2), l3)
    e1 = jnp.exp(l1 - mx)
    e2 = jnp.exp(l2 - mx)
    e3 = jnp.exp(l3 - mx)
    tot = e1 + e2 + e3
    w1 = e1 / tot
    w2 = e2 / tot
    w3 = e3 / tot
    for h in range(N_HEADS):
        hs = slice(h * HEAD_DIM, (h + 1) * HEAD_DIM)
        o = (w1[:, h:h + 1] * o1_ref[:, hs].astype(jnp.float32)
             + w2[:, h:h + 1] * on2_ref[h]
             + w3[:, h:h + 1] * on3_ref[h])
        ya_ref[:, hs] = (o * za_ref[:, hs].astype(jnp.float32)).astype(ya_ref.dtype)


def _merge_pool(i, u_ref, halo_ref, zp_ref, w_pool_ref, pool_scale_ref, ext_ref, s2_ref, s4_ref, s8_ref, yp_ref,
                *, tm):
    rows = HALO + tm
    ext_ref[0:HALO, :] = jnp.where(i > 0, halo_ref[...], 0.0)
    ext_ref[HALO:, :] = u_ref[...]
    g = POOL_GROUP
    s2_ref[8:rows, :] = ext_ref[8:rows, :] + ext_ref[7:rows - 1, :]
    s4_ref[16:rows, :] = s2_ref[16:rows, g:] + s2_ref[14:rows - 2, g:]
    s8_ref[24:rows, :] = s4_ref[24:rows, g:] + s4_ref[20:rows - 4, g:]
    s16 = s8_ref[HALO:rows, g:] + s8_ref[HALO - 8:rows - 8, g:]
    window_sums = (s2_ref[HALO:rows, 0:g], s4_ref[HALO:rows, 0:g], s8_ref[HALO:rows, 0:g], s16)
    pos = i * tm + lax.broadcasted_iota(jnp.int32, (tm, 1), 0)
    for k, w in enumerate(POOL_WINDOWS):
        cs = slice(k * g, (k + 1) * g)
        cnt = jnp.minimum(pos + 1, w).astype(jnp.float32)
        p = window_sums[k] / cnt - ext_ref[HALO:rows, cs]
        y = jnp.dot(p.astype(jnp.bfloat16), w_pool_ref[k], preferred_element_type=jnp.float32)
        y = y * pool_scale_ref[:, cs] * zp_ref[:, cs].astype(jnp.float32)
        yp_ref[:, cs] = y.astype(yp_ref.dtype)


def _merge_finish(pa, pp, g_attn_ref, g_pool_ref, x_ref, w_out_ref, gamma_ref, beta_ref, out_ref):
    merged = (g_attn_ref[...].astype(jnp.float32) * pa + g_pool_ref[...].astype(jnp.float32) * pp)
    out = jnp.dot(merged.astype(jnp.bfloat16), w_out_ref[...], preferred_element_type=jnp.float32)
    r = DEEPNORM_ALPHA * x_ref[...] + out
    mu = jnp.mean(r, axis=-1, keepdims=True)
    rc = r - mu
    var = jnp.mean(rc * rc, axis=-1, keepdims=True)
    out_ref[...] = rc * lax.rsqrt(var + LN_EPS) * gamma_ref[...] + beta_ref[...]


def _merge_kernel(o1_ref, o2_ref, o3_ref, l1_ref, l2_ref, l3_ref, za_ref, zp_ref, u_ref, halo_ref,
                  g_attn_ref, g_pool_ref, x_ref, w_pool_ref, pool_scale_ref, wpa_ref, wpp_ref, w_out_ref,
                  gamma_ref, beta_ref, out_ref, ext_ref, s2_ref, s4_ref, s8_ref, on2_ref, on3_ref, ln2_ref, ln3_ref,
                  ya0_ref, yp0_ref, ya1_ref, yp1_ref, *, tm, n_tiles, tiles_per_seq):
    s = pl.program_id(0)
    i_prep = jnp.minimum(s, n_tiles - 1) % tiles_per_seq

    @pl.when(s == 0)
    def _():
        ya1_ref[...] = jnp.zeros_like(ya1_ref)
        yp1_ref[...] = jnp.zeros_like(yp1_ref)

    def step(ya_w, yp_w, ya_r, yp_r):
        pa = jnp.dot(ya_r[...], wpa_ref[...], preferred_element_type=jnp.float32)
        _merge_combine(o1_ref, o2_ref, o3_ref, l1_ref, l2_ref, l3_ref, za_ref, on2_ref, on3_ref, ln2_ref, ln3_ref,
                       ya_w)
        pp = jnp.dot(yp_r[...], wpp_ref[...], preferred_element_type=jnp.float32)
        _merge_pool(i_prep, u_ref, halo_ref, zp_ref, w_pool_ref, pool_scale_ref, ext_ref, s2_ref, s4_ref, s8_ref,
                    yp_w, tm=tm)
        _merge_finish(pa, pp, g_attn_ref, g_pool_ref, x_ref, w_out_ref, gamma_ref, beta_ref, out_ref)

    @pl.when(s % 2 == 0)
    def _():
        step(ya0_ref, yp0_ref, ya1_ref, yp1_ref)

    @pl.when(s % 2 == 1)
    def _():
        step(ya1_ref, yp1_ref, ya0_ref, yp0_ref)


def _merge(os_, lses, gz3, u3, x, w_pool_bf, pool_scale2, wpa_bf, wpp_bf, w_out_bf, gamma2, beta2, tm=256):
    b, s, _ = x.shape
    tiles_per_seq = s // tm
    n_tiles = b * tiles_per_seq

    def prep_tile(step):
        t = jnp.minimum(step, n_tiles - 1)
        return t // tiles_per_seq, t % tiles_per_seq

    def proj_tile(step):
        t = jnp.maximum(step - 1, 0)
        return t // tiles_per_seq, t % tiles_per_seq

    def tok(which, width, tile=0):
        return pl.BlockSpec((None, tm, width), lambda st: (*which(st), tile))

    def grouped(arr):
        dilation, width = arr.shape[1], arr.shape[3]

        def index(st):
            bb, i = prep_tile(st)
            return bb, 0, i, 0
        if dilation == 1:
            return pl.BlockSpec((None, None, tm, width), index)
        return pl.BlockSpec((None, dilation, tm // dilation, width), index)

    def halo_index(st):
        bb, i = prep_tile(st)
        return bb, jnp.maximum(i * (tm // HALO) - 1, 0), 0

    def const(shape):
        nd = len(shape)
        return pl.BlockSpec(shape, lambda st: (0,) * nd, pipeline_mode=pl.Buffered(1))

    in_specs = [
        grouped(os_[0]), grouped(os_[1]), grouped(os_[2]),
        grouped(lses[0]), grouped(lses[1]), grouped(lses[2]),
        tok(prep_tile, COL_TILE, GZ_TILE_ZATTN), tok(prep_tile, COL_TILE, GZ_TILE_ZPOOL),
        tok(prep_tile, POOL_WIDTH),
        pl.BlockSpec((None, HALO, POOL_WIDTH), halo_index),
        tok(proj_tile, D_MODEL, 0), tok(proj_tile, D_MODEL, 1),
        tok(proj_tile, D_MODEL),
        const((len(POOL_WINDOWS), POOL_GROUP, POOL_GROUP)),
        const((1, POOL_WIDTH)),
        const((ATTN_WIDTH, D_MODEL)),
        const((POOL_WIDTH, D_MODEL)),
        const((D_MODEL, D_MODEL)),
        const((1, D_MODEL)),
        const((1, D_MODEL)),
    ]
    f32 = jnp.float32
    return pl.pallas_call(
        functools.partial(_merge_kernel, tm=tm, n_tiles=n_tiles, tiles_per_seq=tiles_per_seq),
        grid=(n_tiles + 1,),
        in_specs=in_specs,
        out_specs=tok(proj_tile, D_MODEL),
        out_shape=jax.ShapeDtypeStruct((b, s, D_MODEL), f32),
        scratch_shapes=[
            pltpu.VMEM((HALO + tm, POOL_WIDTH), f32),
            pltpu.VMEM((HALO + tm, POOL_WIDTH), f32),
            pltpu.VMEM((HALO + tm, POOL_WIDTH - POOL_GROUP), f32),
            pltpu.VMEM((HALO + tm, POOL_WIDTH - 2 * POOL_GROUP), f32),
            pltpu.VMEM((N_HEADS, tm, HEAD_DIM), f32),
            pltpu.VMEM((N_HEADS, tm, HEAD_DIM), f32),
            pltpu.VMEM((1, tm, LANES), f32),
            pltpu.VMEM((1, tm, LANES), f32),
            pltpu.VMEM((tm, ATTN_WIDTH), jnp.bfloat16),
            pltpu.VMEM((tm, POOL_WIDTH), jnp.bfloat16),
            pltpu.VMEM((tm, ATTN_WIDTH), jnp.bfloat16),
            pltpu.VMEM((tm, POOL_WIDTH), jnp.bfloat16),
        ],
        compiler_params=pltpu.CompilerParams(
            dimension_semantics=("arbitrary",),
            vmem_limit_bytes=56 * MIB),
        name="merge_out_ln",
    )(os_[0], os_[1], os_[2], lses[0], lses[1], lses[2], gz3, gz3, u3, u3, gz3, gz3, x,
      w_pool_bf, pool_scale2, wpa_bf, wpp_bf, w_out_bf, gamma2, beta2)


def _layer(x, w_in, b_gate, w_pool, pool_scale, w_proj_attn, w_proj_pool, w_out, ln_gamma, ln_beta):
    b, s, d = x.shape
    bf = jnp.bfloat16
    u2, x2 = _project_u(x.reshape(b * s, d), w_in)
    os_, lses = [], []
    for g, (window, dilation) in enumerate(DILATED_GROUPS):
        assert window // dilation == ATTN_BLOCK
        qkv = _project_qkv(x2, w_in, b, g, dilation)
        n_blocks = min(8, s // dilation // ATTN_BLOCK)
        o, lse = _dilated_attention(qkv, g, n_blocks)
        os_.append(o)
        lses.append(lse)
    gz2 = _project_gates_z(x2, w_in, b_gate.reshape(1, -1))
    gz3 = gz2.reshape(b, s, GZ_TILES * COL_TILE)
    u3 = u2.reshape(b, s, POOL_WIDTH)
    return _merge(os_, lses, gz3, u3, x, w_pool.astype(bf), pool_scale.reshape(1, -1),
                  w_proj_attn.astype(bf), w_proj_pool.astype(bf), w_out.astype(bf),
                  ln_gamma.reshape(1, -1), ln_beta.reshape(1, -1))


def kernel(x, w_in, b_gate, w_pool, pool_scale, w_proj_attn, w_proj_pool, w_out, ln_gamma, ln_beta):
    depth = w_in.shape[0]
    for layer in range(depth):
        x = _layer(x, w_in[layer], b_gate[layer], w_pool[layer], pool_scale[layer],
                   w_proj_attn[layer], w_proj_pool[layer], w_out[layer],
                   ln_gamma[layer], ln_beta[layer])
    return x
```

```python
import functools

import jax
import jax.numpy as jnp
from jax import lax
from jax.experimental import pallas as pl
from jax.experimental.pallas import tpu as pltpu

D_MODEL = 2048
HEAD_DIM = 128
N_HEADS = 8
ATTN_WIDTH = N_HEADS * HEAD_DIM
DILATED_GROUPS = ((128, 1), (512, 4), (2048, 16))
N_GROUPS = len(DILATED_GROUPS)
POOL_WINDOWS = (2, 4, 8, 16)
POOL_WIDTH = D_MODEL // 2
POOL_GROUP = POOL_WIDTH // len(POOL_WINDOWS)
DEEPNORM_ALPHA = 2.0 ** 0.25
LN_EPS = 1e-5
NEG_INF = -1e30
LOG2_E = 1.4426950408889634

LANES = 128
MIB = 1024 * 1024
SUBLANE_STRIDE = 4
PROJ_VMEM_LIMIT = 54 * MIB

COL_TILE = 1024
W_TILE_ZATTN = 9
W_TILE_U = 10
W_TILE_ZPOOL = 11
W_TILE_GATES = 12
GZ_TILES = 6
GZ_TILE_ZATTN = 4
GZ_TILE_ZPOOL = 5

ATTN_BLOCK = 128
MERGE_CHUNK = 256
HALO = 32


def _cast_weight_tile(first_step, w_ref, wbf_ref):
    @pl.when(first_step)
    def _():
        wbf_ref[...] = w_ref[...].astype(wbf_ref.dtype)


def _proj_qkv_kernel(x_ref, w_ref, o_ref, wbf_ref, *scratch, dilation):
    _cast_weight_tile(jnp.logical_and(pl.program_id(1) == 0, pl.program_id(2) == 0), w_ref, wbf_ref)
    acc = jnp.dot(x_ref[...], wbf_ref[...], preferred_element_type=jnp.float32)
    if dilation == 1:
        o_ref[0] = acc.astype(o_ref.dtype)
        return
    acc_ref = scratch[0]
    slabs, tm = acc_ref.shape[0], acc_ref.shape[1]
    for c in range(slabs):
        acc_ref[c] = acc[:, c * LANES:(c + 1) * LANES]
    if dilation == SUBLANE_STRIDE:
        for r in range(dilation):
            for c in range(slabs):
                o_ref[r, :, c * LANES:(c + 1) * LANES] = (
                    acc_ref[c, pl.ds(r, tm // dilation, stride=dilation), :].astype(o_ref.dtype))
        return
    assert dilation == SUBLANE_STRIDE * SUBLANE_STRIDE
    mid_ref = scratch[1]
    quarter = tm // SUBLANE_STRIDE
    for r0 in range(SUBLANE_STRIDE):
        for c in range(slabs):
            mid_ref[c, pl.ds(r0 * quarter, quarter), :] = acc_ref[c, pl.ds(r0, quarter, stride=SUBLANE_STRIDE), :]
    for r0 in range(SUBLANE_STRIDE):
        for r1 in range(SUBLANE_STRIDE):
            for c in range(slabs):
                o_ref[SUBLANE_STRIDE * r1 + r0, :, c * LANES:(c + 1) * LANES] = (
                    mid_ref[c, pl.ds(r0 * quarter + r1, tm // dilation, stride=SUBLANE_STRIDE), :]
                    .astype(o_ref.dtype))


def _proj_gz_kernel(x_ref, w_ref, b_ref, o_ref, wbf_ref):
    j = pl.program_id(0)
    _cast_weight_tile(pl.program_id(1) == 0, w_ref, wbf_ref)
    is_gate = j < GZ_TILE_ZATTN
    acc = jnp.dot(x_ref[...], wbf_ref[...], preferred_element_type=jnp.float32)
    t = acc + jnp.where(is_gate, b_ref[...], 0.0)
    sg = 0.5 * jnp.tanh(0.5 * t) + 0.5
    o_ref[...] = (sg * jnp.where(is_gate, 1.0, t)).astype(o_ref.dtype)


def _proj_u_kernel(x_ref, w_ref, u_ref, xbf_ref, wbf_ref):
    _cast_weight_tile(pl.program_id(0) == 0, w_ref, wbf_ref)
    xb = x_ref[...].astype(xbf_ref.dtype)
    xbf_ref[...] = xb
    u_ref[...] = jnp.dot(xb, wbf_ref[...], preferred_element_type=jnp.float32)


def _project_u(x2, w_in, tm=512):
    m = x2.shape[0]
    return pl.pallas_call(
        _proj_u_kernel,
        grid=(m // tm,),
        in_specs=[
            pl.BlockSpec((tm, D_MODEL), lambda i: (i, 0)),
            pl.BlockSpec((D_MODEL, COL_TILE), lambda i: (0, W_TILE_U), pipeline_mode=pl.Buffered(1)),
        ],
        out_specs=[
            pl.BlockSpec((tm, COL_TILE), lambda i: (i, 0)),
            pl.BlockSpec((tm, D_MODEL), lambda i: (i, 0)),
        ],
        out_shape=[
            jax.ShapeDtypeStruct((m, POOL_WIDTH), jnp.float32),
            jax.ShapeDtypeStruct((m, D_MODEL), jnp.bfloat16),
        ],
        scratch_shapes=[pltpu.VMEM((D_MODEL, COL_TILE), jnp.bfloat16)],
        compiler_params=pltpu.CompilerParams(
            dimension_semantics=("arbitrary",),
            vmem_limit_bytes=PROJ_VMEM_LIMIT),
        name="proj_u",
    )(x2, w_in)


def _project_qkv(x2, w_in, batch, group, dilation, tm=1024):
    m = x2.shape[0]
    tiles_per_batch = m // batch // tm
    sub = tm // dilation
    n_row_scratch = {1: 0, SUBLANE_STRIDE: 1, SUBLANE_STRIDE * SUBLANE_STRIDE: 2}[dilation]
    return pl.pallas_call(
        functools.partial(_proj_qkv_kernel, dilation=dilation),
        grid=(3, batch, tiles_per_batch),
        in_specs=[
            pl.BlockSpec((tm, D_MODEL), lambda j, b, i: (b * tiles_per_batch + i, 0)),
            pl.BlockSpec((D_MODEL, COL_TILE), lambda j, b, i: (0, N_GROUPS * j + group)),
        ],
        out_specs=pl.BlockSpec((None, None, dilation, sub, COL_TILE), lambda j, b, i: (j, b, 0, i, 0)),
        out_shape=jax.ShapeDtypeStruct((3, batch, dilation, m // batch // dilation, ATTN_WIDTH), jnp.bfloat16),
        scratch_shapes=[pltpu.VMEM((D_MODEL, COL_TILE), jnp.bfloat16)]
        + [pltpu.VMEM((COL_TILE // LANES, tm, LANES), jnp.float32)] * n_row_scratch,
        compiler_params=pltpu.CompilerParams(
            dimension_semantics=("arbitrary", "arbitrary", "arbitrary"),
            vmem_limit_bytes=PROJ_VMEM_LIMIT),
        name=f"proj_qkv_g{group}",
    )(x2, w_in)


def _project_gates_z(x2, w_in, b_gate2, tm=1024):
    m = x2.shape[0]

    def w_tile(j):
        return jnp.where(j < GZ_TILE_ZATTN, W_TILE_GATES + j,
                         jnp.where(j == GZ_TILE_ZATTN, W_TILE_ZATTN, W_TILE_ZPOOL))

    return pl.pallas_call(
        _proj_gz_kernel,
        grid=(GZ_TILES, m // tm),
        in_specs=[
            pl.BlockSpec((tm, D_MODEL), lambda j, i: (i, 0)),
            pl.BlockSpec((D_MODEL, COL_TILE), lambda j, i: (0, w_tile(j))),
            pl.BlockSpec((1, COL_TILE), lambda j, i: (0, jnp.minimum(j, GZ_TILE_ZATTN - 1))),
        ],
        out_specs=pl.BlockSpec((tm, COL_TILE), lambda j, i: (i, j)),
        out_shape=jax.ShapeDtypeStruct((m, GZ_TILES * COL_TILE), jnp.bfloat16),
        scratch_shapes=[pltpu.VMEM((D_MODEL, COL_TILE), jnp.bfloat16)],
        compiler_params=pltpu.CompilerParams(
            dimension_semantics=("arbitrary", "arbitrary"),
            vmem_limit_bytes=PROJ_VMEM_LIMIT),
        name="proj_gates_z",
    )(x2, w_in, b_gate2)


def _attn_kernel(q_ref, k_ref, v_ref, o_ref, lse_ref, kc_ref, vc_ref, *, n_blocks, n_chunks):
    chunk = pl.program_id(2)
    bq = ATTN_BLOCK
    scale = HEAD_DIM ** -0.5

    @pl.when(chunk == 0)
    def _():
        kc_ref[...] = jnp.zeros_like(kc_ref)
        vc_ref[...] = jnp.zeros_like(vc_ref)

    row = lax.broadcasted_iota(jnp.int32, (bq, 2 * bq), 0)
    col = lax.broadcasted_iota(jnp.int32, (bq, 2 * bq), 1)
    dist = bq + row - col
    band = jnp.logical_and(dist >= 0, dist <= bq)
    band_first = jnp.logical_and(band, col >= jnp.where(chunk > 0, 0, bq))
    lane = lax.broadcasted_iota(jnp.int32, (bq, LANES), 1)

    def block(r0, kcat_of, vcat_of, mask):
        heads = [slice(h * HEAD_DIM, (h + 1) * HEAD_DIM) for h in range(N_HEADS)]
        ss = [jnp.where(mask,
                        lax.dot_general(q_ref[pl.ds(r0, bq), hs], kcat_of(hs), (((1,), (1,)), ((), ())),
                                        preferred_element_type=jnp.float32),
                        NEG_INF) for hs in heads]
        ms = [jnp.max(s, axis=-1, keepdims=True) for s in ss]
        es = [jnp.exp2((s - m) * (scale * LOG2_E)) for s, m in zip(ss, ms)]
        dens = [jnp.sum(e, axis=-1, keepdims=True) for e in es]
        outs = [jnp.dot(e.astype(o_ref.dtype), vcat_of(hs), preferred_element_type=jnp.float32)
                for e, hs in zip(es, heads)]
        lse_tile = jnp.zeros((bq, LANES), jnp.float32)
        for h, hs in enumerate(heads):
            o_ref[pl.ds(r0, bq), hs] = (outs[h] / dens[h]).astype(o_ref.dtype)
            lse_tile = jnp.where(lane == h, ms[h] * scale + jnp.log(dens[h]), lse_tile)
        lse_ref[pl.ds(r0, bq), :] = lse_tile

    block(0,
          lambda hs: jnp.concatenate([kc_ref[:, hs], k_ref[0:bq, hs]], axis=0),
          lambda hs: jnp.concatenate([vc_ref[:, hs], v_ref[0:bq, hs]], axis=0),
          band_first)

    def body(blk, carry):
        r0 = pl.multiple_of(blk * bq, bq)
        rk = pl.multiple_of(blk * bq - bq, bq)
        block(r0,
              lambda hs: k_ref[pl.ds(rk, 2 * bq), hs],
              lambda hs: v_ref[pl.ds(rk, 2 * bq), hs],
              band)
        return carry

    if n_blocks > 1:
        lax.fori_loop(1, n_blocks, body, 0)

    if n_chunks > 1:
        rl = (n_blocks - 1) * bq
        kc_ref[...] = k_ref[rl:rl + bq, :]
        vc_ref[...] = v_ref[rl:rl + bq, :]


def _dilated_attention(qkv, group, n_blocks):
    _, b, dilation, sub_len, _ = qkv.shape
    rows = n_blocks * ATTN_BLOCK
    n_chunks = sub_len // rows

    def in_spec(tile):
        return pl.BlockSpec((None, None, None, rows, COL_TILE), lambda bb, r, c: (tile, bb, r, c, 0))

    return pl.pallas_call(
        functools.partial(_attn_kernel, n_blocks=n_blocks, n_chunks=n_chunks),
        grid=(b, dilation, n_chunks),
        in_specs=[in_spec(0), in_spec(1), in_spec(2)],
        out_specs=[
            pl.BlockSpec((None, None, rows, ATTN_WIDTH), lambda bb, r, c: (bb, r, c, 0)),
            pl.BlockSpec((None, None, rows, LANES), lambda bb, r, c: (bb, r, c, 0)),
        ],
        out_shape=[
            jax.ShapeDtypeStruct((b, dilation, sub_len, ATTN_WIDTH), jnp.bfloat16),
            jax.ShapeDtypeStruct((b, dilation, sub_len, LANES), jnp.float32),
        ],
        scratch_shapes=[
            pltpu.VMEM((ATTN_BLOCK, ATTN_WIDTH), jnp.bfloat16),
            pltpu.VMEM((ATTN_BLOCK, ATTN_WIDTH), jnp.bfloat16),
        ],
        compiler_params=pltpu.CompilerParams(
            dimension_semantics=("arbitrary", "arbitrary", "arbitrary"),
            vmem_limit_bytes=40 * MIB),
        name=f"dilated_attn_g{group}",
    )(qkv, qkv, qkv)


def _to_token_order(src_ref, dst_ref, slab, src_cols):
    dilation, sub = src_ref.shape[0], src_ref.shape[1]
    for r in range(dilation):
        dst_ref[slab, pl.ds(r, sub, stride=dilation), :] = src_ref[r, :, src_cols].astype(dst_ref.dtype)


def _combine_weights(l1_ref, l2_ref, l3_ref, ln2_ref, ln3_ref):
    _to_token_order(l2_ref, ln2_ref, 0, slice(None))
    _to_token_order(l3_ref, ln3_ref, 0, slice(None))
    l1 = l1_ref[...]
    l2 = ln2_ref[0]
    l3 = ln3_ref[0]
    mx = jnp.maximum(jnp.maximum(l1, l2), l3)
    e1 = jnp.exp(l1 - mx)
    e2 = jnp.exp(l2 - mx)
    e3 = jnp.exp(l3 - mx)
    tot = e1 + e2 + e3
    return e1 / tot, e2 / tot, e3 / tot


def _combine_head(h, weights, o1_ref, o2_ref, o3_ref, za_ref, on2_ref, on3_ref, ya_ref):
    hs = slice(h * HEAD_DIM, (h + 1) * HEAD_DIM)
    _to_token_order(o2_ref, on2_ref, h, hs)
    _to_token_order(o3_ref, on3_ref, h, hs)
    w1, w2, w3 = weights
    o = (w1[:, h:h + 1] * o1_ref[:, hs].astype(jnp.float32)
         + w2[:, h:h + 1] * on2_ref[h]
         + w3[:, h:h + 1] * on3_ref[h])
    ya_ref[:, hs] = (o * za_ref[:, hs].astype(jnp.float32)).astype(ya_ref.dtype)


def _pool_window_sums(i, u_ref, halo_ref, ext_ref, s2_ref, s4_ref, s8_ref, *, tm):
    rows = HALO + tm
    ext_ref[0:HALO, :] = jnp.where(i > 0, halo_ref[...], 0.0)
    ext_ref[HALO:, :] = u_ref[...]
    g = POOL_GROUP
    s2_ref[8:rows, :] = ext_ref[8:rows, :] + ext_ref[7:rows - 1, :]
    s4_ref[16:rows, :] = s2_ref[16:rows, g:] + s2_ref[14:rows - 2, g:]
    s8_ref[24:rows, :] = s4_ref[24:rows, g:] + s4_ref[20:rows - 4, g:]


def _pool_group(k, i, zp_ref, w_pool_ref, pool_scale_ref, ext_ref, s2_ref, s4_ref, s8_ref, yp_ref, *, tm):
    rows = HALO + tm
    g = POOL_GROUP
    w = POOL_WINDOWS[k]
    if k < 3:
        window_sum = (s2_ref, s4_ref, s8_ref)[k][HALO:rows, 0:g]
    else:
        window_sum = s8_ref[HALO:rows, g:] + s8_ref[HALO - 8:rows - 8, g:]
    cs = slice(k * g, (k + 1) * g)
    pos = i * tm + lax.broadcasted_iota(jnp.int32, (tm, 1), 0)
    cnt = jnp.minimum(pos + 1, w).astype(jnp.float32)
    p = window_sum / cnt - ext_ref[HALO:rows, cs]
    y = jnp.dot(p.astype(jnp.bfloat16), w_pool_ref[k], preferred_element_type=jnp.float32)
    y = y * pool_scale_ref[:, cs] * zp_ref[:, cs].astype(jnp.float32)
    yp_ref[:, cs] = y.astype(yp_ref.dtype)


def _layer_norm_rows(r_ref, gamma_ref, beta_ref):
    r = r_ref[...]
    mu = jnp.mean(r, axis=-1, keepdims=True)
    rc = r - mu
    var = jnp.mean(rc * rc, axis=-1, keepdims=True)
    r_ref[...] = rc * lax.rsqrt(var + LN_EPS) * gamma_ref[...] + beta_ref[...]


def _merge_kernel(o1_ref, o2_ref, o3_ref, l1_ref, l2_ref, l3_ref, za_ref, zp_ref, u_ref, halo_ref,
                  g_attn_ref, g_pool_ref, x_ref, w_pool_ref, pool_scale_ref, wpa_ref, wpp_ref, w_out_ref,
                  gamma_ref, beta_ref, out_ref, ext_ref, s2_ref, s4_ref, s8_ref, on2_ref, on3_ref, ln2_ref, ln3_ref,
                  ya0_ref, yp0_ref, ya1_ref, yp1_ref, merged_ref, *, tm, n_tiles, tiles_per_seq):
    s = pl.program_id(0)
    i_prep = jnp.minimum(s, n_tiles - 1) % tiles_per_seq

    @pl.when(s == 0)
    def _():
        ya1_ref[...] = jnp.zeros_like(ya1_ref)
        yp1_ref[...] = jnp.zeros_like(yp1_ref)

    def step(ya_w, yp_w, ya_r, yp_r):
        chunks = [slice(c * MERGE_CHUNK, (c + 1) * MERGE_CHUNK) for c in range(D_MODEL // MERGE_CHUNK)]
        ya = ya_r[...]
        yp = yp_r[...]
        weights = _combine_weights(l1_ref, l2_ref, l3_ref, ln2_ref, ln3_ref)
        pas = []
        for c, cols in enumerate(chunks):
            pas.append(jnp.dot(ya, wpa_ref[:, cols], preferred_element_type=jnp.float32))
            _combine_head(c, weights, o1_ref, o2_ref, o3_ref, za_ref, on2_ref, on3_ref, ya_w)
        _pool_window_sums(i_prep, u_ref, halo_ref, ext_ref, s2_ref, s4_ref, s8_ref, tm=tm)
        for c, cols in enumerate(chunks):
            pp = jnp.dot(yp, wpp_ref[:, cols], preferred_element_type=jnp.float32)
            merged_ref[:, cols] = (g_attn_ref[:, cols].astype(jnp.float32) * pas[c]
                                   + g_pool_ref[:, cols].astype(jnp.float32) * pp).astype(merged_ref.dtype)
            if c % 2 == 1:
                _pool_group(c // 2, i_prep, zp_ref, w_pool_ref, pool_scale_ref, ext_ref, s2_ref, s4_ref, s8_ref,
                            yp_w, tm=tm)
        merged = merged_ref[...]
        for cols in chunks:
            out = jnp.dot(merged, w_out_ref[:, cols], preferred_element_type=jnp.float32)
            out_ref[:, cols] = DEEPNORM_ALPHA * x_ref[:, cols] + out
        _layer_norm_rows(out_ref, gamma_ref, beta_ref)

    @pl.when(s % 2 == 0)
    def _():
        step(ya0_ref, yp0_ref, ya1_ref, yp1_ref)

    @pl.when(s % 2 == 1)
    def _():
        step(ya1_ref, yp1_ref, ya0_ref, yp0_ref)


def _merge(os_, lses, gz3, u3, x, w_pool_bf, pool_scale2, wpa_bf, wpp_bf, w_out_bf, gamma2, beta2, tm=256):
    b, s, _ = x.shape
    tiles_per_seq = s // tm
    n_tiles = b * tiles_per_seq

    def prep_tile(step):
        t = jnp.minimum(step, n_tiles - 1)
        return t // tiles_per_seq, t % tiles_per_seq

    def proj_tile(step):
        t = jnp.maximum(step - 1, 0)
        return t // tiles_per_seq, t % tiles_per_seq

    def tok(which, width, tile=0):
        return pl.BlockSpec((None, tm, width), lambda st: (*which(st), tile))

    def grouped(arr):
        dilation, width = arr.shape[1], arr.shape[3]

        def index(st):
            bb, i = prep_tile(st)
            return bb, 0, i, 0
        if dilation == 1:
            return pl.BlockSpec((None, None, tm, width), index)
        return pl.BlockSpec((None, dilation, tm // dilation, width), index)

    def halo_index(st):
        bb, i = prep_tile(st)
        return bb, jnp.maximum(i * (tm // HALO) - 1, 0), 0

    def const(shape):
        nd = len(shape)
        return pl.BlockSpec(shape, lambda st: (0,) * nd, pipeline_mode=pl.Buffered(1))

    in_specs = [
        grouped(os_[0]), grouped(os_[1]), grouped(os_[2]),
        grouped(lses[0]), grouped(lses[1]), grouped(lses[2]),
        tok(prep_tile, COL_TILE, GZ_TILE_ZATTN), tok(prep_tile, COL_TILE, GZ_TILE_ZPOOL),
        tok(prep_tile, POOL_WIDTH),
        pl.BlockSpec((None, HALO, POOL_WIDTH), halo_index),
        tok(proj_tile, D_MODEL, 0), tok(proj_tile, D_MODEL, 1),
        tok(proj_tile, D_MODEL),
        const((len(POOL_WINDOWS), POOL_GROUP, POOL_GROUP)),
        const((1, POOL_WIDTH)),
        const((ATTN_WIDTH, D_MODEL)),
        const((POOL_WIDTH, D_MODEL)),
        const((D_MODEL, D_MODEL)),
        const((1, D_MODEL)),
        const((1, D_MODEL)),
    ]
    f32 = jnp.float32
    return pl.pallas_call(
        functools.partial(_merge_kernel, tm=tm, n_tiles=n_tiles, tiles_per_seq=tiles_per_seq),
        grid=(n_tiles + 1,),
        in_specs=in_specs,
        out_specs=tok(proj_tile, D_MODEL),
        out_shape=jax.ShapeDtypeStruct((b, s, D_MODEL), f32),
        scratch_shapes=[
            pltpu.VMEM((HALO + tm, POOL_WIDTH), f32),
            pltpu.VMEM((HALO + tm, POOL_WIDTH), f32),
            pltpu.VMEM((HALO + tm, POOL_WIDTH - POOL_GROUP), f32),
            pltpu.VMEM((HALO + tm, POOL_WIDTH - 2 * POOL_GROUP), f32),
            pltpu.VMEM((N_HEADS, tm, HEAD_DIM), f32),
            pltpu.VMEM((N_HEADS, tm, HEAD_DIM), f32),
            pltpu.VMEM((1, tm, LANES), f32),
            pltpu.VMEM((1, tm, LANES), f32),
            pltpu.VMEM((tm, ATTN_WIDTH), jnp.bfloat16),
            pltpu.VMEM((tm, POOL_WIDTH), jnp.bfloat16),
            pltpu.VMEM((tm, ATTN_WIDTH), jnp.bfloat16),
            pltpu.VMEM((tm, POOL_WIDTH), jnp.bfloat16),
            pltpu.VMEM((tm, D_MODEL), jnp.bfloat16),
        ],
        compiler_params=pltpu.CompilerParams(
            dimension_semantics=("arbitrary",),
            vmem_limit_bytes=56 * MIB),
        name="merge_out_ln",
    )(os_[0], os_[1], os_[2], lses[0], lses[1], lses[2], gz3, gz3, u3, u3, gz3, gz3, x,
      w_pool_bf, pool_scale2, wpa_bf, wpp_bf, w_out_bf, gamma2, beta2)


def _layer(x, w_in, b_gate, w_pool, pool_scale, w_proj_attn, w_proj_pool, w_out, ln_gamma, ln_beta):
    b, s, d = x.shape
    bf = jnp.bfloat16
    u2, x2 = _project_u(x.reshape(b * s, d), w_in)
    os_, lses = [], []
    for g, (window, dilation) in enumerate(DILATED_GROUPS):
        assert window // dilation == ATTN_BLOCK
        qkv = _project_qkv(x2, w_in, b, g, dilation)
        n_blocks = min(8, s // dilation // ATTN_BLOCK)
        o, lse = _dilated_attention(qkv, g, n_blocks)
        os_.append(o)
        lses.append(lse)
    gz2 = _project_gates_z(x2, w_in, b_gate.reshape(1, -1))
    gz3 = gz2.reshape(b, s, GZ_TILES * COL_TILE)
    u3 = u2.reshape(b, s, POOL_WIDTH)
    return _merge(os_, lses, gz3, u3, x, w_pool.astype(bf), pool_scale.reshape(1, -1),
                  w_proj_attn.astype(bf), w_proj_pool.astype(bf), w_out.astype(bf),
                  ln_gamma.reshape(1, -1), ln_beta.reshape(1, -1))


def kernel(x, w_in, b_gate, w_pool, pool_scale, w_proj_attn, w_proj_pool, w_out, ln_gamma, ln_beta):
    depth = w_in.shape[0]
    for layer in range(depth):
        x = _layer(x, w_in[layer], b_gate[layer], w_pool[layer], pool_scale[layer],
                   w_proj_attn[layer], w_proj_pool[layer], w_out[layer],
                   ln_gamma[layer], ln_beta[layer])
    return x
```

```python
import functools

import jax
import jax.numpy as jnp
from jax import lax
from jax.experimental import pallas as pl
from jax.experimental.pallas import tpu as pltpu

D_MODEL = 2048
HEAD_DIM = 128
N_HEADS = 8
ATTN_WIDTH = N_HEADS * HEAD_DIM
DILATED_GROUPS = ((128, 1), (512, 4), (2048, 16))
N_GROUPS = len(DILATED_GROUPS)
POOL_WINDOWS = (2, 4, 8, 16)
POOL_WIDTH = D_MODEL // 2
POOL_GROUP = POOL_WIDTH // len(POOL_WINDOWS)
DEEPNORM_ALPHA = 2.0 ** 0.25
LN_EPS = 1e-5
NEG_INF = -1e30
LOG2_E = 1.4426950408889634

LANES = 128
MIB = 1024 * 1024
SUBLANE_STRIDE = 4
PROJ_VMEM_LIMIT = 54 * MIB

COL_TILE = 1024
W_TILE_ZATTN = 9
W_TILE_U = 10
W_TILE_ZPOOL = 11
W_TILE_GATES = 12
GZ_TILES = 6
GZ_TILE_ZATTN = 4
GZ_TILE_ZPOOL = 5
GZ_ROW_TILES = 16
ATTN_STEPS = 32

ATTN_BLOCK = 128
MERGE_CHUNK = 256
HALO = 32


def _cast_weight_tile(first_step, w_ref, wbf_ref):
    @pl.when(first_step)
    def _():
        wbf_ref[...] = w_ref[...].astype(wbf_ref.dtype)


def _proj_qkv_kernel(x_ref, w_ref, o_ref, wbf_ref, *scratch, dilation):
    _cast_weight_tile(jnp.logical_and(pl.program_id(1) == 0, pl.program_id(2) == 0), w_ref, wbf_ref)
    acc = jnp.dot(x_ref[...], wbf_ref[...], preferred_element_type=jnp.float32)
    if dilation == 1:
        o_ref[0] = acc.astype(o_ref.dtype)
        return
    acc_ref = scratch[0]
    slabs, tm = acc_ref.shape[0], acc_ref.shape[1]
    for c in range(slabs):
        acc_ref[c] = acc[:, c * LANES:(c + 1) * LANES]
    if dilation == SUBLANE_STRIDE:
        for r in range(dilation):
            for c in range(slabs):
                o_ref[r, :, c * LANES:(c + 1) * LANES] = (
                    acc_ref[c, pl.ds(r, tm // dilation, stride=dilation), :].astype(o_ref.dtype))
        return
    assert dilation == SUBLANE_STRIDE * SUBLANE_STRIDE
    mid_ref = scratch[1]
    quarter = tm // SUBLANE_STRIDE
    for r0 in range(SUBLANE_STRIDE):
        for c in range(slabs):
            mid_ref[c, pl.ds(r0 * quarter, quarter), :] = acc_ref[c, pl.ds(r0, quarter, stride=SUBLANE_STRIDE), :]
    for r0 in range(SUBLANE_STRIDE):
        for r1 in range(SUBLANE_STRIDE):
            for c in range(slabs):
                o_ref[SUBLANE_STRIDE * r1 + r0, :, c * LANES:(c + 1) * LANES] = (
                    mid_ref[c, pl.ds(r0 * quarter + r1, tm // dilation, stride=SUBLANE_STRIDE), :]
                    .astype(o_ref.dtype))


def _proj_u_kernel(x_ref, w_ref, u_ref, xbf_ref, wbf_ref):
    _cast_weight_tile(pl.program_id(0) == 0, w_ref, wbf_ref)
    xb = x_ref[...].astype(xbf_ref.dtype)
    xbf_ref[...] = xb
    u_ref[...] = jnp.dot(xb, wbf_ref[...], preferred_element_type=jnp.float32)


def _project_u(x2, w_in, tm=512):
    m = x2.shape[0]
    return pl.pallas_call(
        _proj_u_kernel,
        grid=(m // tm,),
        in_specs=[
            pl.BlockSpec((tm, D_MODEL), lambda i: (i, 0)),
            pl.BlockSpec((D_MODEL, COL_TILE), lambda i: (0, W_TILE_U), pipeline_mode=pl.Buffered(1)),
        ],
        out_specs=[
            pl.BlockSpec((tm, COL_TILE), lambda i: (i, 0)),
            pl.BlockSpec((tm, D_MODEL), lambda i: (i, 0)),
        ],
        out_shape=[
            jax.ShapeDtypeStruct((m, POOL_WIDTH), jnp.float32),
            jax.ShapeDtypeStruct((m, D_MODEL), jnp.bfloat16),
        ],
        scratch_shapes=[pltpu.VMEM((D_MODEL, COL_TILE), jnp.bfloat16)],
        compiler_params=pltpu.CompilerParams(
            dimension_semantics=("arbitrary",),
            vmem_limit_bytes=PROJ_VMEM_LIMIT),
        name="proj_u",
    )(x2, w_in)


def _project_qkv(x2, w_in, batch, group, dilation, tm=1024):
    m = x2.shape[0]
    tiles_per_batch = m // batch // tm
    sub = tm // dilation
    n_row_scratch = {1: 0, SUBLANE_STRIDE: 1, SUBLANE_STRIDE * SUBLANE_STRIDE: 2}[dilation]
    return pl.pallas_call(
        functools.partial(_proj_qkv_kernel, dilation=dilation),
        grid=(3, batch, tiles_per_batch),
        in_specs=[
            pl.BlockSpec((tm, D_MODEL), lambda j, b, i: (b * tiles_per_batch + i, 0)),
            pl.BlockSpec((D_MODEL, COL_TILE), lambda j, b, i: (0, N_GROUPS * j + group)),
        ],
        out_specs=pl.BlockSpec((None, None, dilation, sub, COL_TILE), lambda j, b, i: (j, b, 0, i, 0)),
        out_shape=jax.ShapeDtypeStruct((3, batch, dilation, m // batch // dilation, ATTN_WIDTH), jnp.bfloat16),
        scratch_shapes=[pltpu.VMEM((D_MODEL, COL_TILE), jnp.bfloat16)]
        + [pltpu.VMEM((COL_TILE // LANES, tm, LANES), jnp.float32)] * n_row_scratch,
        compiler_params=pltpu.CompilerParams(
            dimension_semantics=("arbitrary", "arbitrary", "arbitrary"),
            vmem_limit_bytes=PROJ_VMEM_LIMIT),
        name=f"proj_qkv_g{group}",
    )(x2, w_in)


def _attention_block(blk, first_chunk, q_ref, k_ref, v_ref, o_ref, lse_ref, kc_ref, vc_ref):
    bq = ATTN_BLOCK
    scale = HEAD_DIM ** -0.5
    row = lax.broadcasted_iota(jnp.int32, (bq, 2 * bq), 0)
    col = lax.broadcasted_iota(jnp.int32, (bq, 2 * bq), 1)
    dist = bq + row - col
    mask = jnp.logical_and(dist >= 0, dist <= bq)
    r0 = blk * bq
    if blk == 0:
        mask = jnp.logical_and(mask, col >= jnp.where(first_chunk, bq, 0))

        def kcat_of(hs):
            return jnp.concatenate([kc_ref[:, hs], k_ref[0:bq, hs]], axis=0)

        def vcat_of(hs):
            return jnp.concatenate([vc_ref[:, hs], v_ref[0:bq, hs]], axis=0)
    else:
        def kcat_of(hs):
            return k_ref[r0 - bq:r0 + bq, hs]

        def vcat_of(hs):
            return v_ref[r0 - bq:r0 + bq, hs]

    heads = [slice(h * HEAD_DIM, (h + 1) * HEAD_DIM) for h in range(N_HEADS)]
    ss = [jnp.where(mask,
                    lax.dot_general(q_ref[r0:r0 + bq, hs], kcat_of(hs), (((1,), (1,)), ((), ())),
                                    preferred_element_type=jnp.float32),
                    NEG_INF) for hs in heads]
    ms = [jnp.max(s, axis=-1, keepdims=True) for s in ss]
    es = [jnp.exp2((s - m) * (scale * LOG2_E)) for s, m in zip(ss, ms)]
    dens = [jnp.sum(e, axis=-1, keepdims=True) for e in es]
    outs = [jnp.dot(e.astype(o_ref.dtype), vcat_of(hs), preferred_element_type=jnp.float32)
            for e, hs in zip(es, heads)]
    lane = lax.broadcasted_iota(jnp.int32, (bq, LANES), 1)
    lse_tile = jnp.zeros((bq, LANES), jnp.float32)
    for h, hs in enumerate(heads):
        o_ref[r0:r0 + bq, hs] = (outs[h] / dens[h]).astype(o_ref.dtype)
        lse_tile = jnp.where(lane == h, ms[h] * scale + jnp.log(dens[h]), lse_tile)
    lse_ref[r0:r0 + bq, :] = lse_tile


def _gz_attn_kernel(x_ref, w_ref, b_ref, *refs, dilations):
    n_groups = len(dilations)
    qkv_refs = refs[:3 * n_groups]
    gz_ref = refs[3 * n_groups]
    attn_out_refs = refs[3 * n_groups + 1:5 * n_groups + 1]
    wbf_ref, kc_ref, vc_ref = refs[5 * n_groups + 1:]
    s = pl.program_id(0)
    _cast_weight_tile(s % GZ_ROW_TILES == 0, w_ref, wbf_ref)
    chunk_cols = COL_TILE // 4

    for g, dilation in enumerate(dilations):
        q_ref, k_ref, v_ref = qkv_refs[3 * g:3 * g + 3]
        o_ref, lse_ref = attn_out_refs[2 * g:2 * g + 2]
        chunks_per_subseq = GZ_ROW_TILES // dilation
        is_gate = 2 * g + 1 < GZ_TILE_ZATTN

        def gz_chunk(c, is_gate=is_gate):
            cols = slice(c * chunk_cols, (c + 1) * chunk_cols)
            acc = jnp.dot(x_ref[...], wbf_ref[:, cols], preferred_element_type=jnp.float32)
            t = acc + b_ref[:, cols] if is_gate else acc
            sg = 0.5 * jnp.tanh(0.5 * t) + 0.5
            gz_ref[:, cols] = (sg if is_gate else sg * t).astype(gz_ref.dtype)

        @pl.when(s // ATTN_STEPS == g)
        def _(q_ref=q_ref, k_ref=k_ref, v_ref=v_ref, o_ref=o_ref, lse_ref=lse_ref,
              chunks_per_subseq=chunks_per_subseq, gz_chunk=gz_chunk):
            first_chunk = s % chunks_per_subseq == 0

            @pl.when(first_chunk)
            def _():
                kc_ref[...] = jnp.zeros_like(kc_ref)
                vc_ref[...] = jnp.zeros_like(vc_ref)

            gz_chunk(0)
            _attention_block(0, first_chunk, q_ref, k_ref, v_ref, o_ref, lse_ref, kc_ref, vc_ref)
            gz_chunk(1)
            gz_chunk(2)
            _attention_block(1, first_chunk, q_ref, k_ref, v_ref, o_ref, lse_ref, kc_ref, vc_ref)
            gz_chunk(3)
            kc_ref[...] = k_ref[ATTN_BLOCK:2 * ATTN_BLOCK, :]
            vc_ref[...] = v_ref[ATTN_BLOCK:2 * ATTN_BLOCK, :]


def _gates_z_and_attention(x2, w_in, b_gate2, qkvs):
    m = x2.shape[0]
    tm = m // GZ_ROW_TILES
    rows = 2 * ATTN_BLOCK
    n_groups = len(qkvs)
    assert GZ_TILES * GZ_ROW_TILES == n_groups * ATTN_STEPS
    dilations = tuple(q.shape[2] for q in qkvs)

    def w_tile(j):
        return jnp.where(j < GZ_TILE_ZATTN, W_TILE_GATES + j,
                         jnp.where(j == GZ_TILE_ZATTN, W_TILE_ZATTN, W_TILE_ZPOOL))

    def chunk_of(g, qkv):
        _, batch, dilation, sub_len, _ = qkv.shape
        chunks_per_subseq = sub_len // rows
        per_batch = dilation * chunks_per_subseq
        assert batch * per_batch == ATTN_STEPS and GZ_ROW_TILES % dilation == 0

        def index(step):
            a = jnp.clip(step - g * ATTN_STEPS, 0, ATTN_STEPS - 1)
            rc = a % per_batch
            return a // per_batch, rc // chunks_per_subseq, rc % chunks_per_subseq
        return index

    in_specs = [
        pl.BlockSpec((tm, D_MODEL), lambda st: (st % GZ_ROW_TILES, 0)),
        pl.BlockSpec((D_MODEL, COL_TILE), lambda st: (0, w_tile(st // GZ_ROW_TILES))),
        pl.BlockSpec((1, COL_TILE), lambda st: (0, jnp.minimum(st // GZ_ROW_TILES, GZ_TILE_ZATTN - 1))),
    ]
    out_specs = [pl.BlockSpec((tm, COL_TILE), lambda st: (st % GZ_ROW_TILES, st // GZ_ROW_TILES))]
    out_shape = [jax.ShapeDtypeStruct((m, GZ_TILES * COL_TILE), jnp.bfloat16)]
    operands = [x2, w_in, b_gate2]
    for g, qkv in enumerate(qkvs):
        _, batch, dilation, sub_len, _ = qkv.shape
        index = chunk_of(g, qkv)
        for t in range(3):
            in_specs.append(pl.BlockSpec((None, None, None, rows, ATTN_WIDTH),
                                         lambda st, t=t, index=index: (t, *index(st), 0)))
            operands.append(qkv)
        out_specs.append(pl.BlockSpec((None, None, rows, ATTN_WIDTH), lambda st, index=index: (*index(st), 0)))
        out_specs.append(pl.BlockSpec((None, None, rows, LANES), lambda st, index=index: (*index(st), 0)))
        out_shape.append(jax.ShapeDtypeStruct((batch, dilation, sub_len, ATTN_WIDTH), jnp.bfloat16))
        out_shape.append(jax.ShapeDtypeStruct((batch, dilation, sub_len, LANES), jnp.float32))
    results = pl.pallas_call(
        functools.partial(_gz_attn_kernel, dilations=dilations),
        grid=(GZ_TILES * GZ_ROW_TILES,),
        in_specs=in_specs,
        out_specs=out_specs,
        out_shape=out_shape,
        scratch_shapes=[
            pltpu.VMEM((D_MODEL, COL_TILE), jnp.bfloat16),
            pltpu.VMEM((ATTN_BLOCK, ATTN_WIDTH), jnp.bfloat16),
            pltpu.VMEM((ATTN_BLOCK, ATTN_WIDTH), jnp.bfloat16),
        ],
        compiler_params=pltpu.CompilerParams(
            dimension_semantics=("arbitrary",),
            vmem_limit_bytes=PROJ_VMEM_LIMIT),
        name="gates_z_attention",
    )(*operands)
    gz = results[0]
    os_ = [results[1 + 2 * g] for g in range(n_groups)]
    lses = [results[2 + 2 * g] for g in range(n_groups)]
    return gz, os_, lses


def _to_token_order(src_ref, dst_ref, slab, src_cols):
    dilation, sub = src_ref.shape[0], src_ref.shape[1]
    for r in range(dilation):
        dst_ref[slab, pl.ds(r, sub, stride=dilation), :] = src_ref[r, :, src_cols].astype(dst_ref.dtype)


def _combine_weights(l1_ref, l2_ref, l3_ref, ln2_ref, ln3_ref):
    _to_token_order(l2_ref, ln2_ref, 0, slice(None))
    _to_token_order(l3_ref, ln3_ref, 0, slice(None))
    l1 = l1_ref[...]
    l2 = ln2_ref[0]
    l3 = ln3_ref[0]
    mx = jnp.maximum(jnp.maximum(l1, l2), l3)
    e1 = jnp.exp(l1 - mx)
    e2 = jnp.exp(l2 - mx)
    e3 = jnp.exp(l3 - mx)
    tot = e1 + e2 + e3
    return e1 / tot, e2 / tot, e3 / tot


def _combine_head(h, weights, o1_ref, o2_ref, o3_ref, za_ref, on2_ref, on3_ref, ya_ref):
    hs = slice(h * HEAD_DIM, (h + 1) * HEAD_DIM)
    _to_token_order(o2_ref, on2_ref, h, hs)
    _to_token_order(o3_ref, on3_ref, h, hs)
    w1, w2, w3 = weights
    o = (w1[:, h:h + 1] * o1_ref[:, hs].astype(jnp.float32)
         + w2[:, h:h + 1] * on2_ref[h]
         + w3[:, h:h + 1] * on3_ref[h])
    ya_ref[:, hs] = (o * za_ref[:, hs].astype(jnp.float32)).astype(ya_ref.dtype)


def _pool_window_sums(i, u_ref, halo_ref, ext_ref, s2_ref, s4_ref, s8_ref, *, tm):
    rows = HALO + tm
    ext_ref[0:HALO, :] = jnp.where(i > 0, halo_ref[...], 0.0)
    ext_ref[HALO:, :] = u_ref[...]
    g = POOL_GROUP
    s2_ref[8:rows, :] = ext_ref[8:rows, :] + ext_ref[7:rows - 1, :]
    s4_ref[16:rows, :] = s2_ref[16:rows, g:] + s2_ref[14:rows - 2, g:]
    s8_ref[24:rows, :] = s4_ref[24:rows, g:] + s4_ref[20:rows - 4, g:]


def _pool_group(k, i, zp_ref, w_pool_ref, pool_scale_ref, ext_ref, s2_ref, s4_ref, s8_ref, yp_ref, *, tm):
    rows = HALO + tm
    g = POOL_GROUP
    w = POOL_WINDOWS[k]
    if k < 3:
        window_sum = (s2_ref, s4_ref, s8_ref)[k][HALO:rows, 0:g]
    else:
        window_sum = s8_ref[HALO:rows, g:] + s8_ref[HALO - 8:rows - 8, g:]
    cs = slice(k * g, (k + 1) * g)
    pos = i * tm + lax.broadcasted_iota(jnp.int32, (tm, 1), 0)
    cnt = jnp.minimum(pos + 1, w).astype(jnp.float32)
    p = window_sum / cnt - ext_ref[HALO:rows, cs]
    y = jnp.dot(p.astype(jnp.bfloat16), w_pool_ref[k], preferred_element_type=jnp.float32)
    y = y * pool_scale_ref[:, cs] * zp_ref[:, cs].astype(jnp.float32)
    yp_ref[:, cs] = y.astype(yp_ref.dtype)


def _layer_norm_rows(r_ref, gamma_ref, beta_ref):
    r = r_ref[...]
    mu = jnp.mean(r, axis=-1, keepdims=True)
    rc = r - mu
    var = jnp.mean(rc * rc, axis=-1, keepdims=True)
    r_ref[...] = rc * lax.rsqrt(var + LN_EPS) * gamma_ref[...] + beta_ref[...]


def _merge_kernel(o1_ref, o2_ref, o3_ref, l1_ref, l2_ref, l3_ref, za_ref, zp_ref, u_ref, halo_ref,
                  g_attn_ref, g_pool_ref, x_ref, w_pool_ref, pool_scale_ref, wpa_ref, wpp_ref, w_out_ref,
                  gamma_ref, beta_ref, out_ref, ext_ref, s2_ref, s4_ref, s8_ref, on2_ref, on3_ref, ln2_ref, ln3_ref,
                  ya0_ref, yp0_ref, ya1_ref, yp1_ref, merged_ref, *, tm, n_tiles, tiles_per_seq):
    s = pl.program_id(0)
    i_prep = jnp.minimum(s, n_tiles - 1) % tiles_per_seq

    @pl.when(s == 0)
    def _():
        ya1_ref[...] = jnp.zeros_like(ya1_ref)
        yp1_ref[...] = jnp.zeros_like(yp1_ref)

    def step(ya_w, yp_w, ya_r, yp_r):
        chunks = [slice(c * MERGE_CHUNK, (c + 1) * MERGE_CHUNK) for c in range(D_MODEL // MERGE_CHUNK)]
        ya = ya_r[...]
        yp = yp_r[...]
        weights = _combine_weights(l1_ref, l2_ref, l3_ref, ln2_ref, ln3_ref)
        pas = []
        for c, cols in enumerate(chunks):
            pas.append(jnp.dot(ya, wpa_ref[:, cols], preferred_element_type=jnp.float32))
            _combine_head(c, weights, o1_ref, o2_ref, o3_ref, za_ref, on2_ref, on3_ref, ya_w)
        _pool_window_sums(i_prep, u_ref, halo_ref, ext_ref, s2_ref, s4_ref, s8_ref, tm=tm)
        for c, cols in enumerate(chunks):
            pp = jnp.dot(yp, wpp_ref[:, cols], preferred_element_type=jnp.float32)
            merged_ref[:, cols] = (g_attn_ref[:, cols].astype(jnp.float32) * pas[c]
                                   + g_pool_ref[:, cols].astype(jnp.float32) * pp).astype(merged_ref.dtype)
            if c % 2 == 1:
                _pool_group(c // 2, i_prep, zp_ref, w_pool_ref, pool_scale_ref, ext_ref, s2_ref, s4_ref, s8_ref,
                            yp_w, tm=tm)
        merged = merged_ref[...]
        for cols in chunks:
            out = jnp.dot(merged, w_out_ref[:, cols], preferred_element_type=jnp.float32)
            out_ref[:, cols] = DEEPNORM_ALPHA * x_ref[:, cols] + out
        _layer_norm_rows(out_ref, gamma_ref, beta_ref)

    @pl.when(s % 2 == 0)
    def _():
        step(ya0_ref, yp0_ref, ya1_ref, yp1_ref)

    @pl.when(s % 2 == 1)
    def _():
        step(ya1_ref, yp1_ref, ya0_ref, yp0_ref)


def _merge(os_, lses, gz3, u3, x, w_pool_bf, pool_scale2, wpa_bf, wpp_bf, w_out_bf, gamma2, beta2, tm=256):
    b, s, _ = x.shape
    tiles_per_seq = s // tm
    n_tiles = b * tiles_per_seq

    def prep_tile(step):
        t = jnp.minimum(step, n_tiles - 1)
        return t // tiles_per_seq, t % tiles_per_seq

    def proj_tile(step):
        t = jnp.maximum(step - 1, 0)
        return t // tiles_per_seq, t % tiles_per_seq

    def tok(which, width, tile=0):
        return pl.BlockSpec((None, tm, width), lambda st: (*which(st), tile))

    def grouped(arr):
        dilation, width = arr.shape[1], arr.shape[3]

        def index(st):
            bb, i = prep_tile(st)
            return bb, 0, i, 0
        if dilation == 1:
            return pl.BlockSpec((None, None, tm, width), index)
        return pl.BlockSpec((None, dilation, tm // dilation, width), index)

    def halo_index(st):
        bb, i = prep_tile(st)
        return bb, jnp.maximum(i * (tm // HALO) - 1, 0), 0

    def const(shape):
        nd = len(shape)
        return pl.BlockSpec(shape, lambda st: (0,) * nd, pipeline_mode=pl.Buffered(1))

    in_specs = [
        grouped(os_[0]), grouped(os_[1]), grouped(os_[2]),
        grouped(lses[0]), grouped(lses[1]), grouped(lses[2]),
        tok(prep_tile, COL_TILE, GZ_TILE_ZATTN), tok(prep_tile, COL_TILE, GZ_TILE_ZPOOL),
        tok(prep_tile, POOL_WIDTH),
        pl.BlockSpec((None, HALO, POOL_WIDTH), halo_index),
        tok(proj_tile, D_MODEL, 0), tok(proj_tile, D_MODEL, 1),
        tok(proj_tile, D_MODEL),
        const((len(POOL_WINDOWS), POOL_GROUP, POOL_GROUP)),
        const((1, POOL_WIDTH)),
        const((ATTN_WIDTH, D_MODEL)),
        const((POOL_WIDTH, D_MODEL)),
        const((D_MODEL, D_MODEL)),
        const((1, D_MODEL)),
        const((1, D_MODEL)),
    ]
    f32 = jnp.float32
    return pl.pallas_call(
        functools.partial(_merge_kernel, tm=tm, n_tiles=n_tiles, tiles_per_seq=tiles_per_seq),
        grid=(n_tiles + 1,),
        in_specs=in_specs,
        out_specs=tok(proj_tile, D_MODEL),
        out_shape=jax.ShapeDtypeStruct((b, s, D_MODEL), f32),
        scratch_shapes=[
            pltpu.VMEM((HALO + tm, POOL_WIDTH), f32),
            pltpu.VMEM((HALO + tm, POOL_WIDTH), f32),
            pltpu.VMEM((HALO + tm, POOL_WIDTH - POOL_GROUP), f32),
            pltpu.VMEM((HALO + tm, POOL_WIDTH - 2 * POOL_GROUP), f32),
            pltpu.VMEM((N_HEADS, tm, HEAD_DIM), f32),
            pltpu.VMEM((N_HEADS, tm, HEAD_DIM), f32),
            pltpu.VMEM((1, tm, LANES), f32),
            pltpu.VMEM((1, tm, LANES), f32),
            pltpu.VMEM((tm, ATTN_WIDTH), jnp.bfloat16),
            pltpu.VMEM((tm, POOL_WIDTH), jnp.bfloat16),
            pltpu.VMEM((tm, ATTN_WIDTH), jnp.bfloat16),
            pltpu.VMEM((tm, POOL_WIDTH), jnp.bfloat16),
            pltpu.VMEM((tm, D_MODEL), jnp.bfloat16),
        ],
        compiler_params=pltpu.CompilerParams(
            dimension_semantics=("arbitrary",),
            vmem_limit_bytes=56 * MIB),
        name="merge_out_ln",
    )(os_[0], os_[1], os_[2], lses[0], lses[1], lses[2], gz3, gz3, u3, u3, gz3, gz3, x,
      w_pool_bf, pool_scale2, wpa_bf, wpp_bf, w_out_bf, gamma2, beta2)


def _layer(x, w_in, b_gate, w_pool, pool_scale, w_proj_attn, w_proj_pool, w_out, ln_gamma, ln_beta):
    b, s, d = x.shape
    bf = jnp.bfloat16
    u2, x2 = _project_u(x.reshape(b * s, d), w_in)
    qkvs = []
    for g, (window, dilation) in enumerate(DILATED_GROUPS):
        assert window // dilation == ATTN_BLOCK
        qkvs.append(_project_qkv(x2, w_in, b, g, dilation))
    gz2, os_, lses = _gates_z_and_attention(x2, w_in, b_gate.reshape(1, -1), qkvs)
    gz3 = gz2.reshape(b, s, GZ_TILES * COL_TILE)
    u3 = u2.reshape(b, s, POOL_WIDTH)
    return _merge(os_, lses, gz3, u3, x, w_pool.astype(bf), pool_scale.reshape(1, -1),
                  w_proj_attn.astype(bf), w_proj_pool.astype(bf), w_out.astype(bf),
                  ln_gamma.reshape(1, -1), ln_beta.reshape(1, -1))


def kernel(x, w_in, b_gate, w_pool, pool_scale, w_proj_attn, w_proj_pool, w_out, ln_gamma, ln_beta):
    depth = w_in.shape[0]
    for layer in range(depth):
        x = _layer(x, w_in[layer], b_gate[layer], w_pool[layer], pool_scale[layer],
                   w_proj_attn[layer], w_proj_pool[layer], w_out[layer],
                   ln_gamma[layer], ln_beta[layer])
    return x
```

```python
import functools

import jax
import jax.numpy as jnp
from jax import lax
from jax.experimental import pallas as pl
from jax.experimental.pallas import tpu as pltpu

D_MODEL = 2048
HEAD_DIM = 128
N_HEADS = 8
ATTN_WIDTH = N_HEADS * HEAD_DIM
DILATED_GROUPS = ((128, 1), (512, 4), (2048, 16))
N_GROUPS = len(DILATED_GROUPS)
POOL_WINDOWS = (2, 4, 8, 16)
POOL_WIDTH = D_MODEL // 2
POOL_GROUP = POOL_WIDTH // len(POOL_WINDOWS)
DEEPNORM_ALPHA = 2.0 ** 0.25
LN_EPS = 1e-5
NEG_INF = -1e30
LOG2_E = 1.4426950408889634

LANES = 128
MIB = 1024 * 1024
SUBLANE_STRIDE = 4
PROJ_VMEM_LIMIT = 54 * MIB

COL_TILE = 1024
W_TILE_ZATTN = 9
W_TILE_U = 10
W_TILE_ZPOOL = 11
W_TILE_GATES = 12
GZ_TILES = 6
GZ_TILE_ZATTN = 4
GZ_TILE_ZPOOL = 5
GZ_ROW_TILES = 16
ATTN_STEPS = 32

ATTN_BLOCK = 128
MERGE_CHUNK = 256
HALO = 32


def _cast_weight_tile(first_step, w_ref, wbf_ref):
    @pl.when(first_step)
    def _():
        wbf_ref[...] = w_ref[...].astype(wbf_ref.dtype)


def _proj_qkv_kernel(x_ref, w_ref, o_ref, wbf_ref, *scratch, dilation):
    _cast_weight_tile(jnp.logical_and(pl.program_id(1) == 0, pl.program_id(2) == 0), w_ref, wbf_ref)
    acc = jnp.dot(x_ref[...], wbf_ref[...], preferred_element_type=jnp.float32)
    if dilation == 1:
        o_ref[0] = acc.astype(o_ref.dtype)
        return
    acc_ref = scratch[0]
    slabs, tm = acc_ref.shape[0], acc_ref.shape[1]
    for c in range(slabs):
        acc_ref[c] = acc[:, c * LANES:(c + 1) * LANES]
    if dilation == SUBLANE_STRIDE:
        for r in range(dilation):
            for c in range(slabs):
                o_ref[r, :, c * LANES:(c + 1) * LANES] = (
                    acc_ref[c, pl.ds(r, tm // dilation, stride=dilation), :].astype(o_ref.dtype))
        return
    assert dilation == SUBLANE_STRIDE * SUBLANE_STRIDE
    mid_ref = scratch[1]
    quarter = tm // SUBLANE_STRIDE
    for r0 in range(SUBLANE_STRIDE):
        for c in range(slabs):
            mid_ref[c, pl.ds(r0 * quarter, quarter), :] = acc_ref[c, pl.ds(r0, quarter, stride=SUBLANE_STRIDE), :]
    for r0 in range(SUBLANE_STRIDE):
        for r1 in range(SUBLANE_STRIDE):
            for c in range(slabs):
                o_ref[SUBLANE_STRIDE * r1 + r0, :, c * LANES:(c + 1) * LANES] = (
                    mid_ref[c, pl.ds(r0 * quarter + r1, tm // dilation, stride=SUBLANE_STRIDE), :]
                    .astype(o_ref.dtype))


def _proj_u_kernel(x_ref, w_ref, u_ref, xbf_ref, wbf_ref):
    _cast_weight_tile(pl.program_id(0) == 0, w_ref, wbf_ref)
    xb = x_ref[...].astype(xbf_ref.dtype)
    xbf_ref[...] = xb
    u_ref[...] = jnp.dot(xb, wbf_ref[...], preferred_element_type=jnp.float32)


def _project_u(x2, w_in, tm=512):
    m = x2.shape[0]
    return pl.pallas_call(
        _proj_u_kernel,
        grid=(m // tm,),
        in_specs=[
            pl.BlockSpec((tm, D_MODEL), lambda i: (i, 0)),
            pl.BlockSpec((D_MODEL, COL_TILE), lambda i: (0, W_TILE_U), pipeline_mode=pl.Buffered(1)),
        ],
        out_specs=[
            pl.BlockSpec((tm, COL_TILE), lambda i: (i, 0)),
            pl.BlockSpec((tm, D_MODEL), lambda i: (i, 0)),
        ],
        out_shape=[
            jax.ShapeDtypeStruct((m, POOL_WIDTH), jnp.float32),
            jax.ShapeDtypeStruct((m, D_MODEL), jnp.bfloat16),
        ],
        scratch_shapes=[pltpu.VMEM((D_MODEL, COL_TILE), jnp.bfloat16)],
        compiler_params=pltpu.CompilerParams(
            dimension_semantics=("arbitrary",),
            vmem_limit_bytes=PROJ_VMEM_LIMIT),
        name="proj_u",
    )(x2, w_in)


def _project_qkv(x2, w_in, batch, group, dilation, tm=1024):
    m = x2.shape[0]
    tiles_per_batch = m // batch // tm
    sub = tm // dilation
    n_row_scratch = {1: 0, SUBLANE_STRIDE: 1, SUBLANE_STRIDE * SUBLANE_STRIDE: 2}[dilation]
    return pl.pallas_call(
        functools.partial(_proj_qkv_kernel, dilation=dilation),
        grid=(3, batch, tiles_per_batch),
        in_specs=[
            pl.BlockSpec((tm, D_MODEL), lambda j, b, i: (b * tiles_per_batch + i, 0)),
            pl.BlockSpec((D_MODEL, COL_TILE), lambda j, b, i: (0, N_GROUPS * j + group)),
        ],
        out_specs=pl.BlockSpec((None, None, dilation, sub, COL_TILE), lambda j, b, i: (j, b, 0, i, 0)),
        out_shape=jax.ShapeDtypeStruct((3, batch, dilation, m // batch // dilation, ATTN_WIDTH), jnp.bfloat16),
        scratch_shapes=[pltpu.VMEM((D_MODEL, COL_TILE), jnp.bfloat16)]
        + [pltpu.VMEM((COL_TILE // LANES, tm, LANES), jnp.float32)] * n_row_scratch,
        compiler_params=pltpu.CompilerParams(
            dimension_semantics=("arbitrary", "arbitrary", "arbitrary"),
            vmem_limit_bytes=PROJ_VMEM_LIMIT),
        name=f"proj_qkv_g{group}",
    )(x2, w_in)


def _attention_block(blk, first_chunk, q_ref, k_ref, v_ref, o_ref, lse_ref, kc_ref, vc_ref):
    bq = ATTN_BLOCK
    scale = HEAD_DIM ** -0.5
    row = lax.broadcasted_iota(jnp.int32, (bq, 2 * bq), 0)
    col = lax.broadcasted_iota(jnp.int32, (bq, 2 * bq), 1)
    dist = bq + row - col
    mask = jnp.logical_and(dist >= 0, dist <= bq)
    r0 = blk * bq
    if blk == 0:
        mask = jnp.logical_and(mask, col >= jnp.where(first_chunk, bq, 0))

        def kcat_of(hs):
            return jnp.concatenate([kc_ref[:, hs], k_ref[0:bq, hs]], axis=0)

        def vcat_of(hs):
            return jnp.concatenate([vc_ref[:, hs], v_ref[0:bq, hs]], axis=0)
    else:
        def kcat_of(hs):
            return k_ref[r0 - bq:r0 + bq, hs]

        def vcat_of(hs):
            return v_ref[r0 - bq:r0 + bq, hs]

    heads = [slice(h * HEAD_DIM, (h + 1) * HEAD_DIM) for h in range(N_HEADS)]
    ss = [jnp.where(mask,
                    lax.dot_general(q_ref[r0:r0 + bq, hs], kcat_of(hs), (((1,), (1,)), ((), ())),
                                    preferred_element_type=jnp.float32),
                    NEG_INF) for hs in heads]
    ms = [jnp.max(s, axis=-1, keepdims=True) for s in ss]

    def finish():
        es = [jnp.exp2((s - m) * (scale * LOG2_E)) for s, m in zip(ss, ms)]
        dens = [jnp.sum(e, axis=-1, keepdims=True) for e in es]
        outs = [jnp.dot(e.astype(o_ref.dtype), vcat_of(hs), preferred_element_type=jnp.float32)
                for e, hs in zip(es, heads)]
        lane = lax.broadcasted_iota(jnp.int32, (bq, LANES), 1)
        lse_tile = jnp.zeros((bq, LANES), jnp.float32)
        for h, hs in enumerate(heads):
            o_ref[r0:r0 + bq, hs] = (outs[h] / dens[h]).astype(o_ref.dtype)
            lse_tile = jnp.where(lane == h, ms[h] * scale + jnp.log(dens[h]), lse_tile)
        lse_ref[r0:r0 + bq, :] = lse_tile
    return finish


def _gz_attn_kernel(x_ref, w_ref, b_ref, *refs, dilations):
    n_groups = len(dilations)
    qkv_refs = refs[:3 * n_groups]
    gz_ref = refs[3 * n_groups]
    attn_out_refs = refs[3 * n_groups + 1:5 * n_groups + 1]
    wbf_ref, kc_ref, vc_ref = refs[5 * n_groups + 1:]
    s = pl.program_id(0)
    _cast_weight_tile(s % GZ_ROW_TILES == 0, w_ref, wbf_ref)
    chunk_cols = COL_TILE // 4

    for g, dilation in enumerate(dilations):
        q_ref, k_ref, v_ref = qkv_refs[3 * g:3 * g + 3]
        o_ref, lse_ref = attn_out_refs[2 * g:2 * g + 2]
        chunks_per_subseq = GZ_ROW_TILES // dilation
        is_gate = 2 * g + 1 < GZ_TILE_ZATTN

        def gz_chunk(c, is_gate=is_gate):
            cols = slice(c * chunk_cols, (c + 1) * chunk_cols)
            acc = jnp.dot(x_ref[...], wbf_ref[:, cols], preferred_element_type=jnp.float32)
            t = acc + b_ref[:, cols] if is_gate else acc
            sg = 0.5 * jnp.tanh(0.5 * t) + 0.5
            gz_ref[:, cols] = (sg if is_gate else sg * t).astype(gz_ref.dtype)

        @pl.when(s // ATTN_STEPS == g)
        def _(q_ref=q_ref, k_ref=k_ref, v_ref=v_ref, o_ref=o_ref, lse_ref=lse_ref,
              chunks_per_subseq=chunks_per_subseq, gz_chunk=gz_chunk):
            first_chunk = s % chunks_per_subseq == 0

            @pl.when(first_chunk)
            def _():
                kc_ref[...] = jnp.zeros_like(kc_ref)
                vc_ref[...] = jnp.zeros_like(vc_ref)

            gz_chunk(0)
            finish0 = _attention_block(0, first_chunk, q_ref, k_ref, v_ref, o_ref, lse_ref, kc_ref, vc_ref)
            gz_chunk(1)
            finish0()
            gz_chunk(2)
            finish1 = _attention_block(1, first_chunk, q_ref, k_ref, v_ref, o_ref, lse_ref, kc_ref, vc_ref)
            gz_chunk(3)
            finish1()
            kc_ref[...] = k_ref[ATTN_BLOCK:2 * ATTN_BLOCK, :]
            vc_ref[...] = v_ref[ATTN_BLOCK:2 * ATTN_BLOCK, :]


def _gates_z_and_attention(x2, w_in, b_gate2, qkvs):
    m = x2.shape[0]
    tm = m // GZ_ROW_TILES
    rows = 2 * ATTN_BLOCK
    n_groups = len(qkvs)
    assert GZ_TILES * GZ_ROW_TILES == n_groups * ATTN_STEPS
    dilations = tuple(q.shape[2] for q in qkvs)

    def w_tile(j):
        return jnp.where(j < GZ_TILE_ZATTN, W_TILE_GATES + j,
                         jnp.where(j == GZ_TILE_ZATTN, W_TILE_ZATTN, W_TILE_ZPOOL))

    def chunk_of(g, qkv):
        _, batch, dilation, sub_len, _ = qkv.shape
        chunks_per_subseq = sub_len // rows
        per_batch = dilation * chunks_per_subseq
        assert batch * per_batch == ATTN_STEPS and GZ_ROW_TILES % dilation == 0

        def index(step):
            a = jnp.clip(step - g * ATTN_STEPS, 0, ATTN_STEPS - 1)
            rc = a % per_batch
            return a // per_batch, rc // chunks_per_subseq, rc % chunks_per_subseq
        return index

    in_specs = [
        pl.BlockSpec((tm, D_MODEL), lambda st: (st % GZ_ROW_TILES, 0)),
        pl.BlockSpec((D_MODEL, COL_TILE), lambda st: (0, w_tile(st // GZ_ROW_TILES))),
        pl.BlockSpec((1, COL_TILE), lambda st: (0, jnp.minimum(st // GZ_ROW_TILES, GZ_TILE_ZATTN - 1))),
    ]
    out_specs = [pl.BlockSpec((tm, COL_TILE), lambda st: (st % GZ_ROW_TILES, st // GZ_ROW_TILES))]
    out_shape = [jax.ShapeDtypeStruct((m, GZ_TILES * COL_TILE), jnp.bfloat16)]
    operands = [x2, w_in, b_gate2]
    for g, qkv in enumerate(qkvs):
        _, batch, dilation, sub_len, _ = qkv.shape
        index = chunk_of(g, qkv)
        for t in range(3):
            in_specs.append(pl.BlockSpec((None, None, None, rows, ATTN_WIDTH),
                                         lambda st, t=t, index=index: (t, *index(st), 0)))
            operands.append(qkv)
        out_specs.append(pl.BlockSpec((None, None, rows, ATTN_WIDTH), lambda st, index=index: (*index(st), 0)))
        out_specs.append(pl.BlockSpec((None, None, rows, LANES), lambda st, index=index: (*index(st), 0)))
        out_shape.append(jax.ShapeDtypeStruct((batch, dilation, sub_len, ATTN_WIDTH), jnp.bfloat16))
        out_shape.append(jax.ShapeDtypeStruct((batch, dilation, sub_len, LANES), jnp.float32))
    results = pl.pallas_call(
        functools.partial(_gz_attn_kernel, dilations=dilations),
        grid=(GZ_TILES * GZ_ROW_TILES,),
        in_specs=in_specs,
        out_specs=out_specs,
        out_shape=out_shape,
        scratch_shapes=[
            pltpu.VMEM((D_MODEL, COL_TILE), jnp.bfloat16),
            pltpu.VMEM((ATTN_BLOCK, ATTN_WIDTH), jnp.bfloat16),
            pltpu.VMEM((ATTN_BLOCK, ATTN_WIDTH), jnp.bfloat16),
        ],
        compiler_params=pltpu.CompilerParams(
            dimension_semantics=("arbitrary",),
            vmem_limit_bytes=PROJ_VMEM_LIMIT),
        name="gates_z_attention",
    )(*operands)
    gz = results[0]
    os_ = [results[1 + 2 * g] for g in range(n_groups)]
    lses = [results[2 + 2 * g] for g in range(n_groups)]
    return gz, os_, lses


def _to_token_order(src_ref, dst_ref, slab, src_cols):
    dilation, sub = src_ref.shape[0], src_ref.shape[1]
    for r in range(dilation):
        dst_ref[slab, pl.ds(r, sub, stride=dilation), :] = src_ref[r, :, src_cols].astype(dst_ref.dtype)


def _combine_weights(l1_ref, l2_ref, l3_ref, ln2_ref, ln3_ref):
    _to_token_order(l2_ref, ln2_ref, 0, slice(None))
    _to_token_order(l3_ref, ln3_ref, 0, slice(None))
    l1 = l1_ref[...]
    l2 = ln2_ref[0]
    l3 = ln3_ref[0]
    mx = jnp.maximum(jnp.maximum(l1, l2), l3)
    e1 = jnp.exp(l1 - mx)
    e2 = jnp.exp(l2 - mx)
    e3 = jnp.exp(l3 - mx)
    tot = e1 + e2 + e3
    return e1 / tot, e2 / tot, e3 / tot


def _combine_head(h, weights, o1_ref, o2_ref, o3_ref, za_ref, on2_ref, on3_ref, ya_ref):
    hs = slice(h * HEAD_DIM, (h + 1) * HEAD_DIM)
    _to_token_order(o2_ref, on2_ref, h, hs)
    _to_token_order(o3_ref, on3_ref, h, hs)
    w1, w2, w3 = weights
    o = (w1[:, h:h + 1] * o1_ref[:, hs].astype(jnp.float32)
         + w2[:, h:h + 1] * on2_ref[h]
         + w3[:, h:h + 1] * on3_ref[h])
    ya_ref[:, hs] = (o * za_ref[:, hs].astype(jnp.float32)).astype(ya_ref.dtype)


def _pool_window_sums(i, u_ref, halo_ref, ext_ref, s2_ref, s4_ref, s8_ref, *, tm):
    rows = HALO + tm
    ext_ref[0:HALO, :] = jnp.where(i > 0, halo_ref[...], 0.0)
    ext_ref[HALO:, :] = u_ref[...]
    g = POOL_GROUP
    s2_ref[8:rows, :] = ext_ref[8:rows, :] + ext_ref[7:rows - 1, :]
    s4_ref[16:rows, :] = s2_ref[16:rows, g:] + s2_ref[14:rows - 2, g:]
    s8_ref[24:rows, :] = s4_ref[24:rows, g:] + s4_ref[20:rows - 4, g:]


def _pool_group(k, i, zp_ref, w_pool_ref, pool_scale_ref, ext_ref, s2_ref, s4_ref, s8_ref, yp_ref, *, tm):
    rows = HALO + tm
    g = POOL_GROUP
    w = POOL_WINDOWS[k]
    if k < 3:
        window_sum = (s2_ref, s4_ref, s8_ref)[k][HALO:rows, 0:g]
    else:
        window_sum = s8_ref[HALO:rows, g:] + s8_ref[HALO - 8:rows - 8, g:]
    cs = slice(k * g, (k + 1) * g)
    pos = i * tm + lax.broadcasted_iota(jnp.int32, (tm, 1), 0)
    cnt = jnp.minimum(pos + 1, w).astype(jnp.float32)
    p = window_sum / cnt - ext_ref[HALO:rows, cs]
    y = jnp.dot(p.astype(jnp.bfloat16), w_pool_ref[k], preferred_element_type=jnp.float32)
    y = y * pool_scale_ref[:, cs] * zp_ref[:, cs].astype(jnp.float32)
    yp_ref[:, cs] = y.astype(yp_ref.dtype)


def _layer_norm_rows(r_ref, gamma_ref, beta_ref):
    r = r_ref[...]
    mu = jnp.mean(r, axis=-1, keepdims=True)
    rc = r - mu
    var = jnp.mean(rc * rc, axis=-1, keepdims=True)
    r_ref[...] = rc * lax.rsqrt(var + LN_EPS) * gamma_ref[...] + beta_ref[...]


def _merge_kernel(o1_ref, o2_ref, o3_ref, l1_ref, l2_ref, l3_ref, za_ref, zp_ref, u_ref, halo_ref,
                  g_attn_ref, g_pool_ref, x_ref, w_pool_ref, pool_scale_ref, wpa_ref, wpp_ref, w_out_ref,
                  gamma_ref, beta_ref, out_ref, ext_ref, s2_ref, s4_ref, s8_ref, on2_ref, on3_ref, ln2_ref, ln3_ref,
                  ya0_ref, yp0_ref, ya1_ref, yp1_ref, merged_ref, *, tm, n_tiles, tiles_per_seq):
    s = pl.program_id(0)
    i_prep = jnp.minimum(s, n_tiles - 1) % tiles_per_seq

    @pl.when(s == 0)
    def _():
        ya1_ref[...] = jnp.zeros_like(ya1_ref)
        yp1_ref[...] = jnp.zeros_like(yp1_ref)

    def step(ya_w, yp_w, ya_r, yp_r):
        chunks = [slice(c * MERGE_CHUNK, (c + 1) * MERGE_CHUNK) for c in range(D_MODEL // MERGE_CHUNK)]
        ya = ya_r[...]
        yp = yp_r[...]
        weights = _combine_weights(l1_ref, l2_ref, l3_ref, ln2_ref, ln3_ref)
        pas = []
        for c, cols in enumerate(chunks):
            pas.append(jnp.dot(ya, wpa_ref[:, cols], preferred_element_type=jnp.float32))
            _combine_head(c, weights, o1_ref, o2_ref, o3_ref, za_ref, on2_ref, on3_ref, ya_w)
        _pool_window_sums(i_prep, u_ref, halo_ref, ext_ref, s2_ref, s4_ref, s8_ref, tm=tm)
        for c, cols in enumerate(chunks):
            pp = jnp.dot(yp, wpp_ref[:, cols], preferred_element_type=jnp.float32)
            merged_ref[:, cols] = (g_attn_ref[:, cols].astype(jnp.float32) * pas[c]
                                   + g_pool_ref[:, cols].astype(jnp.float32) * pp).astype(merged_ref.dtype)
            if c % 2 == 1:
                _pool_group(c // 2, i_prep, zp_ref, w_pool_ref, pool_scale_ref, ext_ref, s2_ref, s4_ref, s8_ref,
                            yp_w, tm=tm)
        merged = merged_ref[...]
        for cols in chunks:
            out = jnp.dot(merged, w_out_ref[:, cols], preferred_element_type=jnp.float32)
            out_ref[:, cols] = DEEPNORM_ALPHA * x_ref[:, cols] + out
        _layer_norm_rows(out_ref, gamma_ref, beta_ref)

    @pl.when(s % 2 == 0)
    def _():
        step(ya0_ref, yp0_ref, ya1_ref, yp1_ref)

    @pl.when(s % 2 == 1)
    def _():
        step(ya1_ref, yp1_ref, ya0_ref, yp0_ref)


def _merge(os_, lses, gz3, u3, x, w_pool_bf, pool_scale2, wpa_bf, wpp_bf, w_out_bf, gamma2, beta2, tm=256):
    b, s, _ = x.shape
    tiles_per_seq = s // tm
    n_tiles = b * tiles_per_seq

    def prep_tile(step):
        t = jnp.minimum(step, n_tiles - 1)
        return t // tiles_per_seq, t % tiles_per_seq

    def proj_tile(step):
        t = jnp.maximum(step - 1, 0)
        return t // tiles_per_seq, t % tiles_per_seq

    def tok(which, width, tile=0):
        return pl.BlockSpec((None, tm, width), lambda st: (*which(st), tile))

    def grouped(arr):
        dilation, width = arr.shape[1], arr.shape[3]

        def index(st):
            bb, i = prep_tile(st)
            return bb, 0, i, 0
        if dilation == 1:
            return pl.BlockSpec((None, None, tm, width), index)
        return pl.BlockSpec((None, dilation, tm // dilation, width), index)

    def halo_index(st):
        bb, i = prep_tile(st)
        return bb, jnp.maximum(i * (tm // HALO) - 1, 0), 0

    def const(shape):
        nd = len(shape)
        return pl.BlockSpec(shape, lambda st: (0,) * nd, pipeline_mode=pl.Buffered(1))

    in_specs = [
        grouped(os_[0]), grouped(os_[1]), grouped(os_[2]),
        grouped(lses[0]), grouped(lses[1]), grouped(lses[2]),
        tok(prep_tile, COL_TILE, GZ_TILE_ZATTN), tok(prep_tile, COL_TILE, GZ_TILE_ZPOOL),
        tok(prep_tile, POOL_WIDTH),
        pl.BlockSpec((None, HALO, POOL_WIDTH), halo_index),
        tok(proj_tile, D_MODEL, 0), tok(proj_tile, D_MODEL, 1),
        tok(proj_tile, D_MODEL),
        const((len(POOL_WINDOWS), POOL_GROUP, POOL_GROUP)),
        const((1, POOL_WIDTH)),
        const((ATTN_WIDTH, D_MODEL)),
        const((POOL_WIDTH, D_MODEL)),
        const((D_MODEL, D_MODEL)),
        const((1, D_MODEL)),
        const((1, D_MODEL)),
    ]
    f32 = jnp.float32
    return pl.pallas_call(
        functools.partial(_merge_kernel, tm=tm, n_tiles=n_tiles, tiles_per_seq=tiles_per_seq),
        grid=(n_tiles + 1,),
        in_specs=in_specs,
        out_specs=tok(proj_tile, D_MODEL),
        out_shape=jax.ShapeDtypeStruct((b, s, D_MODEL), f32),
        scratch_shapes=[
            pltpu.VMEM((HALO + tm, POOL_WIDTH), f32),
            pltpu.VMEM((HALO + tm, POOL_WIDTH), f32),
            pltpu.VMEM((HALO + tm, POOL_WIDTH - POOL_GROUP), f32),
            pltpu.VMEM((HALO + tm, POOL_WIDTH - 2 * POOL_GROUP), f32),
            pltpu.VMEM((N_HEADS, tm, HEAD_DIM), f32),
            pltpu.VMEM((N_HEADS, tm, HEAD_DIM), f32),
            pltpu.VMEM((1, tm, LANES), f32),
            pltpu.VMEM((1, tm, LANES), f32),
            pltpu.VMEM((tm, ATTN_WIDTH), jnp.bfloat16),
            pltpu.VMEM((tm, POOL_WIDTH), jnp.bfloat16),
            pltpu.VMEM((tm, ATTN_WIDTH), jnp.bfloat16),
            pltpu.VMEM((tm, POOL_WIDTH), jnp.bfloat16),
            pltpu.VMEM((tm, D_MODEL), jnp.bfloat16),
        ],
        compiler_params=pltpu.CompilerParams(
            dimension_semantics=("arbitrary",),
            vmem_limit_bytes=56 * MIB),
        name="merge_out_ln",
    )(os_[0], os_[1], os_[2], lses[0], lses[1], lses[2], gz3, gz3, u3, u3, gz3, gz3, x,
      w_pool_bf, pool_scale2, wpa_bf, wpp_bf, w_out_bf, gamma2, beta2)


def _layer(x, w_in, b_gate, w_pool, pool_scale, w_proj_attn, w_proj_pool, w_out, ln_gamma, ln_beta):
    b, s, d = x.shape
    bf = jnp.bfloat16
    u2, x2 = _project_u(x.reshape(b * s, d), w_in)
    qkvs = []
    for g, (window, dilation) in enumerate(DILATED_GROUPS):
        assert window // dilation == ATTN_BLOCK
        qkvs.append(_project_qkv(x2, w_in, b, g, dilation))
    gz2, os_, lses = _gates_z_and_attention(x2, w_in, b_gate.reshape(1, -1), qkvs)
    gz3 = gz2.reshape(b, s, GZ_TILES * COL_TILE)
    u3 = u2.reshape(b, s, POOL_WIDTH)
    return _merge(os_, lses, gz3, u3, x, w_pool.astype(bf), pool_scale.reshape(1, -1),
                  w_proj_attn.astype(bf), w_proj_pool.astype(bf), w_out.astype(bf),
                  ln_gamma.reshape(1, -1), ln_beta.reshape(1, -1))


def kernel(x, w_in, b_gate, w_pool, pool_scale, w_proj_attn, w_proj_pool, w_out, ln_gamma, ln_beta):
    depth = w_in.shape[0]
    for layer in range(depth):
        x = _layer(x, w_in[layer], b_gate[layer], w_pool[layer], pool_scale[layer],
                   w_proj_attn[layer], w_proj_pool[layer], w_out[layer],
                   ln_gamma[layer], ln_beta[layer])
    return x
```

```python
import functools

import jax
import jax.numpy as jnp
from jax import lax
from jax.experimental import pallas as pl
from jax.experimental.pallas import tpu as pltpu

D_MODEL = 2048
HEAD_DIM = 128
N_HEADS = 8
ATTN_WIDTH = N_HEADS * HEAD_DIM
DILATED_GROUPS = ((128, 1), (512, 4), (2048, 16))
N_GROUPS = len(DILATED_GROUPS)
POOL_WINDOWS = (2, 4, 8, 16)
POOL_WIDTH = D_MODEL // 2
POOL_GROUP = POOL_WIDTH // len(POOL_WINDOWS)
DEEPNORM_ALPHA = 2.0 ** 0.25
LN_EPS = 1e-5
NEG_INF = -1e30
LOG2_E = 1.4426950408889634

LANES = 128
MIB = 1024 * 1024
SUBLANE_STRIDE = 4
VMEM_BYTES_V7X = 64 * MIB
PROJ_VMEM_LIMIT = VMEM_BYTES_V7X - 10 * MIB
MERGE_VMEM_LIMIT = VMEM_BYTES_V7X - 8 * MIB

COL_TILE = 1024
W_TILE_ZATTN = 9
W_TILE_U = 10
W_TILE_ZPOOL = 11
W_TILE_GATES = 12
GZ_TILES = 6
GZ_TILE_ZATTN = 4
GZ_TILE_ZPOOL = 5
GZ_ROW_TILES = 16
ATTN_STEPS = 32

ATTN_BLOCK = 128
MERGE_CHUNK = 256
HALO = 32


def _cast_weight_tile(first_step, w_ref, wbf_ref):
    @pl.when(first_step)
    def _():
        wbf_ref[...] = w_ref[...].astype(wbf_ref.dtype)


def _proj_qkv_kernel(x_ref, w_ref, *refs, dilation, n_side):
    side_in, o_ref, side_out = refs[:n_side], refs[n_side], refs[n_side + 1:2 * n_side + 1]
    wbf_ref, scratch = refs[2 * n_side + 1], refs[2 * n_side + 2:]
    _cast_weight_tile(jnp.logical_and(pl.program_id(1) == 0, pl.program_id(2) == 0), w_ref, wbf_ref)
    if n_side:
        @pl.when(pl.program_id(0) == 0)
        def _():
            for src, dst in zip(side_in, side_out):
                dst[...] = src[...].astype(dst.dtype)

    acc = jnp.dot(x_ref[...], wbf_ref[...], preferred_element_type=jnp.float32)
    if dilation == 1:
        o_ref[0] = acc.astype(o_ref.dtype)
        return
    acc_ref = scratch[0]
    slabs, tm = acc_ref.shape[0], acc_ref.shape[1]
    for c in range(slabs):
        acc_ref[c] = acc[:, c * LANES:(c + 1) * LANES]
    if dilation == SUBLANE_STRIDE:
        for r in range(dilation):
            for c in range(slabs):
                o_ref[r, :, c * LANES:(c + 1) * LANES] = (
                    acc_ref[c, pl.ds(r, tm // dilation, stride=dilation), :].astype(o_ref.dtype))
        return
    assert dilation == SUBLANE_STRIDE * SUBLANE_STRIDE
    mid_ref = scratch[1]
    quarter = tm // SUBLANE_STRIDE
    for r0 in range(SUBLANE_STRIDE):
        for c in range(slabs):
            mid_ref[c, pl.ds(r0 * quarter, quarter), :] = acc_ref[c, pl.ds(r0, quarter, stride=SUBLANE_STRIDE), :]
    for r0 in range(SUBLANE_STRIDE):
        for r1 in range(SUBLANE_STRIDE):
            for c in range(slabs):
                o_ref[SUBLANE_STRIDE * r1 + r0, :, c * LANES:(c + 1) * LANES] = (
                    mid_ref[c, pl.ds(r0 * quarter + r1, tm // dilation, stride=SUBLANE_STRIDE), :]
                    .astype(o_ref.dtype))


def _proj_u_kernel(x_ref, w_ref, u_ref, xbf_ref, wbf_ref):
    _cast_weight_tile(pl.program_id(0) == 0, w_ref, wbf_ref)
    xb = x_ref[...].astype(xbf_ref.dtype)
    xbf_ref[...] = xb
    u_ref[...] = jnp.dot(xb, wbf_ref[...], preferred_element_type=jnp.float32)


def _project_u(x2, w_in, tm=512):
    m = x2.shape[0]
    return pl.pallas_call(
        _proj_u_kernel,
        grid=(m // tm,),
        in_specs=[
            pl.BlockSpec((tm, D_MODEL), lambda i: (i, 0)),
            pl.BlockSpec((D_MODEL, COL_TILE), lambda i: (0, W_TILE_U), pipeline_mode=pl.Buffered(1)),
        ],
        out_specs=[
            pl.BlockSpec((tm, COL_TILE), lambda i: (i, 0)),
            pl.BlockSpec((tm, D_MODEL), lambda i: (i, 0)),
        ],
        out_shape=[
            jax.ShapeDtypeStruct((m, POOL_WIDTH), jnp.float32),
            jax.ShapeDtypeStruct((m, D_MODEL), jnp.bfloat16),
        ],
        scratch_shapes=[pltpu.VMEM((D_MODEL, COL_TILE), jnp.bfloat16)],
        compiler_params=pltpu.CompilerParams(
            dimension_semantics=("arbitrary",),
            vmem_limit_bytes=PROJ_VMEM_LIMIT),
        name="proj_u",
    )(x2, w_in)


def _project_qkv(x2, w_in, batch, group, dilation, side_weights=(), tm=1024):
    m = x2.shape[0]
    tiles_per_batch = m // batch // tm
    row_tiles = batch * tiles_per_batch
    sub = tm // dilation
    n_row_scratch = {1: 0, SUBLANE_STRIDE: 1, SUBLANE_STRIDE * SUBLANE_STRIDE: 2}[dilation]

    def side_index(j, b, i):
        return jnp.where(j == 0, b * tiles_per_batch + i, row_tiles - 1), 0

    side_specs = [pl.BlockSpec((w.shape[0] // row_tiles, w.shape[1]), side_index) for w in side_weights]
    results = pl.pallas_call(
        functools.partial(_proj_qkv_kernel, dilation=dilation, n_side=len(side_weights)),
        grid=(3, batch, tiles_per_batch),
        in_specs=[
            pl.BlockSpec((tm, D_MODEL), lambda j, b, i: (b * tiles_per_batch + i, 0)),
            pl.BlockSpec((D_MODEL, COL_TILE), lambda j, b, i: (0, N_GROUPS * j + group)),
        ] + side_specs,
        out_specs=[pl.BlockSpec((None, None, dilation, sub, COL_TILE), lambda j, b, i: (j, b, 0, i, 0))]
        + side_specs,
        out_shape=[jax.ShapeDtypeStruct((3, batch, dilation, m // batch // dilation, ATTN_WIDTH), jnp.bfloat16)]
        + [jax.ShapeDtypeStruct(w.shape, jnp.bfloat16) for w in side_weights],
        scratch_shapes=[pltpu.VMEM((D_MODEL, COL_TILE), jnp.bfloat16)]
        + [pltpu.VMEM((COL_TILE // LANES, tm, LANES), jnp.float32)] * n_row_scratch,
        compiler_params=pltpu.CompilerParams(
            dimension_semantics=("arbitrary", "arbitrary", "arbitrary"),
            vmem_limit_bytes=PROJ_VMEM_LIMIT),
        name=f"proj_qkv_g{group}",
    )(x2, w_in, *side_weights)
    return results[0], results[1:]


def _attention_block(blk, first_chunk, q_ref, k_ref, v_ref, o_ref, lse_ref, kc_ref, vc_ref):
    bq = ATTN_BLOCK
    scale = HEAD_DIM ** -0.5
    row = lax.broadcasted_iota(jnp.int32, (bq, 2 * bq), 0)
    col = lax.broadcasted_iota(jnp.int32, (bq, 2 * bq), 1)
    dist = bq + row - col
    mask = jnp.logical_and(dist >= 0, dist <= bq)
    r0 = blk * bq
    if blk == 0:
        mask = jnp.logical_and(mask, col >= jnp.where(first_chunk, bq, 0))

        def kcat_of(hs):
            return jnp.concatenate([kc_ref[:, hs], k_ref[0:bq, hs]], axis=0)

        def vcat_of(hs):
            return jnp.concatenate([vc_ref[:, hs], v_ref[0:bq, hs]], axis=0)
    else:
        def kcat_of(hs):
            return k_ref[r0 - bq:r0 + bq, hs]

        def vcat_of(hs):
            return v_ref[r0 - bq:r0 + bq, hs]

    heads = [slice(h * HEAD_DIM, (h + 1) * HEAD_DIM) for h in range(N_HEADS)]
    ss = [jnp.where(mask,
                    lax.dot_general(q_ref[r0:r0 + bq, hs], kcat_of(hs), (((1,), (1,)), ((), ())),
                                    preferred_element_type=jnp.float32),
                    NEG_INF) for hs in heads]
    ms = [jnp.max(s, axis=-1, keepdims=True) for s in ss]

    def finish():
        es = [jnp.exp2((s - m) * (scale * LOG2_E)) for s, m in zip(ss, ms)]
        dens = [jnp.sum(e, axis=-1, keepdims=True) for e in es]
        outs = [jnp.dot(e.astype(o_ref.dtype), vcat_of(hs), preferred_element_type=jnp.float32)
                for e, hs in zip(es, heads)]
        lane = lax.broadcasted_iota(jnp.int32, (bq, LANES), 1)
        lse_tile = jnp.zeros((bq, LANES), jnp.float32)
        for h, hs in enumerate(heads):
            o_ref[r0:r0 + bq, hs] = (outs[h] / dens[h]).astype(o_ref.dtype)
            lse_tile = jnp.where(lane == h, ms[h] * scale + jnp.log(dens[h]), lse_tile)
        lse_ref[r0:r0 + bq, :] = lse_tile
    return finish


def _gz_attn_kernel(x_ref, w_ref, b_ref, *refs, dilations):
    n_groups = len(dilations)
    qkv_refs = refs[:3 * n_groups]
    gz_ref = refs[3 * n_groups]
    attn_out_refs = refs[3 * n_groups + 1:5 * n_groups + 1]
    wbf_ref, kc_ref, vc_ref = refs[5 * n_groups + 1:]
    s = pl.program_id(0)
    _cast_weight_tile(s % GZ_ROW_TILES == 0, w_ref, wbf_ref)
    chunk_cols = COL_TILE // 4

    for g, dilation in enumerate(dilations):
        q_ref, k_ref, v_ref = qkv_refs[3 * g:3 * g + 3]
        o_ref, lse_ref = attn_out_refs[2 * g:2 * g + 2]
        chunks_per_subseq = GZ_ROW_TILES // dilation
        is_gate = 2 * g + 1 < GZ_TILE_ZATTN

        def gz_chunk(c, is_gate=is_gate):
            cols = slice(c * chunk_cols, (c + 1) * chunk_cols)
            acc = jnp.dot(x_ref[...], wbf_ref[:, cols], preferred_element_type=jnp.float32)
            t = acc + b_ref[:, cols] if is_gate else acc
            sg = 0.5 * jnp.tanh(0.5 * t) + 0.5
            gz_ref[:, cols] = (sg if is_gate else sg * t).astype(gz_ref.dtype)

        @pl.when(s // ATTN_STEPS == g)
        def _(q_ref=q_ref, k_ref=k_ref, v_ref=v_ref, o_ref=o_ref, lse_ref=lse_ref,
              chunks_per_subseq=chunks_per_subseq, gz_chunk=gz_chunk):
            first_chunk = s % chunks_per_subseq == 0

            @pl.when(first_chunk)
            def _():
                kc_ref[...] = jnp.zeros_like(kc_ref)
                vc_ref[...] = jnp.zeros_like(vc_ref)

            gz_chunk(0)
            finish0 = _attention_block(0, first_chunk, q_ref, k_ref, v_ref, o_ref, lse_ref, kc_ref, vc_ref)
            gz_chunk(1)
            finish0()
            gz_chunk(2)
            finish1 = _attention_block(1, first_chunk, q_ref, k_ref, v_ref, o_ref, lse_ref, kc_ref, vc_ref)
            gz_chunk(3)
            finish1()
            kc_ref[...] = k_ref[ATTN_BLOCK:2 * ATTN_BLOCK, :]
            vc_ref[...] = v_ref[ATTN_BLOCK:2 * ATTN_BLOCK, :]


def _gates_z_and_attention(x2, w_in, b_gate2, qkvs):
    m = x2.shape[0]
    tm = m // GZ_ROW_TILES
    rows = 2 * ATTN_BLOCK
    n_groups = len(qkvs)
    assert GZ_TILES * GZ_ROW_TILES == n_groups * ATTN_STEPS
    dilations = tuple(q.shape[2] for q in qkvs)

    def w_tile(j):
        return jnp.where(j < GZ_TILE_ZATTN, W_TILE_GATES + j,
                         jnp.where(j == GZ_TILE_ZATTN, W_TILE_ZATTN, W_TILE_ZPOOL))

    def chunk_of(g, qkv):
        _, batch, dilation, sub_len, _ = qkv.shape
        chunks_per_subseq = sub_len // rows
        per_batch = dilation * chunks_per_subseq
        assert batch * per_batch == ATTN_STEPS and GZ_ROW_TILES % dilation == 0

        def index(step):
            a = jnp.clip(step - g * ATTN_STEPS, 0, ATTN_STEPS - 1)
            rc = a % per_batch
            return a // per_batch, rc // chunks_per_subseq, rc % chunks_per_subseq
        return index

    in_specs = [
        pl.BlockSpec((tm, D_MODEL), lambda st: (st % GZ_ROW_TILES, 0)),
        pl.BlockSpec((D_MODEL, COL_TILE), lambda st: (0, w_tile(st // GZ_ROW_TILES))),
        pl.BlockSpec((1, COL_TILE), lambda st: (0, jnp.minimum(st // GZ_ROW_TILES, GZ_TILE_ZATTN - 1))),
    ]
    out_specs = [pl.BlockSpec((tm, COL_TILE), lambda st: (st % GZ_ROW_TILES, st // GZ_ROW_TILES))]
    out_shape = [jax.ShapeDtypeStruct((m, GZ_TILES * COL_TILE), jnp.bfloat16)]
    operands = [x2, w_in, b_gate2]
    for g, qkv in enumerate(qkvs):
        _, batch, dilation, sub_len, _ = qkv.shape
        index = chunk_of(g, qkv)
        for t in range(3):
            in_specs.append(pl.BlockSpec((None, None, None, rows, ATTN_WIDTH),
                                         lambda st, t=t, index=index: (t, *index(st), 0)))
            operands.append(qkv)
        out_specs.append(pl.BlockSpec((None, None, rows, ATTN_WIDTH), lambda st, index=index: (*index(st), 0)))
        out_specs.append(pl.BlockSpec((None, None, rows, LANES), lambda st, index=index: (*index(st), 0)))
        out_shape.append(jax.ShapeDtypeStruct((batch, dilation, sub_len, ATTN_WIDTH), jnp.bfloat16))
        out_shape.append(jax.ShapeDtypeStruct((batch, dilation, sub_len, LANES), jnp.float32))
    results = pl.pallas_call(
        functools.partial(_gz_attn_kernel, dilations=dilations),
        grid=(GZ_TILES * GZ_ROW_TILES,),
        in_specs=in_specs,
        out_specs=out_specs,
        out_shape=out_shape,
        scratch_shapes=[
            pltpu.VMEM((D_MODEL, COL_TILE), jnp.bfloat16),
            pltpu.VMEM((ATTN_BLOCK, ATTN_WIDTH), jnp.bfloat16),
            pltpu.VMEM((ATTN_BLOCK, ATTN_WIDTH), jnp.bfloat16),
        ],
        compiler_params=pltpu.CompilerParams(
            dimension_semantics=("arbitrary",),
            vmem_limit_bytes=PROJ_VMEM_LIMIT),
        name="gates_z_attention",
    )(*operands)
    gz = results[0]
    os_ = [results[1 + 2 * g] for g in range(n_groups)]
    lses = [results[2 + 2 * g] for g in range(n_groups)]
    return gz, os_, lses


def _to_token_order(src_ref, dst_ref, slab, src_cols):
    dilation, sub = src_ref.shape[0], src_ref.shape[1]
    for r in range(dilation):
        dst_ref[slab, pl.ds(r, sub, stride=dilation), :] = src_ref[r, :, src_cols].astype(dst_ref.dtype)


def _combine_weights(l1_ref, l2_ref, l3_ref, ln2_ref, ln3_ref):
    _to_token_order(l2_ref, ln2_ref, 0, slice(None))
    _to_token_order(l3_ref, ln3_ref, 0, slice(None))
    l1 = l1_ref[...]
    l2 = ln2_ref[0]
    l3 = ln3_ref[0]
    mx = jnp.maximum(jnp.maximum(l1, l2), l3)
    e1 = jnp.exp(l1 - mx)
    e2 = jnp.exp(l2 - mx)
    e3 = jnp.exp(l3 - mx)
    tot = e1 + e2 + e3
    return e1 / tot, e2 / tot, e3 / tot


def _combine_head(h, weights, o1_ref, o2_ref, o3_ref, za_ref, on2_ref, on3_ref, ya_ref):
    hs = slice(h * HEAD_DIM, (h + 1) * HEAD_DIM)
    _to_token_order(o2_ref, on2_ref, h, hs)
    _to_token_order(o3_ref, on3_ref, h, hs)
    w1, w2, w3 = weights
    o = (w1[:, h:h + 1] * o1_ref[:, hs].astype(jnp.float32)
         + w2[:, h:h + 1] * on2_ref[h]
         + w3[:, h:h + 1] * on3_ref[h])
    ya_ref[:, hs] = (o * za_ref[:, hs].astype(jnp.float32)).astype(ya_ref.dtype)


def _pool_window_sums(i, u_ref, halo_ref, ext_ref, s2_ref, s4_ref, s8_ref, *, tm):
    rows = HALO + tm
    ext_ref[0:HALO, :] = jnp.where(i > 0, halo_ref[...], 0.0)
    ext_ref[HALO:, :] = u_ref[...]
    g = POOL_GROUP
    s2_ref[8:rows, :] = ext_ref[8:rows, :] + ext_ref[7:rows - 1, :]
    s4_ref[16:rows, :] = s2_ref[16:rows, g:] + s2_ref[14:rows - 2, g:]
    s8_ref[24:rows, :] = s4_ref[24:rows, g:] + s4_ref[20:rows - 4, g:]


def _pool_group(k, i, zp_ref, w_pool_ref, pool_scale_ref, ext_ref, s2_ref, s4_ref, s8_ref, yp_ref, *, tm):
    rows = HALO + tm
    g = POOL_GROUP
    w = POOL_WINDOWS[k]
    if k < 3:
        window_sum = (s2_ref, s4_ref, s8_ref)[k][HALO:rows, 0:g]
    else:
        window_sum = s8_ref[HALO:rows, g:] + s8_ref[HALO - 8:rows - 8, g:]
    cs = slice(k * g, (k + 1) * g)
    pos = i * tm + lax.broadcasted_iota(jnp.int32, (tm, 1), 0)
    cnt = jnp.minimum(pos + 1, w).astype(jnp.float32)
    p = window_sum / cnt - ext_ref[HALO:rows, cs]
    y = jnp.dot(p.astype(jnp.bfloat16), w_pool_ref[k], preferred_element_type=jnp.float32)
    y = y * pool_scale_ref[:, cs] * zp_ref[:, cs].astype(jnp.float32)
    yp_ref[:, cs] = y.astype(yp_ref.dtype)


def _layer_norm_rows(r_ref, gamma_ref, beta_ref):
    r = r_ref[...]
    mu = jnp.mean(r, axis=-1, keepdims=True)
    rc = r - mu
    var = jnp.mean(rc * rc, axis=-1, keepdims=True)
    r_ref[...] = rc * lax.rsqrt(var + LN_EPS) * gamma_ref[...] + beta_ref[...]


def _merge_kernel(o1_ref, o2_ref, o3_ref, l1_ref, l2_ref, l3_ref, za_ref, zp_ref, u_ref, halo_ref,
                  g_attn_ref, g_pool_ref, x_ref, w_pool_ref, pool_scale_ref, wpa_ref, wpp_ref, w_out_ref,
                  gamma_ref, beta_ref, out_ref, ext_ref, s2_ref, s4_ref, s8_ref, on2_ref, on3_ref, ln2_ref, ln3_ref,
                  ya0_ref, yp0_ref, ya1_ref, yp1_ref, merged_ref, *, tm, n_tiles, tiles_per_seq):
    s = pl.program_id(0)
    i_prep = jnp.minimum(s, n_tiles - 1) % tiles_per_seq

    @pl.when(s == 0)
    def _():
        ya1_ref[...] = jnp.zeros_like(ya1_ref)
        yp1_ref[...] = jnp.zeros_like(yp1_ref)

    def step(ya_w, yp_w, ya_r, yp_r):
        chunks = [slice(c * MERGE_CHUNK, (c + 1) * MERGE_CHUNK) for c in range(D_MODEL // MERGE_CHUNK)]
        ya = ya_r[...]
        yp = yp_r[...]
        weights = _combine_weights(l1_ref, l2_ref, l3_ref, ln2_ref, ln3_ref)
        pas = []
        for c, cols in enumerate(chunks):
            pas.append(jnp.dot(ya, wpa_ref[:, cols], preferred_element_type=jnp.float32))
            _combine_head(c, weights, o1_ref, o2_ref, o3_ref, za_ref, on2_ref, on3_ref, ya_w)
        _pool_window_sums(i_prep, u_ref, halo_ref, ext_ref, s2_ref, s4_ref, s8_ref, tm=tm)
        for c, cols in enumerate(chunks):
            pp = jnp.dot(yp, wpp_ref[:, cols], preferred_element_type=jnp.float32)
            merged_ref[:, cols] = (g_attn_ref[:, cols].astype(jnp.float32) * pas[c]
                                   + g_pool_ref[:, cols].astype(jnp.float32) * pp).astype(merged_ref.dtype)
            if c % 2 == 1:
                _pool_group(c // 2, i_prep, zp_ref, w_pool_ref, pool_scale_ref, ext_ref, s2_ref, s4_ref, s8_ref,
                            yp_w, tm=tm)
        merged = merged_ref[...]
        for cols in chunks:
            out = jnp.dot(merged, w_out_ref[:, cols], preferred_element_type=jnp.float32)
            out_ref[:, cols] = DEEPNORM_ALPHA * x_ref[:, cols] + out
        _layer_norm_rows(out_ref, gamma_ref, beta_ref)

    @pl.when(s % 2 == 0)
    def _():
        step(ya0_ref, yp0_ref, ya1_ref, yp1_ref)

    @pl.when(s % 2 == 1)
    def _():
        step(ya1_ref, yp1_ref, ya0_ref, yp0_ref)


def _merge(os_, lses, gz3, u3, x, w_pool_bf, pool_scale2, wpa_bf, wpp_bf, w_out_bf, gamma2, beta2, tm=256):
    b, s, _ = x.shape
    tiles_per_seq = s // tm
    n_tiles = b * tiles_per_seq

    def prep_tile(step):
        t = jnp.minimum(step, n_tiles - 1)
        return t // tiles_per_seq, t % tiles_per_seq

    def proj_tile(step):
        t = jnp.maximum(step - 1, 0)
        return t // tiles_per_seq, t % tiles_per_seq

    def tok(which, width, tile=0):
        return pl.BlockSpec((None, tm, width), lambda st: (*which(st), tile))

    def grouped(arr):
        dilation, width = arr.shape[1], arr.shape[3]

        def index(st):
            bb, i = prep_tile(st)
            return bb, 0, i, 0
        if dilation == 1:
            return pl.BlockSpec((None, None, tm, width), index)
        return pl.BlockSpec((None, dilation, tm // dilation, width), index)

    def halo_index(st):
        bb, i = prep_tile(st)
        return bb, jnp.maximum(i * (tm // HALO) - 1, 0), 0

    def const(shape):
        nd = len(shape)
        return pl.BlockSpec(shape, lambda st: (0,) * nd, pipeline_mode=pl.Buffered(1))

    in_specs = [
        grouped(os_[0]), grouped(os_[1]), grouped(os_[2]),
        grouped(lses[0]), grouped(lses[1]), grouped(lses[2]),
        tok(prep_tile, COL_TILE, GZ_TILE_ZATTN), tok(prep_tile, COL_TILE, GZ_TILE_ZPOOL),
        tok(prep_tile, POOL_WIDTH),
        pl.BlockSpec((None, HALO, POOL_WIDTH), halo_index),
        tok(proj_tile, D_MODEL, 0), tok(proj_tile, D_MODEL, 1),
        tok(proj_tile, D_MODEL),
        const((len(POOL_WINDOWS), POOL_GROUP, POOL_GROUP)),
        const((1, POOL_WIDTH)),
        const((ATTN_WIDTH, D_MODEL)),
        const((POOL_WIDTH, D_MODEL)),
        const((D_MODEL, D_MODEL)),
        const((1, D_MODEL)),
        const((1, D_MODEL)),
    ]
    f32 = jnp.float32
    return pl.pallas_call(
        functools.partial(_merge_kernel, tm=tm, n_tiles=n_tiles, tiles_per_seq=tiles_per_seq),
        grid=(n_tiles + 1,),
        in_specs=in_specs,
        out_specs=tok(proj_tile, D_MODEL),
        out_shape=jax.ShapeDtypeStruct((b, s, D_MODEL), f32),
        scratch_shapes=[
            pltpu.VMEM((HALO + tm, POOL_WIDTH), f32),
            pltpu.VMEM((HALO + tm, POOL_WIDTH), f32),
            pltpu.VMEM((HALO + tm, POOL_WIDTH - POOL_GROUP), f32),
            pltpu.VMEM((HALO + tm, POOL_WIDTH - 2 * POOL_GROUP), f32),
            pltpu.VMEM((N_HEADS, tm, HEAD_DIM), f32),
            pltpu.VMEM((N_HEADS, tm, HEAD_DIM), f32),
            pltpu.VMEM((1, tm, LANES), f32),
            pltpu.VMEM((1, tm, LANES), f32),
            pltpu.VMEM((tm, ATTN_WIDTH), jnp.bfloat16),
            pltpu.VMEM((tm, POOL_WIDTH), jnp.bfloat16),
            pltpu.VMEM((tm, ATTN_WIDTH), jnp.bfloat16),
            pltpu.VMEM((tm, POOL_WIDTH), jnp.bfloat16),
            pltpu.VMEM((tm, D_MODEL), jnp.bfloat16),
        ],
        compiler_params=pltpu.CompilerParams(
            dimension_semantics=("arbitrary",),
            vmem_limit_bytes=MERGE_VMEM_LIMIT),
        name="merge_out_ln",
    )(os_[0], os_[1], os_[2], lses[0], lses[1], lses[2], gz3, gz3, u3, u3, gz3, gz3, x,
      w_pool_bf, pool_scale2, wpa_bf, wpp_bf, w_out_bf, gamma2, beta2)


def _layer(x, w_in, b_gate, w_pool, pool_scale, w_proj_attn, w_proj_pool, w_out, ln_gamma, ln_beta):
    b, s, d = x.shape
    u2, x2 = _project_u(x.reshape(b * s, d), w_in)
    later_weights = (w_pool.reshape(-1, POOL_GROUP), w_proj_attn, w_proj_pool, w_out)
    qkvs = []
    for g, (window, dilation) in enumerate(DILATED_GROUPS):
        assert window // dilation == ATTN_BLOCK
        qkv, rounded = _project_qkv(x2, w_in, b, g, dilation, later_weights if g == 0 else ())
        qkvs.append(qkv)
        if g == 0:
            w_pool_bf, wpa_bf, wpp_bf, w_out_bf = rounded
    gz2, os_, lses = _gates_z_and_attention(x2, w_in, b_gate.reshape(1, -1), qkvs)
    gz3 = gz2.reshape(b, s, GZ_TILES * COL_TILE)
    u3 = u2.reshape(b, s, POOL_WIDTH)
    return _merge(os_, lses, gz3, u3, x, w_pool_bf.reshape(w_pool.shape), pool_scale.reshape(1, -1),
                  wpa_bf, wpp_bf, w_out_bf, ln_gamma.reshape(1, -1), ln_beta.reshape(1, -1))


def kernel(x, w_in, b_gate, w_pool, pool_scale, w_proj_attn, w_proj_pool, w_out, ln_gamma, ln_beta):
    depth = w_in.shape[0]
    for layer in range(depth):
        x = _layer(x, w_in[layer], b_gate[layer], w_pool[layer], pool_scale[layer],
                   w_proj_attn[layer], w_proj_pool[layer], w_out[layer],
                   ln_gamma[layer], ln_beta[layer])
    return x
```

```python
import functools

import jax
import jax.numpy as jnp
from jax import lax
from jax.experimental import pallas as pl
from jax.experimental.pallas import tpu as pltpu

D_MODEL = 2048
HEAD_DIM = 128
N_HEADS = 8
ATTN_WIDTH = N_HEADS * HEAD_DIM
DILATED_GROUPS = ((128, 1), (512, 4), (2048, 16))
N_GROUPS = len(DILATED_GROUPS)
POOL_WINDOWS = (2, 4, 8, 16)
POOL_WIDTH = D_MODEL // 2
POOL_GROUP = POOL_WIDTH // len(POOL_WINDOWS)
DEEPNORM_ALPHA = 2.0 ** 0.25
LN_EPS = 1e-5
NEG_INF = -1e30
LOG2_E = 1.4426950408889634

LANES = 128
MIB = 1024 * 1024
SUBLANE_STRIDE = 4
VMEM_BYTES_V7X = 64 * MIB
PROJ_VMEM_LIMIT = VMEM_BYTES_V7X - 10 * MIB
MERGE_VMEM_LIMIT = VMEM_BYTES_V7X - 8 * MIB

COL_TILE = 1024
W_TILE_ZATTN = 9
W_TILE_U = 10
W_TILE_ZPOOL = 11
W_TILE_GATES = 12
GZ_TILES = 6
GZ_TILE_ZATTN = 4
GZ_TILE_ZPOOL = 5
GZ_ROW_TILES = 16
ATTN_STEPS = 32

ATTN_BLOCK = 128
MERGE_CHUNK = 256
HALO = 32


def _cast_weight_tile(first_step, w_ref, wbf_ref):
    @pl.when(first_step)
    def _():
        wbf_ref[...] = w_ref[...].astype(wbf_ref.dtype)


def _proj_qkv_kernel(x_ref, w_ref, *refs, dilation, n_side):
    side_in, o_ref, side_out = refs[:n_side], refs[n_side], refs[n_side + 1:2 * n_side + 1]
    wbf_ref, scratch = refs[2 * n_side + 1], refs[2 * n_side + 2:]
    _cast_weight_tile(jnp.logical_and(pl.program_id(1) == 0, pl.program_id(2) == 0), w_ref, wbf_ref)
    if n_side:
        @pl.when(pl.program_id(0) == 0)
        def _():
            for src, dst in zip(side_in, side_out):
                dst[...] = src[...].astype(dst.dtype)

    acc = jnp.dot(x_ref[...], wbf_ref[...], preferred_element_type=jnp.float32)
    if dilation == 1:
        o_ref[0] = acc.astype(o_ref.dtype)
        return
    acc_ref = scratch[0]
    slabs, tm = acc_ref.shape[0], acc_ref.shape[1]
    for c in range(slabs):
        acc_ref[c] = acc[:, c * LANES:(c + 1) * LANES]
    if dilation == SUBLANE_STRIDE:
        for r in range(dilation):
            for c in range(slabs):
                o_ref[r, :, c * LANES:(c + 1) * LANES] = (
                    acc_ref[c, pl.ds(r, tm // dilation, stride=dilation), :].astype(o_ref.dtype))
        return
    assert dilation == SUBLANE_STRIDE * SUBLANE_STRIDE
    mid_ref = scratch[1]
    quarter = tm // SUBLANE_STRIDE
    for r0 in range(SUBLANE_STRIDE):
        for c in range(slabs):
            mid_ref[c, pl.ds(r0 * quarter, quarter), :] = acc_ref[c, pl.ds(r0, quarter, stride=SUBLANE_STRIDE), :]
    for r0 in range(SUBLANE_STRIDE):
        for r1 in range(SUBLANE_STRIDE):
            for c in range(slabs):
                o_ref[SUBLANE_STRIDE * r1 + r0, :, c * LANES:(c + 1) * LANES] = (
                    mid_ref[c, pl.ds(r0 * quarter + r1, tm // dilation, stride=SUBLANE_STRIDE), :]
                    .astype(o_ref.dtype))


def _proj_u_kernel(x_ref, w_ref, u_ref, xbf_ref, wbf_ref):
    _cast_weight_tile(pl.program_id(0) == 0, w_ref, wbf_ref)
    xb = x_ref[...].astype(xbf_ref.dtype)
    xbf_ref[...] = xb
    u_ref[...] = jnp.dot(xb, wbf_ref[...], preferred_element_type=jnp.float32)


def _project_u(x2, w_in, tm=512):
    m = x2.shape[0]
    return pl.pallas_call(
        _proj_u_kernel,
        grid=(m // tm,),
        in_specs=[
            pl.BlockSpec((tm, D_MODEL), lambda i: (i, 0)),
            pl.BlockSpec((D_MODEL, COL_TILE), lambda i: (0, W_TILE_U), pipeline_mode=pl.Buffered(1)),
        ],
        out_specs=[
            pl.BlockSpec((tm, COL_TILE), lambda i: (i, 0)),
            pl.BlockSpec((tm, D_MODEL), lambda i: (i, 0)),
        ],
        out_shape=[
            jax.ShapeDtypeStruct((m, POOL_WIDTH), jnp.float32),
            jax.ShapeDtypeStruct((m, D_MODEL), jnp.bfloat16),
        ],
        scratch_shapes=[pltpu.VMEM((D_MODEL, COL_TILE), jnp.bfloat16)],
        compiler_params=pltpu.CompilerParams(
            dimension_semantics=("arbitrary",),
            vmem_limit_bytes=PROJ_VMEM_LIMIT),
        name="proj_u",
    )(x2, w_in)


def _project_qkv(x2, w_in, batch, group, dilation, side_weights=(), tm=1024):
    m = x2.shape[0]
    tiles_per_batch = m // batch // tm
    row_tiles = batch * tiles_per_batch
    sub = tm // dilation
    n_row_scratch = {1: 0, SUBLANE_STRIDE: 1, SUBLANE_STRIDE * SUBLANE_STRIDE: 2}[dilation]

    def side_index(j, b, i):
        return jnp.where(j == 0, b * tiles_per_batch + i, row_tiles - 1), 0

    side_specs = [pl.BlockSpec((w.shape[0] // row_tiles, w.shape[1]), side_index) for w in side_weights]
    results = pl.pallas_call(
        functools.partial(_proj_qkv_kernel, dilation=dilation, n_side=len(side_weights)),
        grid=(3, batch, tiles_per_batch),
        in_specs=[
            pl.BlockSpec((tm, D_MODEL), lambda j, b, i: (b * tiles_per_batch + i, 0)),
            pl.BlockSpec((D_MODEL, COL_TILE), lambda j, b, i: (0, N_GROUPS * j + group)),
        ] + side_specs,
        out_specs=[pl.BlockSpec((None, None, dilation, sub, COL_TILE), lambda j, b, i: (j, b, 0, i, 0))]
        + side_specs,
        out_shape=[jax.ShapeDtypeStruct((3, batch, dilation, m // batch // dilation, ATTN_WIDTH), jnp.bfloat16)]
        + [jax.ShapeDtypeStruct(w.shape, jnp.bfloat16) for w in side_weights],
        scratch_shapes=[pltpu.VMEM((D_MODEL, COL_TILE), jnp.bfloat16)]
        + [pltpu.VMEM((COL_TILE // LANES, tm, LANES), jnp.float32)] * n_row_scratch,
        compiler_params=pltpu.CompilerParams(
            dimension_semantics=("arbitrary", "arbitrary", "arbitrary"),
            vmem_limit_bytes=PROJ_VMEM_LIMIT),
        name=f"proj_qkv_g{group}",
    )(x2, w_in, *side_weights)
    return results[0], results[1:]


def _attention_block(blk, first_chunk, q_ref, k_ref, v_ref, o_ref, lse_ref, kc_ref, vc_ref):
    bq = ATTN_BLOCK
    scale = HEAD_DIM ** -0.5
    row = lax.broadcasted_iota(jnp.int32, (bq, 2 * bq), 0)
    col = lax.broadcasted_iota(jnp.int32, (bq, 2 * bq), 1)
    dist = bq + row - col
    mask = jnp.logical_and(dist >= 0, dist <= bq)
    r0 = blk * bq
    if blk == 0:
        mask = jnp.logical_and(mask, col >= jnp.where(first_chunk, bq, 0))

        def kcat_of(hs):
            return jnp.concatenate([kc_ref[:, hs], k_ref[0:bq, hs]], axis=0)

        def vcat_of(hs):
            return jnp.concatenate([vc_ref[:, hs], v_ref[0:bq, hs]], axis=0)
    else:
        def kcat_of(hs):
            return k_ref[r0 - bq:r0 + bq, hs]

        def vcat_of(hs):
            return v_ref[r0 - bq:r0 + bq, hs]

    heads = [slice(h * HEAD_DIM, (h + 1) * HEAD_DIM) for h in range(N_HEADS)]
    ss = [jnp.where(mask,
                    lax.dot_general(q_ref[r0:r0 + bq, hs], kcat_of(hs), (((1,), (1,)), ((), ())),
                                    preferred_element_type=jnp.float32),
                    NEG_INF) for hs in heads]
    ms = [jnp.max(s, axis=-1, keepdims=True) for s in ss]

    def finish():
        es = [jnp.exp2((s - m) * (scale * LOG2_E)) for s, m in zip(ss, ms)]
        dens = [jnp.sum(e, axis=-1, keepdims=True) for e in es]
        outs = [jnp.dot(e.astype(o_ref.dtype), vcat_of(hs), preferred_element_type=jnp.float32)
                for e, hs in zip(es, heads)]
        lane = lax.broadcasted_iota(jnp.int32, (bq, LANES), 1)
        lse_tile = jnp.zeros((bq, LANES), jnp.float32)
        for h, hs in enumerate(heads):
            o_ref[r0:r0 + bq, hs] = (outs[h] / dens[h]).astype(o_ref.dtype)
            lse_tile = jnp.where(lane == h, ms[h] * scale + jnp.log(dens[h]), lse_tile)
        lse_ref[r0:r0 + bq, :] = lse_tile
    return finish


def _gz_attn_kernel(x_ref, w_ref, b_ref, *refs, dilations):
    n_groups = len(dilations)
    qkv_refs = refs[:3 * n_groups]
    gz_ref = refs[3 * n_groups]
    attn_out_refs = refs[3 * n_groups + 1:5 * n_groups + 1]
    wbf_ref, kc_ref, vc_ref = refs[5 * n_groups + 1:]
    s = pl.program_id(0)
    _cast_weight_tile(s % GZ_ROW_TILES == 0, w_ref, wbf_ref)
    chunk_cols = COL_TILE // 4

    for g, dilation in enumerate(dilations):
        q_ref, k_ref, v_ref = qkv_refs[3 * g:3 * g + 3]
        o_ref, lse_ref = attn_out_refs[2 * g:2 * g + 2]
        chunks_per_subseq = GZ_ROW_TILES // dilation
        is_gate = 2 * g + 1 < GZ_TILE_ZATTN

        def gz_chunk(c, is_gate=is_gate):
            cols = slice(c * chunk_cols, (c + 1) * chunk_cols)
            acc = jnp.dot(x_ref[...], wbf_ref[:, cols], preferred_element_type=jnp.float32)
            t = acc + b_ref[:, cols] if is_gate else acc
            sg = 0.5 * jnp.tanh(0.5 * t) + 0.5
            gz_ref[:, cols] = (sg if is_gate else sg * t).astype(gz_ref.dtype)

        @pl.when(s // ATTN_STEPS == g)
        def _(q_ref=q_ref, k_ref=k_ref, v_ref=v_ref, o_ref=o_ref, lse_ref=lse_ref,
              chunks_per_subseq=chunks_per_subseq, gz_chunk=gz_chunk):
            first_chunk = s % chunks_per_subseq == 0

            @pl.when(first_chunk)
            def _():
                kc_ref[...] = jnp.zeros_like(kc_ref)
                vc_ref[...] = jnp.zeros_like(vc_ref)

            gz_chunk(0)
            finish0 = _attention_block(0, first_chunk, q_ref, k_ref, v_ref, o_ref, lse_ref, kc_ref, vc_ref)
            gz_chunk(1)
            finish0()
            gz_chunk(2)
            finish1 = _attention_block(1, first_chunk, q_ref, k_ref, v_ref, o_ref, lse_ref, kc_ref, vc_ref)
            gz_chunk(3)
            finish1()
            kc_ref[...] = k_ref[ATTN_BLOCK:2 * ATTN_BLOCK, :]
            vc_ref[...] = v_ref[ATTN_BLOCK:2 * ATTN_BLOCK, :]


def _gates_z_and_attention(x2, w_in, b_gate2, qkvs):
    m = x2.shape[0]
    tm = m // GZ_ROW_TILES
    rows = 2 * ATTN_BLOCK
    n_groups = len(qkvs)
    assert GZ_TILES * GZ_ROW_TILES == n_groups * ATTN_STEPS
    dilations = tuple(q.shape[2] for q in qkvs)

    def w_tile(j):
        return jnp.where(j < GZ_TILE_ZATTN, W_TILE_GATES + j,
                         jnp.where(j == GZ_TILE_ZATTN, W_TILE_ZATTN, W_TILE_ZPOOL))

    def chunk_of(g, qkv):
        _, batch, dilation, sub_len, _ = qkv.shape
        chunks_per_subseq = sub_len // rows
        per_batch = dilation * chunks_per_subseq
        assert batch * per_batch == ATTN_STEPS and GZ_ROW_TILES % dilation == 0

        def index(step):
            a = jnp.clip(step - g * ATTN_STEPS, 0, ATTN_STEPS - 1)
            rc = a % per_batch
            return a // per_batch, rc // chunks_per_subseq, rc % chunks_per_subseq
        return index

    in_specs = [
        pl.BlockSpec((tm, D_MODEL), lambda st: (st % GZ_ROW_TILES, 0)),
        pl.BlockSpec((D_MODEL, COL_TILE), lambda st: (0, w_tile(st // GZ_ROW_TILES))),
        pl.BlockSpec((1, COL_TILE), lambda st: (0, jnp.minimum(st // GZ_ROW_TILES, GZ_TILE_ZATTN - 1))),
    ]
    out_specs = [pl.BlockSpec((tm, COL_TILE), lambda st: (st % GZ_ROW_TILES, st // GZ_ROW_TILES))]
    out_shape = [jax.ShapeDtypeStruct((m, GZ_TILES * COL_TILE), jnp.bfloat16)]
    operands = [x2, w_in, b_gate2]
    for g, qkv in enumerate(qkvs):
        _, batch, dilation, sub_len, _ = qkv.shape
        index = chunk_of(g, qkv)
        for t in range(3):
            in_specs.append(pl.BlockSpec((None, None, None, rows, ATTN_WIDTH),
                                         lambda st, t=t, index=index: (t, *index(st), 0)))
            operands.append(qkv)
        out_specs.append(pl.BlockSpec((None, None, rows, ATTN_WIDTH), lambda st, index=index: (*index(st), 0)))
        out_specs.append(pl.BlockSpec((None, None, rows, LANES), lambda st, index=index: (*index(st), 0)))
        out_shape.append(jax.ShapeDtypeStruct((batch, dilation, sub_len, ATTN_WIDTH), jnp.bfloat16))
        out_shape.append(jax.ShapeDtypeStruct((batch, dilation, sub_len, LANES), jnp.float32))
    results = pl.pallas_call(
        functools.partial(_gz_attn_kernel, dilations=dilations),
        grid=(GZ_TILES * GZ_ROW_TILES,),
        in_specs=in_specs,
        out_specs=out_specs,
        out_shape=out_shape,
        scratch_shapes=[
            pltpu.VMEM((D_MODEL, COL_TILE), jnp.bfloat16),
            pltpu.VMEM((ATTN_BLOCK, ATTN_WIDTH), jnp.bfloat16),
            pltpu.VMEM((ATTN_BLOCK, ATTN_WIDTH), jnp.bfloat16),
        ],
        compiler_params=pltpu.CompilerParams(
            dimension_semantics=("arbitrary",),
            vmem_limit_bytes=PROJ_VMEM_LIMIT),
        name="gates_z_attention",
    )(*operands)
    gz = results[0]
    os_ = [results[1 + 2 * g] for g in range(n_groups)]
    lses = [results[2 + 2 * g] for g in range(n_groups)]
    return gz, os_, lses


def _to_token_order(src_ref, dst_ref, slab, src_cols, mid_ref):
    dilation, sub = src_ref.shape[0], src_ref.shape[1]
    if dilation == SUBLANE_STRIDE:
        for r in range(dilation):
            dst_ref[slab, pl.ds(r, sub, stride=dilation), :] = src_ref[r, :, src_cols].astype(dst_ref.dtype)
        return
    assert dilation == SUBLANE_STRIDE * SUBLANE_STRIDE
    quarter = sub * SUBLANE_STRIDE
    for r in range(dilation):
        r1, r0 = divmod(r, SUBLANE_STRIDE)
        mid_ref[slab, pl.ds(r0 * quarter + r1, sub, stride=SUBLANE_STRIDE), :] = (
            src_ref[r, :, src_cols].astype(mid_ref.dtype))
    for r0 in range(SUBLANE_STRIDE):
        dst_ref[slab, pl.ds(r0, quarter, stride=SUBLANE_STRIDE), :] = mid_ref[slab, pl.ds(r0 * quarter, quarter), :]


def _combine_weights(l1_ref, l2_ref, l3_ref, ln2_ref, ln3_ref, lmid_ref):
    _to_token_order(l2_ref, ln2_ref, 0, slice(None), lmid_ref)
    _to_token_order(l3_ref, ln3_ref, 0, slice(None), lmid_ref)
    l1 = l1_ref[...]
    l2 = ln2_ref[0]
    l3 = ln3_ref[0]
    mx = jnp.maximum(jnp.maximum(l1, l2), l3)
    e1 = jnp.exp(l1 - mx)
    e2 = jnp.exp(l2 - mx)
    e3 = jnp.exp(l3 - mx)
    tot = e1 + e2 + e3
    return e1 / tot, e2 / tot, e3 / tot


def _combine_head(h, weights, o1_ref, o2_ref, o3_ref, za_ref, on2_ref, on3_ref, omid_ref, ya_ref):
    hs = slice(h * HEAD_DIM, (h + 1) * HEAD_DIM)
    _to_token_order(o2_ref, on2_ref, h, hs, omid_ref)
    _to_token_order(o3_ref, on3_ref, h, hs, omid_ref)
    w1, w2, w3 = weights
    o = (w1[:, h:h + 1] * o1_ref[:, hs].astype(jnp.float32)
         + w2[:, h:h + 1] * on2_ref[h]
         + w3[:, h:h + 1] * on3_ref[h])
    ya_ref[:, hs] = (o * za_ref[:, hs].astype(jnp.float32)).astype(ya_ref.dtype)


def _pool_window_sums(i, u_ref, halo_ref, ext_ref, s2_ref, s4_ref, s8_ref, *, tm):
    rows = HALO + tm
    ext_ref[0:HALO, :] = jnp.where(i > 0, halo_ref[...], 0.0)
    ext_ref[HALO:, :] = u_ref[...]
    g = POOL_GROUP
    s2_ref[8:rows, :] = ext_ref[8:rows, :] + ext_ref[7:rows - 1, :]
    s4_ref[16:rows, :] = s2_ref[16:rows, g:] + s2_ref[14:rows - 2, g:]
    s8_ref[24:rows, :] = s4_ref[24:rows, g:] + s4_ref[20:rows - 4, g:]


def _pool_group(k, i, zp_ref, w_pool_ref, pool_scale_ref, ext_ref, s2_ref, s4_ref, s8_ref, yp_ref, *, tm):
    rows = HALO + tm
    g = POOL_GROUP
    w = POOL_WINDOWS[k]
    if k < 3:
        window_sum = (s2_ref, s4_ref, s8_ref)[k][HALO:rows, 0:g]
    else:
        window_sum = s8_ref[HALO:rows, g:] + s8_ref[HALO - 8:rows - 8, g:]
    cs = slice(k * g, (k + 1) * g)
    pos = i * tm + lax.broadcasted_iota(jnp.int32, (tm, 1), 0)
    cnt = jnp.minimum(pos + 1, w).astype(jnp.float32)
    p = window_sum / cnt - ext_ref[HALO:rows, cs]
    y = jnp.dot(p.astype(jnp.bfloat16), w_pool_ref[k], preferred_element_type=jnp.float32)
    y = y * pool_scale_ref[:, cs] * zp_ref[:, cs].astype(jnp.float32)
    yp_ref[:, cs] = y.astype(yp_ref.dtype)


def _layer_norm_rows(r_ref, gamma_ref, beta_ref):
    r = r_ref[...]
    mu = jnp.mean(r, axis=-1, keepdims=True)
    rc = r - mu
    var = jnp.mean(rc * rc, axis=-1, keepdims=True)
    r_ref[...] = rc * lax.rsqrt(var + LN_EPS) * gamma_ref[...] + beta_ref[...]


def _merge_kernel(o1_ref, o2_ref, o3_ref, l1_ref, l2_ref, l3_ref, za_ref, zp_ref, u_ref, halo_ref,
                  g_attn_ref, g_pool_ref, x_ref, w_pool_ref, pool_scale_ref, wpa_ref, wpp_ref, w_out_ref,
                  gamma_ref, beta_ref, out_ref, ext_ref, s2_ref, s4_ref, s8_ref, on2_ref, on3_ref, omid_ref,
                  ln2_ref, ln3_ref, lmid_ref,
                  ya0_ref, yp0_ref, ya1_ref, yp1_ref, merged_ref, *, tm, n_tiles, tiles_per_seq):
    s = pl.program_id(0)
    i_prep = jnp.minimum(s, n_tiles - 1) % tiles_per_seq

    @pl.when(s == 0)
    def _():
        ya1_ref[...] = jnp.zeros_like(ya1_ref)
        yp1_ref[...] = jnp.zeros_like(yp1_ref)

    def step(ya_w, yp_w, ya_r, yp_r):
        chunks = [slice(c * MERGE_CHUNK, (c + 1) * MERGE_CHUNK) for c in range(D_MODEL // MERGE_CHUNK)]
        ya = ya_r[...]
        yp = yp_r[...]
        weights = _combine_weights(l1_ref, l2_ref, l3_ref, ln2_ref, ln3_ref, lmid_ref)
        _pool_window_sums(i_prep, u_ref, halo_ref, ext_ref, s2_ref, s4_ref, s8_ref, tm=tm)
        for c, cols in enumerate(chunks):
            pa = jnp.dot(ya, wpa_ref[:, cols], preferred_element_type=jnp.float32)
            pp = jnp.dot(yp, wpp_ref[:, cols], preferred_element_type=jnp.float32)
            merged_ref[:, cols] = (g_attn_ref[:, cols].astype(jnp.float32) * pa
                                   + g_pool_ref[:, cols].astype(jnp.float32) * pp).astype(merged_ref.dtype)
            _combine_head(c, weights, o1_ref, o2_ref, o3_ref, za_ref, on2_ref, on3_ref, omid_ref, ya_w)
            if c % 2 == 1:
                _pool_group(c // 2, i_prep, zp_ref, w_pool_ref, pool_scale_ref, ext_ref, s2_ref, s4_ref, s8_ref,
                            yp_w, tm=tm)
        merged = merged_ref[...]
        for cols in chunks:
            out = jnp.dot(merged, w_out_ref[:, cols], preferred_element_type=jnp.float32)
            out_ref[:, cols] = DEEPNORM_ALPHA * x_ref[:, cols] + out
        _layer_norm_rows(out_ref, gamma_ref, beta_ref)

    @pl.when(s % 2 == 0)
    def _():
        step(ya0_ref, yp0_ref, ya1_ref, yp1_ref)

    @pl.when(s % 2 == 1)
    def _():
        step(ya1_ref, yp1_ref, ya0_ref, yp0_ref)


def _merge(os_, lses, gz3, u3, x, w_pool_bf, pool_scale2, wpa_bf, wpp_bf, w_out_bf, gamma2, beta2, tm=256):
    b, s, _ = x.shape
    tiles_per_seq = s // tm
    n_tiles = b * tiles_per_seq

    def prep_tile(step):
        t = jnp.minimum(step, n_tiles - 1)
        return t // tiles_per_seq, t % tiles_per_seq

    def proj_tile(step):
        t = jnp.maximum(step - 1, 0)
        return t // tiles_per_seq, t % tiles_per_seq

    def tok(which, width, tile=0):
        return pl.BlockSpec((None, tm, width), lambda st: (*which(st), tile))

    def grouped(arr):
        dilation, width = arr.shape[1], arr.shape[3]

        def index(st):
            bb, i = prep_tile(st)
            return bb, 0, i, 0
        if dilation == 1:
            return pl.BlockSpec((None, None, tm, width), index)
        return pl.BlockSpec((None, dilation, tm // dilation, width), index)

    def halo_index(st):
        bb, i = prep_tile(st)
        return bb, jnp.maximum(i * (tm // HALO) - 1, 0), 0

    def const(shape):
        nd = len(shape)
        return pl.BlockSpec(shape, lambda st: (0,) * nd, pipeline_mode=pl.Buffered(1))

    in_specs = [
        grouped(os_[0]), grouped(os_[1]), grouped(os_[2]),
        grouped(lses[0]), grouped(lses[1]), grouped(lses[2]),
        tok(prep_tile, COL_TILE, GZ_TILE_ZATTN), tok(prep_tile, COL_TILE, GZ_TILE_ZPOOL),
        tok(prep_tile, POOL_WIDTH),
        pl.BlockSpec((None, HALO, POOL_WIDTH), halo_index),
        tok(proj_tile, D_MODEL, 0), tok(proj_tile, D_MODEL, 1),
        tok(proj_tile, D_MODEL),
        const((len(POOL_WINDOWS), POOL_GROUP, POOL_GROUP)),
        const((1, POOL_WIDTH)),
        const((ATTN_WIDTH, D_MODEL)),
        const((POOL_WIDTH, D_MODEL)),
        const((D_MODEL, D_MODEL)),
        const((1, D_MODEL)),
        const((1, D_MODEL)),
    ]
    f32 = jnp.float32
    return pl.pallas_call(
        functools.partial(_merge_kernel, tm=tm, n_tiles=n_tiles, tiles_per_seq=tiles_per_seq),
        grid=(n_tiles + 1,),
        in_specs=in_specs,
        out_specs=tok(proj_tile, D_MODEL),
        out_shape=jax.ShapeDtypeStruct((b, s, D_MODEL), f32),
        scratch_shapes=[
            pltpu.VMEM((HALO + tm, POOL_WIDTH), f32),
            pltpu.VMEM((HALO + tm, POOL_WIDTH), f32),
            pltpu.VMEM((HALO + tm, POOL_WIDTH - POOL_GROUP), f32),
            pltpu.VMEM((HALO + tm, POOL_WIDTH - 2 * POOL_GROUP), f32),
            pltpu.VMEM((N_HEADS, tm, HEAD_DIM), f32),
            pltpu.VMEM((N_HEADS, tm, HEAD_DIM), f32),
            pltpu.VMEM((N_HEADS, tm, HEAD_DIM), f32),
            pltpu.VMEM((1, tm, LANES), f32),
            pltpu.VMEM((1, tm, LANES), f32),
            pltpu.VMEM((1, tm, LANES), f32),
            pltpu.VMEM((tm, ATTN_WIDTH), jnp.bfloat16),
            pltpu.VMEM((tm, POOL_WIDTH), jnp.bfloat16),
            pltpu.VMEM((tm, ATTN_WIDTH), jnp.bfloat16),
            pltpu.VMEM((tm, POOL_WIDTH), jnp.bfloat16),
            pltpu.VMEM((tm, D_MODEL), jnp.bfloat16),
        ],
        compiler_params=pltpu.CompilerParams(
            dimension_semantics=("arbitrary",),
            vmem_limit_bytes=MERGE_VMEM_LIMIT),
        name="merge_out_ln",
    )(os_[0], os_[1], os_[2], lses[0], lses[1], lses[2], gz3, gz3, u3, u3, gz3, gz3, x,
      w_pool_bf, pool_scale2, wpa_bf, wpp_bf, w_out_bf, gamma2, beta2)


def _layer(x, w_in, b_gate, w_pool, pool_scale, w_proj_attn, w_proj_pool, w_out, ln_gamma, ln_beta):
    b, s, d = x.shape
    u2, x2 = _project_u(x.reshape(b * s, d), w_in)
    later_weights = (w_pool.reshape(-1, POOL_GROUP), w_proj_attn, w_proj_pool, w_out)
    qkvs = []
    for g, (window, dilation) in enumerate(DILATED_GROUPS):
        assert window // dilation == ATTN_BLOCK
        qkv, rounded = _project_qkv(x2, w_in, b, g, dilation, later_weights if g == 0 else ())
        qkvs.append(qkv)
        if g == 0:
            w_pool_bf, wpa_bf, wpp_bf, w_out_bf = rounded
    gz2, os_, lses = _gates_z_and_attention(x2, w_in, b_gate.reshape(1, -1), qkvs)
    gz3 = gz2.reshape(b, s, GZ_TILES * COL_TILE)
    u3 = u2.reshape(b, s, POOL_WIDTH)
    return _merge(os_, lses, gz3, u3, x, w_pool_bf.reshape(w_pool.shape), pool_scale.reshape(1, -1),
                  wpa_bf, wpp_bf, w_out_bf, ln_gamma.reshape(1, -1), ln_beta.reshape(1, -1))


def kernel(x, w_in, b_gate, w_pool, pool_scale, w_proj_attn, w_proj_pool, w_out, ln_gamma, ln_beta):
    depth = w_in.shape[0]
    for layer in range(depth):
        x = _layer(x, w_in[layer], b_gate[layer], w_pool[layer], pool_scale[layer],
                   w_proj_attn[layer], w_proj_pool[layer], w_out[layer],
                   ln_gamma[layer], ln_beta[layer])
    return x
```

```python
import functools

import jax
import jax.numpy as jnp
from jax import lax
from jax.experimental import pallas as pl
from jax.experimental.pallas import tpu as pltpu

D_MODEL = 2048
HEAD_DIM = 128
N_HEADS = 8
ATTN_WIDTH = N_HEADS * HEAD_DIM
DILATED_GROUPS = ((128, 1), (512, 4), (2048, 16))
N_GROUPS = len(DILATED_GROUPS)
POOL_WINDOWS = (2, 4, 8, 16)
POOL_WIDTH = D_MODEL // 2
POOL_GROUP = POOL_WIDTH // len(POOL_WINDOWS)
DEEPNORM_ALPHA = 2.0 ** 0.25
LN_EPS = 1e-5
NEG_INF = -1e30
LOG2_E = 1.4426950408889634

LANES = 128
MIB = 1024 * 1024
SUBLANE_STRIDE = 4
VMEM_BYTES_V7X = 64 * MIB
PROJ_VMEM_LIMIT = VMEM_BYTES_V7X - 10 * MIB
MERGE_VMEM_LIMIT = VMEM_BYTES_V7X - 8 * MIB

COL_TILE = 1024
W_TILE_ZATTN = 9
W_TILE_U = 10
W_TILE_ZPOOL = 11
W_TILE_GATES = 12
GZ_TILES = 6
GZ_TILE_ZATTN = 4
GZ_TILE_ZPOOL = 5
GZ_ROW_TILES = 16
ATTN_STEPS = 32

ATTN_BLOCK = 128
MERGE_CHUNK = 256
HALO = 32


def _cast_weight_tile(first_step, w_ref, wbf_ref):
    @pl.when(first_step)
    def _():
        wbf_ref[...] = w_ref[...].astype(wbf_ref.dtype)


def _proj_qkv_kernel(x_ref, w_ref, *refs, dilation, n_side):
    side_in, o_ref, side_out = refs[:n_side], refs[n_side], refs[n_side + 1:2 * n_side + 1]
    wbf_ref, scratch = refs[2 * n_side + 1], refs[2 * n_side + 2:]
    _cast_weight_tile(jnp.logical_and(pl.program_id(1) == 0, pl.program_id(2) == 0), w_ref, wbf_ref)
    if n_side:
        @pl.when(pl.program_id(0) == 0)
        def _():
            for src, dst in zip(side_in, side_out):
                dst[...] = src[...].astype(dst.dtype)

    acc = jnp.dot(x_ref[...], wbf_ref[...], preferred_element_type=jnp.float32)
    if dilation == 1:
        o_ref[0] = acc.astype(o_ref.dtype)
        return
    acc_ref = scratch[0]
    slabs, tm = acc_ref.shape[0], acc_ref.shape[1]
    for c in range(slabs):
        acc_ref[c] = acc[:, c * LANES:(c + 1) * LANES]
    if dilation == SUBLANE_STRIDE:
        for r in range(dilation):
            for c in range(slabs):
                o_ref[r, :, c * LANES:(c + 1) * LANES] = (
                    acc_ref[c, pl.ds(r, tm // dilation, stride=dilation), :].astype(o_ref.dtype))
        return
    assert dilation == SUBLANE_STRIDE * SUBLANE_STRIDE
    mid_ref = scratch[1]
    quarter = tm // SUBLANE_STRIDE
    for r0 in range(SUBLANE_STRIDE):
        for c in range(slabs):
            mid_ref[c, pl.ds(r0 * quarter, quarter), :] = acc_ref[c, pl.ds(r0, quarter, stride=SUBLANE_STRIDE), :]
    for r0 in range(SUBLANE_STRIDE):
        for r1 in range(SUBLANE_STRIDE):
            for c in range(slabs):
                o_ref[SUBLANE_STRIDE * r1 + r0, :, c * LANES:(c + 1) * LANES] = (
                    mid_ref[c, pl.ds(r0 * quarter + r1, tm // dilation, stride=SUBLANE_STRIDE), :]
                    .astype(o_ref.dtype))


def _proj_u_kernel(x_ref, w_ref, u_ref, xbf_ref, wbf_ref):
    _cast_weight_tile(pl.program_id(0) == 0, w_ref, wbf_ref)
    xb = x_ref[...].astype(xbf_ref.dtype)
    xbf_ref[...] = xb
    u_ref[...] = jnp.dot(xb, wbf_ref[...], preferred_element_type=jnp.float32)


def _project_u(x2, w_in, tm=512):
    m = x2.shape[0]
    return pl.pallas_call(
        _proj_u_kernel,
        grid=(m // tm,),
        in_specs=[
            pl.BlockSpec((tm, D_MODEL), lambda i: (i, 0)),
            pl.BlockSpec((D_MODEL, COL_TILE), lambda i: (0, W_TILE_U), pipeline_mode=pl.Buffered(1)),
        ],
        out_specs=[
            pl.BlockSpec((tm, COL_TILE), lambda i: (i, 0)),
            pl.BlockSpec((tm, D_MODEL), lambda i: (i, 0)),
        ],
        out_shape=[
            jax.ShapeDtypeStruct((m, POOL_WIDTH), jnp.float32),
            jax.ShapeDtypeStruct((m, D_MODEL), jnp.bfloat16),
        ],
        scratch_shapes=[pltpu.VMEM((D_MODEL, COL_TILE), jnp.bfloat16)],
        compiler_params=pltpu.CompilerParams(
            dimension_semantics=("arbitrary",),
            vmem_limit_bytes=PROJ_VMEM_LIMIT),
        name="proj_u",
    )(x2, w_in)


def _project_qkv(x2, w_in, batch, group, dilation, side_weights=(), tm=1024):
    m = x2.shape[0]
    tiles_per_batch = m // batch // tm
    row_tiles = batch * tiles_per_batch
    sub = tm // dilation
    n_row_scratch = {1: 0, SUBLANE_STRIDE: 1, SUBLANE_STRIDE * SUBLANE_STRIDE: 2}[dilation]

    def side_index(j, b, i):
        return jnp.where(j == 0, b * tiles_per_batch + i, row_tiles - 1), 0

    side_specs = [pl.BlockSpec((w.shape[0] // row_tiles, w.shape[1]), side_index) for w in side_weights]
    results = pl.pallas_call(
        functools.partial(_proj_qkv_kernel, dilation=dilation, n_side=len(side_weights)),
        grid=(3, batch, tiles_per_batch),
        in_specs=[
            pl.BlockSpec((tm, D_MODEL), lambda j, b, i: (b * tiles_per_batch + i, 0)),
            pl.BlockSpec((D_MODEL, COL_TILE), lambda j, b, i: (0, N_GROUPS * j + group)),
        ] + side_specs,
        out_specs=[pl.BlockSpec((None, None, dilation, sub, COL_TILE), lambda j, b, i: (j, b, 0, i, 0))]
        + side_specs,
        out_shape=[jax.ShapeDtypeStruct((3, batch, dilation, m // batch // dilation, ATTN_WIDTH), jnp.bfloat16)]
        + [jax.ShapeDtypeStruct(w.shape, jnp.bfloat16) for w in side_weights],
        scratch_shapes=[pltpu.VMEM((D_MODEL, COL_TILE), jnp.bfloat16)]
        + [pltpu.VMEM((COL_TILE // LANES, tm, LANES), jnp.float32)] * n_row_scratch,
        compiler_params=pltpu.CompilerParams(
            dimension_semantics=("arbitrary", "arbitrary", "arbitrary"),
            vmem_limit_bytes=PROJ_VMEM_LIMIT),
        name=f"proj_qkv_g{group}",
    )(x2, w_in, *side_weights)
    return results[0], results[1:]


def _attention_block(blk, first_chunk, q_ref, k_ref, v_ref, o_ref, lse_ref, kc_ref, vc_ref):
    bq = ATTN_BLOCK
    scale = HEAD_DIM ** -0.5
    row = lax.broadcasted_iota(jnp.int32, (bq, 2 * bq), 0)
    col = lax.broadcasted_iota(jnp.int32, (bq, 2 * bq), 1)
    dist = bq + row - col
    mask = jnp.logical_and(dist >= 0, dist <= bq)
    r0 = blk * bq
    if blk == 0:
        mask = jnp.logical_and(mask, col >= jnp.where(first_chunk, bq, 0))

        def kcat_of(hs):
            return jnp.concatenate([kc_ref[:, hs], k_ref[0:bq, hs]], axis=0)

        def vcat_of(hs):
            return jnp.concatenate([vc_ref[:, hs], v_ref[0:bq, hs]], axis=0)
    else:
        def kcat_of(hs):
            return k_ref[r0 - bq:r0 + bq, hs]

        def vcat_of(hs):
            return v_ref[r0 - bq:r0 + bq, hs]

    heads = [slice(h * HEAD_DIM, (h + 1) * HEAD_DIM) for h in range(N_HEADS)]
    ss = [jnp.where(mask,
                    lax.dot_general(q_ref[r0:r0 + bq, hs], kcat_of(hs), (((1,), (1,)), ((), ())),
                                    preferred_element_type=jnp.float32),
                    NEG_INF) for hs in heads]
    ms = [jnp.max(s, axis=-1, keepdims=True) for s in ss]

    def finish():
        es = [jnp.exp2((s - m) * (scale * LOG2_E)) for s, m in zip(ss, ms)]
        dens = [jnp.sum(e, axis=-1, keepdims=True) for e in es]
        outs = [jnp.dot(e.astype(o_ref.dtype), vcat_of(hs), preferred_element_type=jnp.float32)
                for e, hs in zip(es, heads)]
        lane = lax.broadcasted_iota(jnp.int32, (bq, LANES), 1)
        m_tile = jnp.zeros((bq, LANES), jnp.float32)
        den_tile = jnp.ones((bq, LANES), jnp.float32)
        for h in range(N_HEADS):
            m_tile = jnp.where(lane == h, ms[h], m_tile)
            den_tile = jnp.where(lane == h, dens[h], den_tile)
        lse_ref[r0:r0 + bq, :] = m_tile * scale + jnp.log(den_tile)
        inv_tile = 1.0 / den_tile
        for h, hs in enumerate(heads):
            o_ref[r0:r0 + bq, hs] = (outs[h] * inv_tile[:, h:h + 1]).astype(o_ref.dtype)
    return finish


def _gz_attn_kernel(x_ref, w_ref, b_ref, *refs, dilations):
    n_groups = len(dilations)
    qkv_refs = refs[:3 * n_groups]
    gz_ref = refs[3 * n_groups]
    attn_out_refs = refs[3 * n_groups + 1:5 * n_groups + 1]
    wbf_ref, kc_ref, vc_ref = refs[5 * n_groups + 1:]
    s = pl.program_id(0)
    _cast_weight_tile(s % GZ_ROW_TILES == 0, w_ref, wbf_ref)
    chunk_cols = COL_TILE // 4

    for g, dilation in enumerate(dilations):
        q_ref, k_ref, v_ref = qkv_refs[3 * g:3 * g + 3]
        o_ref, lse_ref = attn_out_refs[2 * g:2 * g + 2]
        chunks_per_subseq = GZ_ROW_TILES // dilation
        is_gate = 2 * g + 1 < GZ_TILE_ZATTN

        def gz_chunk(c, is_gate=is_gate):
            cols = slice(c * chunk_cols, (c + 1) * chunk_cols)
            acc = jnp.dot(x_ref[...], wbf_ref[:, cols], preferred_element_type=jnp.float32)
            t = acc + b_ref[:, cols] if is_gate else acc
            sg = 0.5 * jnp.tanh(0.5 * t) + 0.5
            gz_ref[:, cols] = (sg if is_gate else sg * t).astype(gz_ref.dtype)

        @pl.when(s // ATTN_STEPS == g)
        def _(q_ref=q_ref, k_ref=k_ref, v_ref=v_ref, o_ref=o_ref, lse_ref=lse_ref,
              chunks_per_subseq=chunks_per_subseq, gz_chunk=gz_chunk):
            first_chunk = s % chunks_per_subseq == 0

            @pl.when(first_chunk)
            def _():
                kc_ref[...] = jnp.zeros_like(kc_ref)
                vc_ref[...] = jnp.zeros_like(vc_ref)

            gz_chunk(0)
            finish0 = _attention_block(0, first_chunk, q_ref, k_ref, v_ref, o_ref, lse_ref, kc_ref, vc_ref)
            gz_chunk(1)
            finish0()
            gz_chunk(2)
            finish1 = _attention_block(1, first_chunk, q_ref, k_ref, v_ref, o_ref, lse_ref, kc_ref, vc_ref)
            gz_chunk(3)
            finish1()
            kc_ref[...] = k_ref[ATTN_BLOCK:2 * ATTN_BLOCK, :]
            vc_ref[...] = v_ref[ATTN_BLOCK:2 * ATTN_BLOCK, :]


def _gates_z_and_attention(x2, w_in, b_gate2, qkvs):
    m = x2.shape[0]
    tm = m // GZ_ROW_TILES
    rows = 2 * ATTN_BLOCK
    n_groups = len(qkvs)
    assert GZ_TILES * GZ_ROW_TILES == n_groups * ATTN_STEPS
    dilations = tuple(q.shape[2] for q in qkvs)

    def w_tile(j):
        return jnp.where(j < GZ_TILE_ZATTN, W_TILE_GATES + j,
                         jnp.where(j == GZ_TILE_ZATTN, W_TILE_ZATTN, W_TILE_ZPOOL))

    def chunk_of(g, qkv):
        _, batch, dilation, sub_len, _ = qkv.shape
        chunks_per_subseq = sub_len // rows
        per_batch = dilation * chunks_per_subseq
        assert batch * per_batch == ATTN_STEPS and GZ_ROW_TILES % dilation == 0

        def index(step):
            a = jnp.clip(step - g * ATTN_STEPS, 0, ATTN_STEPS - 1)
            rc = a % per_batch
            return a // per_batch, rc // chunks_per_subseq, rc % chunks_per_subseq
        return index

    in_specs = [
        pl.BlockSpec((tm, D_MODEL), lambda st: (st % GZ_ROW_TILES, 0)),
        pl.BlockSpec((D_MODEL, COL_TILE), lambda st: (0, w_tile(st // GZ_ROW_TILES))),
        pl.BlockSpec((1, COL_TILE), lambda st: (0, jnp.minimum(st // GZ_ROW_TILES, GZ_TILE_ZATTN - 1))),
    ]
    out_specs = [pl.BlockSpec((tm, COL_TILE), lambda st: (st % GZ_ROW_TILES, st // GZ_ROW_TILES))]
    out_shape = [jax.ShapeDtypeStruct((m, GZ_TILES * COL_TILE), jnp.bfloat16)]
    operands = [x2, w_in, b_gate2]
    for g, qkv in enumerate(qkvs):
        _, batch, dilation, sub_len, _ = qkv.shape
        index = chunk_of(g, qkv)
        for t in range(3):
            in_specs.append(pl.BlockSpec((None, None, None, rows, ATTN_WIDTH),
                                         lambda st, t=t, index=index: (t, *index(st), 0)))
            operands.append(qkv)
        out_specs.append(pl.BlockSpec((None, None, rows, ATTN_WIDTH), lambda st, index=index: (*index(st), 0)))
        out_specs.append(pl.BlockSpec((None, None, rows, LANES), lambda st, index=index: (*index(st), 0)))
        out_shape.append(jax.ShapeDtypeStruct((batch, dilation, sub_len, ATTN_WIDTH), jnp.bfloat16))
        out_shape.append(jax.ShapeDtypeStruct((batch, dilation, sub_len, LANES), jnp.float32))
    results = pl.pallas_call(
        functools.partial(_gz_attn_kernel, dilations=dilations),
        grid=(GZ_TILES * GZ_ROW_TILES,),
        in_specs=in_specs,
        out_specs=out_specs,
        out_shape=out_shape,
        scratch_shapes=[
            pltpu.VMEM((D_MODEL, COL_TILE), jnp.bfloat16),
            pltpu.VMEM((ATTN_BLOCK, ATTN_WIDTH), jnp.bfloat16),
            pltpu.VMEM((ATTN_BLOCK, ATTN_WIDTH), jnp.bfloat16),
        ],
        compiler_params=pltpu.CompilerParams(
            dimension_semantics=("arbitrary",),
            vmem_limit_bytes=PROJ_VMEM_LIMIT),
        name="gates_z_attention",
    )(*operands)
    gz = results[0]
    os_ = [results[1 + 2 * g] for g in range(n_groups)]
    lses = [results[2 + 2 * g] for g in range(n_groups)]
    return gz, os_, lses


def _to_token_order(src_ref, dst_ref, slab, src_cols, mid_ref):
    dilation, sub = src_ref.shape[0], src_ref.shape[1]
    if dilation == SUBLANE_STRIDE:
        for r in range(dilation):
            dst_ref[slab, pl.ds(r, sub, stride=dilation), :] = src_ref[r, :, src_cols].astype(dst_ref.dtype)
        return
    assert dilation == SUBLANE_STRIDE * SUBLANE_STRIDE
    quarter = sub * SUBLANE_STRIDE
    for r in range(dilation):
        r1, r0 = divmod(r, SUBLANE_STRIDE)
        mid_ref[slab, pl.ds(r0 * quarter + r1, sub, stride=SUBLANE_STRIDE), :] = (
            src_ref[r, :, src_cols].astype(mid_ref.dtype))
    for r0 in range(SUBLANE_STRIDE):
        dst_ref[slab, pl.ds(r0, quarter, stride=SUBLANE_STRIDE), :] = mid_ref[slab, pl.ds(r0 * quarter, quarter), :]


def _combine_weights(l1_ref, l2_ref, l3_ref, ln2_ref, ln3_ref, lmid_ref):
    _to_token_order(l2_ref, ln2_ref, 0, slice(None), lmid_ref)
    _to_token_order(l3_ref, ln3_ref, 0, slice(None), lmid_ref)
    l1 = l1_ref[...]
    l2 = ln2_ref[0]
    l3 = ln3_ref[0]
    mx = jnp.maximum(jnp.maximum(l1, l2), l3)
    e1 = jnp.exp(l1 - mx)
    e2 = jnp.exp(l2 - mx)
    e3 = jnp.exp(l3 - mx)
    tot = e1 + e2 + e3
    return e1 / tot, e2 / tot, e3 / tot


def _combine_head(h, weights, o1_ref, o2_ref, o3_ref, za_ref, on2_ref, on3_ref, omid_ref, ya_ref):
    hs = slice(h * HEAD_DIM, (h + 1) * HEAD_DIM)
    _to_token_order(o2_ref, on2_ref, h, hs, omid_ref)
    _to_token_order(o3_ref, on3_ref, h, hs, omid_ref)
    w1, w2, w3 = weights
    o = (w1[:, h:h + 1] * o1_ref[:, hs].astype(jnp.float32)
         + w2[:, h:h + 1] * on2_ref[h]
         + w3[:, h:h + 1] * on3_ref[h])
    ya_ref[:, hs] = (o * za_ref[:, hs].astype(jnp.float32)).astype(ya_ref.dtype)


def _pool_window_sums(i, u_ref, halo_ref, ext_ref, s2_ref, s4_ref, s8_ref, *, tm):
    rows = HALO + tm
    ext_ref[0:HALO, :] = jnp.where(i > 0, halo_ref[...], 0.0)
    ext_ref[HALO:, :] = u_ref[...]
    g = POOL_GROUP
    s2_ref[8:rows, :] = ext_ref[8:rows, :] + ext_ref[7:rows - 1, :]
    s4_ref[16:rows, :] = s2_ref[16:rows, g:] + s2_ref[14:rows - 2, g:]
    s8_ref[24:rows, :] = s4_ref[24:rows, g:] + s4_ref[20:rows - 4, g:]


def _pool_group(k, i, zp_ref, w_pool_ref, pool_scale_ref, ext_ref, s2_ref, s4_ref, s8_ref, yp_ref, *, tm):
    rows = HALO + tm
    g = POOL_GROUP
    w = POOL_WINDOWS[k]
    if k < 3:
        window_sum = (s2_ref, s4_ref, s8_ref)[k][HALO:rows, 0:g]
    else:
        window_sum = s8_ref[HALO:rows, g:] + s8_ref[HALO - 8:rows - 8, g:]
    cs = slice(k * g, (k + 1) * g)
    pos = i * tm + lax.broadcasted_iota(jnp.int32, (tm, 1), 0)
    cnt = jnp.minimum(pos + 1, w).astype(jnp.float32)
    p = window_sum / cnt - ext_ref[HALO:rows, cs]
    y = jnp.dot(p.astype(jnp.bfloat16), w_pool_ref[k], preferred_element_type=jnp.float32)
    y = y * pool_scale_ref[:, cs] * zp_ref[:, cs].astype(jnp.float32)
    yp_ref[:, cs] = y.astype(yp_ref.dtype)


def _layer_norm_rows(r_ref, gamma_ref, beta_ref):
    r = r_ref[...]
    mu = jnp.mean(r, axis=-1, keepdims=True)
    rc = r - mu
    var = jnp.mean(rc * rc, axis=-1, keepdims=True)
    r_ref[...] = rc * lax.rsqrt(var + LN_EPS) * gamma_ref[...] + beta_ref[...]


def _merge_kernel(o1_ref, o2_ref, o3_ref, l1_ref, l2_ref, l3_ref, za_ref, zp_ref, u_ref, halo_ref,
                  g_attn_ref, g_pool_ref, x_ref, w_pool_ref, pool_scale_ref, wpa_ref, wpp_ref, w_out_ref,
                  gamma_ref, beta_ref, out_ref, ext_ref, s2_ref, s4_ref, s8_ref, on2_ref, on3_ref, omid_ref,
                  ln2_ref, ln3_ref, lmid_ref,
                  ya0_ref, yp0_ref, ya1_ref, yp1_ref, merged_ref, *, tm, n_tiles, tiles_per_seq):
    s = pl.program_id(0)
    i_prep = jnp.minimum(s, n_tiles - 1) % tiles_per_seq

    @pl.when(s == 0)
    def _():
        ya1_ref[...] = jnp.zeros_like(ya1_ref)
        yp1_ref[...] = jnp.zeros_like(yp1_ref)

    def step(ya_w, yp_w, ya_r, yp_r):
        chunks = [slice(c * MERGE_CHUNK, (c + 1) * MERGE_CHUNK) for c in range(D_MODEL // MERGE_CHUNK)]
        ya = ya_r[...]
        yp = yp_r[...]
        weights = _combine_weights(l1_ref, l2_ref, l3_ref, ln2_ref, ln3_ref, lmid_ref)
        _pool_window_sums(i_prep, u_ref, halo_ref, ext_ref, s2_ref, s4_ref, s8_ref, tm=tm)
        for c, cols in enumerate(chunks):
            pa = jnp.dot(ya, wpa_ref[:, cols], preferred_element_type=jnp.float32)
            pp = jnp.dot(yp, wpp_ref[:, cols], preferred_element_type=jnp.float32)
            merged_ref[:, cols] = (g_attn_ref[:, cols].astype(jnp.float32) * pa
                                   + g_pool_ref[:, cols].astype(jnp.float32) * pp).astype(merged_ref.dtype)
            _combine_head(c, weights, o1_ref, o2_ref, o3_ref, za_ref, on2_ref, on3_ref, omid_ref, ya_w)
            if c % 2 == 1:
                _pool_group(c // 2, i_prep, zp_ref, w_pool_ref, pool_scale_ref, ext_ref, s2_ref, s4_ref, s8_ref,
                            yp_w, tm=tm)
        merged = merged_ref[...]
        for cols in chunks:
            out = jnp.dot(merged, w_out_ref[:, cols], preferred_element_type=jnp.float32)
            out_ref[:, cols] = DEEPNORM_ALPHA * x_ref[:, cols] + out
        _layer_norm_rows(out_ref, gamma_ref, beta_ref)

    @pl.when(s % 2 == 0)
    def _():
        step(ya0_ref, yp0_ref, ya1_ref, yp1_ref)

    @pl.when(s % 2 == 1)
    def _():
        step(ya1_ref, yp1_ref, ya0_ref, yp0_ref)


def _merge(os_, lses, gz3, u3, x, w_pool_bf, pool_scale2, wpa_bf, wpp_bf, w_out_bf, gamma2, beta2, tm=256):
    b, s, _ = x.shape
    tiles_per_seq = s // tm
    n_tiles = b * tiles_per_seq

    def prep_tile(step):
        t = jnp.minimum(step, n_tiles - 1)
        return t // tiles_per_seq, t % tiles_per_seq

    def proj_tile(step):
        t = jnp.maximum(step - 1, 0)
        return t // tiles_per_seq, t % tiles_per_seq

    def tok(which, width, tile=0):
        return pl.BlockSpec((None, tm, width), lambda st: (*which(st), tile))

    def grouped(arr):
        dilation, width = arr.shape[1], arr.shape[3]

        def index(st):
            bb, i = prep_tile(st)
            return bb, 0, i, 0
        if dilation == 1:
            return pl.BlockSpec((None, None, tm, width), index)
        return pl.BlockSpec((None, dilation, tm // dilation, width), index)

    def halo_index(st):
        bb, i = prep_tile(st)
        return bb, jnp.maximum(i * (tm // HALO) - 1, 0), 0

    def const(shape):
        nd = len(shape)
        return pl.BlockSpec(shape, lambda st: (0,) * nd, pipeline_mode=pl.Buffered(1))

    in_specs = [
        grouped(os_[0]), grouped(os_[1]), grouped(os_[2]),
        grouped(lses[0]), grouped(lses[1]), grouped(lses[2]),
        tok(prep_tile, COL_TILE, GZ_TILE_ZATTN), tok(prep_tile, COL_TILE, GZ_TILE_ZPOOL),
        tok(prep_tile, POOL_WIDTH),
        pl.BlockSpec((None, HALO, POOL_WIDTH), halo_index),
        tok(proj_tile, D_MODEL, 0), tok(proj_tile, D_MODEL, 1),
        tok(proj_tile, D_MODEL),
        const((len(POOL_WINDOWS), POOL_GROUP, POOL_GROUP)),
        const((1, POOL_WIDTH)),
        const((ATTN_WIDTH, D_MODEL)),
        const((POOL_WIDTH, D_MODEL)),
        const((D_MODEL, D_MODEL)),
        const((1, D_MODEL)),
        const((1, D_MODEL)),
    ]
    f32 = jnp.float32
    return pl.pallas_call(
        functools.partial(_merge_kernel, tm=tm, n_tiles=n_tiles, tiles_per_seq=tiles_per_seq),
        grid=(n_tiles + 1,),
        in_specs=in_specs,
        out_specs=tok(proj_tile, D_MODEL),
        out_shape=jax.ShapeDtypeStruct((b, s, D_MODEL), f32),
        scratch_shapes=[
            pltpu.VMEM((HALO + tm, POOL_WIDTH), f32),
            pltpu.VMEM((HALO + tm, POOL_WIDTH), f32),
            pltpu.VMEM((HALO + tm, POOL_WIDTH - POOL_GROUP), f32),
            pltpu.VMEM((HALO + tm, POOL_WIDTH - 2 * POOL_GROUP), f32),
            pltpu.VMEM((N_HEADS, tm, HEAD_DIM), f32),
            pltpu.VMEM((N_HEADS, tm, HEAD_DIM), f32),
            pltpu.VMEM((N_HEADS, tm, HEAD_DIM), f32),
            pltpu.VMEM((1, tm, LANES), f32),
            pltpu.VMEM((1, tm, LANES), f32),
            pltpu.VMEM((1, tm, LANES), f32),
            pltpu.VMEM((tm, ATTN_WIDTH), jnp.bfloat16),
            pltpu.VMEM((tm, POOL_WIDTH), jnp.bfloat16),
            pltpu.VMEM((tm, ATTN_WIDTH), jnp.bfloat16),
            pltpu.VMEM((tm, POOL_WIDTH), jnp.bfloat16),
            pltpu.VMEM((tm, D_MODEL), jnp.bfloat16),
        ],
        compiler_params=pltpu.CompilerParams(
            dimension_semantics=("arbitrary",),
            vmem_limit_bytes=MERGE_VMEM_LIMIT),
        name="merge_out_ln",
    )(os_[0], os_[1], os_[2], lses[0], lses[1], lses[2], gz3, gz3, u3, u3, gz3, gz3, x,
      w_pool_bf, pool_scale2, wpa_bf, wpp_bf, w_out_bf, gamma2, beta2)


def _layer(x, w_in, b_gate, w_pool, pool_scale, w_proj_attn, w_proj_pool, w_out, ln_gamma, ln_beta):
    b, s, d = x.shape
    u2, x2 = _project_u(x.reshape(b * s, d), w_in)
    later_weights = (w_pool.reshape(-1, POOL_GROUP), w_proj_attn, w_proj_pool, w_out)
    qkvs = []
    for g, (window, dilation) in enumerate(DILATED_GROUPS):
        assert window // dilation == ATTN_BLOCK
        qkv, rounded = _project_qkv(x2, w_in, b, g, dilation, later_weights if g == 0 else ())
        qkvs.append(qkv)
        if g == 0:
            w_pool_bf, wpa_bf, wpp_bf, w_out_bf = rounded
    gz2, os_, lses = _gates_z_and_attention(x2, w_in, b_gate.reshape(1, -1), qkvs)
    gz3 = gz2.reshape(b, s, GZ_TILES * COL_TILE)
    u3 = u2.reshape(b, s, POOL_WIDTH)
    return _merge(os_, lses, gz3, u3, x, w_pool_bf.reshape(w_pool.shape), pool_scale.reshape(1, -1),
                  wpa_bf, wpp_bf, w_out_bf, ln_gamma.reshape(1, -1), ln_beta.reshape(1, -1))


def kernel(x, w_in, b_gate, w_pool, pool_scale, w_proj_attn, w_proj_pool, w_out, ln_gamma, ln_beta):
    depth = w_in.shape[0]
    for layer in range(depth):
        x = _layer(x, w_in[layer], b_gate[layer], w_pool[layer], pool_scale[layer],
                   w_proj_attn[layer], w_proj_pool[layer], w_out[layer],
                   ln_gamma[layer], ln_beta[layer])
    return x
```

```python
import functools

import jax
import jax.numpy as jnp
from jax import lax
from jax.experimental import pallas as pl
from jax.experimental.pallas import tpu as pltpu

D_MODEL = 2048
HEAD_DIM = 128
N_HEADS = 8
ATTN_WIDTH = N_HEADS * HEAD_DIM
DILATED_GROUPS = ((128, 1), (512, 4), (2048, 16))
N_GROUPS = len(DILATED_GROUPS)
POOL_WINDOWS = (2, 4, 8, 16)
POOL_WIDTH = D_MODEL // 2
POOL_GROUP = POOL_WIDTH // len(POOL_WINDOWS)
DEEPNORM_ALPHA = 2.0 ** 0.25
LN_EPS = 1e-5
NEG_INF = -1e30
LOG2_E = 1.4426950408889634

LANES = 128
MIB = 1024 * 1024
SUBLANE_STRIDE = 4
VMEM_BYTES_V7X = 64 * MIB
PROJ_VMEM_LIMIT = VMEM_BYTES_V7X - 10 * MIB
MERGE_VMEM_LIMIT = VMEM_BYTES_V7X - 8 * MIB

COL_TILE = 1024
W_TILE_ZATTN = 9
W_TILE_U = 10
W_TILE_ZPOOL = 11
W_TILE_GATES = 12
GZ_TILES = 6
GZ_TILE_ZATTN = 4
GZ_TILE_ZPOOL = 5
GZ_ROW_TILES = 16
ATTN_STEPS = 32

ATTN_BLOCK = 128
MERGE_CHUNK = 256
HALO = 32


def _cast_weight_tile(first_step, w_ref, wbf_ref):
    @pl.when(first_step)
    def _():
        wbf_ref[...] = w_ref[...].astype(wbf_ref.dtype)


def _proj_qkv_kernel(x_ref, w_ref, *refs, dilation, n_side, n_stream):
    n_extra = n_side + n_stream
    extra_in, o_ref, extra_out = refs[:n_extra], refs[n_extra], refs[n_extra + 1:2 * n_extra + 1]
    wbf_ref, scratch = refs[2 * n_extra + 1], refs[2 * n_extra + 2:]
    _cast_weight_tile(jnp.logical_and(pl.program_id(1) == 0, pl.program_id(2) == 0), w_ref, wbf_ref)
    if n_side:
        @pl.when(pl.program_id(0) == 0)
        def _():
            for src, dst in zip(extra_in[:n_side], extra_out[:n_side]):
                dst[...] = src[...].astype(dst.dtype)
    for src, dst in zip(extra_in[n_side:], extra_out[n_side:]):
        dst[...] = src[...].astype(dst.dtype)

    acc = jnp.dot(x_ref[...], wbf_ref[...], preferred_element_type=jnp.float32)
    if dilation == 1:
        o_ref[0] = acc.astype(o_ref.dtype)
        return
    acc_ref = scratch[0]
    slabs, tm = acc_ref.shape[0], acc_ref.shape[1]
    for c in range(slabs):
        acc_ref[c] = acc[:, c * LANES:(c + 1) * LANES]
    if dilation == SUBLANE_STRIDE:
        for r in range(dilation):
            for c in range(slabs):
                o_ref[r, :, c * LANES:(c + 1) * LANES] = (
                    acc_ref[c, pl.ds(r, tm // dilation, stride=dilation), :].astype(o_ref.dtype))
        return
    assert dilation == SUBLANE_STRIDE * SUBLANE_STRIDE
    mid_ref = scratch[1]
    quarter = tm // SUBLANE_STRIDE
    for r0 in range(SUBLANE_STRIDE):
        for c in range(slabs):
            mid_ref[c, pl.ds(r0 * quarter, quarter), :] = acc_ref[c, pl.ds(r0, quarter, stride=SUBLANE_STRIDE), :]
    for r0 in range(SUBLANE_STRIDE):
        for r1 in range(SUBLANE_STRIDE):
            for c in range(slabs):
                o_ref[SUBLANE_STRIDE * r1 + r0, :, c * LANES:(c + 1) * LANES] = (
                    mid_ref[c, pl.ds(r0 * quarter + r1, tm // dilation, stride=SUBLANE_STRIDE), :]
                    .astype(o_ref.dtype))


def _proj_u_kernel(x_ref, w_ref, u_ref, xbf_ref, wbf_ref):
    _cast_weight_tile(pl.program_id(0) == 0, w_ref, wbf_ref)
    xb = x_ref[...].astype(xbf_ref.dtype)
    xbf_ref[...] = xb
    u_ref[...] = jnp.dot(xb, wbf_ref[...], preferred_element_type=jnp.float32)


def _project_u(x2, w_in, tm=512):
    m = x2.shape[0]
    return pl.pallas_call(
        _proj_u_kernel,
        grid=(m // tm,),
        in_specs=[
            pl.BlockSpec((tm, D_MODEL), lambda i: (i, 0)),
            pl.BlockSpec((D_MODEL, COL_TILE), lambda i: (0, W_TILE_U), pipeline_mode=pl.Buffered(1)),
        ],
        out_specs=[
            pl.BlockSpec((tm, COL_TILE), lambda i: (i, 0)),
            pl.BlockSpec((tm, D_MODEL), lambda i: (i, 0)),
        ],
        out_shape=[
            jax.ShapeDtypeStruct((m, POOL_WIDTH), jnp.float32),
            jax.ShapeDtypeStruct((m, D_MODEL), jnp.bfloat16),
        ],
        scratch_shapes=[pltpu.VMEM((D_MODEL, COL_TILE), jnp.bfloat16)],
        compiler_params=pltpu.CompilerParams(
            dimension_semantics=("arbitrary",),
            vmem_limit_bytes=PROJ_VMEM_LIMIT),
        name="proj_u",
    )(x2, w_in)


def _gz_w_tile(j):
    return jnp.where(j < GZ_TILE_ZATTN, W_TILE_GATES + j, jnp.where(j == GZ_TILE_ZATTN, W_TILE_ZATTN, W_TILE_ZPOOL))


def _project_qkv(x2, w_in, batch, group, dilation, side_weights=(), round_gz_weights=False, tm=1024):
    m = x2.shape[0]
    tiles_per_batch = m // batch // tm
    row_tiles = batch * tiles_per_batch
    sub = tm // dilation
    n_row_scratch = {1: 0, SUBLANE_STRIDE: 1, SUBLANE_STRIDE * SUBLANE_STRIDE: 2}[dilation]

    def side_index(j, b, i):
        return jnp.where(j == 0, b * tiles_per_batch + i, row_tiles - 1), 0

    in_extra = [pl.BlockSpec((w.shape[0] // row_tiles, w.shape[1]), side_index) for w in side_weights]
    out_extra = list(in_extra)
    extra_shapes = [jax.ShapeDtypeStruct(w.shape, jnp.bfloat16) for w in side_weights]
    extra_operands = list(side_weights)
    if round_gz_weights:
        blocks_per_tile = 3 * row_tiles // GZ_TILES
        assert blocks_per_tile * GZ_TILES == 3 * row_tiles

        def step(j, b, i):
            return (j * batch + b) * tiles_per_batch + i

        in_extra.append(pl.BlockSpec(
            (D_MODEL // blocks_per_tile, COL_TILE),
            lambda j, b, i: (step(j, b, i) % blocks_per_tile, _gz_w_tile(step(j, b, i) // blocks_per_tile))))
        out_extra.append(pl.BlockSpec(
            (D_MODEL // blocks_per_tile, COL_TILE),
            lambda j, b, i: (step(j, b, i) % blocks_per_tile, step(j, b, i) // blocks_per_tile)))
        extra_shapes.append(jax.ShapeDtypeStruct((D_MODEL, GZ_TILES * COL_TILE), jnp.bfloat16))
        extra_operands.append(w_in)
    results = pl.pallas_call(
        functools.partial(_proj_qkv_kernel, dilation=dilation, n_side=len(side_weights),
                          n_stream=int(round_gz_weights)),
        grid=(3, batch, tiles_per_batch),
        in_specs=[
            pl.BlockSpec((tm, D_MODEL), lambda j, b, i: (b * tiles_per_batch + i, 0)),
            pl.BlockSpec((D_MODEL, COL_TILE), lambda j, b, i: (0, N_GROUPS * j + group)),
        ] + in_extra,
        out_specs=[pl.BlockSpec((None, None, dilation, sub, COL_TILE), lambda j, b, i: (j, b, 0, i, 0))]
        + out_extra,
        out_shape=[jax.ShapeDtypeStruct((3, batch, dilation, m // batch // dilation, ATTN_WIDTH), jnp.bfloat16)]
        + extra_shapes,
        scratch_shapes=[pltpu.VMEM((D_MODEL, COL_TILE), jnp.bfloat16)]
        + [pltpu.VMEM((COL_TILE // LANES, tm, LANES), jnp.float32)] * n_row_scratch,
        compiler_params=pltpu.CompilerParams(
            dimension_semantics=("arbitrary", "arbitrary", "arbitrary"),
            vmem_limit_bytes=PROJ_VMEM_LIMIT),
        name=f"proj_qkv_g{group}",
    )(x2, w_in, *extra_operands)
    return results[0], results[1:]


def _attention_block(blk, first_chunk, q_ref, k_ref, v_ref, o_ref, lse_ref, kc_ref, vc_ref):
    bq = ATTN_BLOCK
    scale = HEAD_DIM ** -0.5
    row = lax.broadcasted_iota(jnp.int32, (bq, 2 * bq), 0)
    col = lax.broadcasted_iota(jnp.int32, (bq, 2 * bq), 1)
    dist = bq + row - col
    mask = jnp.logical_and(dist >= 0, dist <= bq)
    r0 = blk * bq
    if blk == 0:
        mask = jnp.logical_and(mask, col >= jnp.where(first_chunk, bq, 0))

        def kcat_of(hs):
            return jnp.concatenate([kc_ref[:, hs], k_ref[0:bq, hs]], axis=0)

        def vcat_of(hs):
            return jnp.concatenate([vc_ref[:, hs], v_ref[0:bq, hs]], axis=0)
    else:
        def kcat_of(hs):
            return k_ref[r0 - bq:r0 + bq, hs]

        def vcat_of(hs):
            return v_ref[r0 - bq:r0 + bq, hs]

    heads = [slice(h * HEAD_DIM, (h + 1) * HEAD_DIM) for h in range(N_HEADS)]
    ss = [jnp.where(mask,
                    lax.dot_general(q_ref[r0:r0 + bq, hs], kcat_of(hs), (((1,), (1,)), ((), ())),
                                    preferred_element_type=jnp.float32),
                    NEG_INF) for hs in heads]
    ms = [jnp.max(s, axis=-1, keepdims=True) for s in ss]

    def finish():
        es = [jnp.exp2((s - m) * (scale * LOG2_E)) for s, m in zip(ss, ms)]
        dens = [jnp.sum(e, axis=-1, keepdims=True) for e in es]
        outs = [jnp.dot(e.astype(o_ref.dtype), vcat_of(hs), preferred_element_type=jnp.float32)
                for e, hs in zip(es, heads)]
        lane = lax.broadcasted_iota(jnp.int32, (bq, LANES), 1)
        lse_tile = jnp.zeros((bq, LANES), jnp.float32)
        for h, hs in enumerate(heads):
            o_ref[r0:r0 + bq, hs] = (outs[h] / dens[h]).astype(o_ref.dtype)
            lse_tile = jnp.where(lane == h, ms[h] * scale + jnp.log(dens[h]), lse_tile)
        lse_ref[r0:r0 + bq, :] = lse_tile
    return finish


def _gz_attn_kernel(x_ref, w_ref, b_ref, *refs, dilations):
    n_groups = len(dilations)
    qkv_refs = refs[:3 * n_groups]
    gz_ref = refs[3 * n_groups]
    attn_out_refs = refs[3 * n_groups + 1:5 * n_groups + 1]
    kc_ref, vc_ref = refs[5 * n_groups + 1:]
    s = pl.program_id(0)
    chunk_cols = COL_TILE // 4

    for g, dilation in enumerate(dilations):
        q_ref, k_ref, v_ref = qkv_refs[3 * g:3 * g + 3]
        o_ref, lse_ref = attn_out_refs[2 * g:2 * g + 2]
        chunks_per_subseq = GZ_ROW_TILES // dilation
        is_gate = 2 * g + 1 < GZ_TILE_ZATTN

        def gz_chunk(c, is_gate=is_gate):
            cols = slice(c * chunk_cols, (c + 1) * chunk_cols)
            acc = jnp.dot(x_ref[...], w_ref[:, cols], preferred_element_type=jnp.float32)
            t = acc + b_ref[:, cols] if is_gate else acc
            sg = 0.5 * jnp.tanh(0.5 * t) + 0.5
            gz_ref[:, cols] = (sg if is_gate else sg * t).astype(gz_ref.dtype)

        @pl.when(s // ATTN_STEPS == g)
        def _(q_ref=q_ref, k_ref=k_ref, v_ref=v_ref, o_ref=o_ref, lse_ref=lse_ref,
              chunks_per_subseq=chunks_per_subseq, gz_chunk=gz_chunk):
            first_chunk = s % chunks_per_subseq == 0

            @pl.when(first_chunk)
            def _():
                kc_ref[...] = jnp.zeros_like(kc_ref)
                vc_ref[...] = jnp.zeros_like(vc_ref)

            gz_chunk(0)
            finish0 = _attention_block(0, first_chunk, q_ref, k_ref, v_ref, o_ref, lse_ref, kc_ref, vc_ref)
            gz_chunk(1)
            finish0()
            gz_chunk(2)
            finish1 = _attention_block(1, first_chunk, q_ref, k_ref, v_ref, o_ref, lse_ref, kc_ref, vc_ref)
            gz_chunk(3)
            finish1()
            kc_ref[...] = k_ref[ATTN_BLOCK:2 * ATTN_BLOCK, :]
            vc_ref[...] = v_ref[ATTN_BLOCK:2 * ATTN_BLOCK, :]


def _gates_z_and_attention(x2, w_gz_bf, b_gate2, qkvs):
    m = x2.shape[0]
    tm = m // GZ_ROW_TILES
    rows = 2 * ATTN_BLOCK
    n_groups = len(qkvs)
    assert GZ_TILES * GZ_ROW_TILES == n_groups * ATTN_STEPS
    dilations = tuple(q.shape[2] for q in qkvs)

    def chunk_of(g, qkv):
        _, batch, dilation, sub_len, _ = qkv.shape
        chunks_per_subseq = sub_len // rows
        per_batch = dilation * chunks_per_subseq
        assert batch * per_batch == ATTN_STEPS and GZ_ROW_TILES % dilation == 0

        def index(step):
            a = jnp.clip(step - g * ATTN_STEPS, 0, ATTN_STEPS - 1)
            rc = a % per_batch
            return a // per_batch, rc // chunks_per_subseq, rc % chunks_per_subseq
        return index

    in_specs = [
        pl.BlockSpec((tm, D_MODEL), lambda st: (st % GZ_ROW_TILES, 0)),
        pl.BlockSpec((D_MODEL, COL_TILE), lambda st: (0, st // GZ_ROW_TILES)),
        pl.BlockSpec((1, COL_TILE), lambda st: (0, jnp.minimum(st // GZ_ROW_TILES, GZ_TILE_ZATTN - 1))),
    ]
    out_specs = [pl.BlockSpec((tm, COL_TILE), lambda st: (st % GZ_ROW_TILES, st // GZ_ROW_TILES))]
    out_shape = [jax.ShapeDtypeStruct((m, GZ_TILES * COL_TILE), jnp.bfloat16)]
    operands = [x2, w_gz_bf, b_gate2]
    for g, qkv in enumerate(qkvs):
        _, batch, dilation, sub_len, _ = qkv.shape
        index = chunk_of(g, qkv)
        for t in range(3):
            in_specs.append(pl.BlockSpec((None, None, None, rows, ATTN_WIDTH),
                                         lambda st, t=t, index=index: (t, *index(st), 0)))
            operands.append(qkv)
        out_specs.append(pl.BlockSpec((None, None, rows, ATTN_WIDTH), lambda st, index=index: (*index(st), 0)))
        out_specs.append(pl.BlockSpec((None, None, rows, LANES), lambda st, index=index: (*index(st), 0)))
        out_shape.append(jax.ShapeDtypeStruct((batch, dilation, sub_len, ATTN_WIDTH), jnp.bfloat16))
        out_shape.append(jax.ShapeDtypeStruct((batch, dilation, sub_len, LANES), jnp.float32))
    results = pl.pallas_call(
        functools.partial(_gz_attn_kernel, dilations=dilations),
        grid=(GZ_TILES * GZ_ROW_TILES,),
        in_specs=in_specs,
        out_specs=out_specs,
        out_shape=out_shape,
        scratch_shapes=[
            pltpu.VMEM((ATTN_BLOCK, ATTN_WIDTH), jnp.bfloat16),
            pltpu.VMEM((ATTN_BLOCK, ATTN_WIDTH), jnp.bfloat16),
        ],
        compiler_params=pltpu.CompilerParams(
            dimension_semantics=("arbitrary",),
            vmem_limit_bytes=PROJ_VMEM_LIMIT),
        name="gates_z_attention",
    )(*operands)
    gz = results[0]
    os_ = [results[1 + 2 * g] for g in range(n_groups)]
    lses = [results[2 + 2 * g] for g in range(n_groups)]
    return gz, os_, lses


def _to_token_order(src_ref, dst_ref, slab, src_cols, mid_ref):
    dilation, sub = src_ref.shape[0], src_ref.shape[1]
    if dilation == SUBLANE_STRIDE:
        for r in range(dilation):
            dst_ref[slab, pl.ds(r, sub, stride=dilation), :] = src_ref[r, :, src_cols].astype(dst_ref.dtype)
        return
    assert dilation == SUBLANE_STRIDE * SUBLANE_STRIDE
    quarter = sub * SUBLANE_STRIDE
    for r in range(dilation):
        r1, r0 = divmod(r, SUBLANE_STRIDE)
        mid_ref[slab, pl.ds(r0 * quarter + r1, sub, stride=SUBLANE_STRIDE), :] = (
            src_ref[r, :, src_cols].astype(mid_ref.dtype))
    for r0 in range(SUBLANE_STRIDE):
        dst_ref[slab, pl.ds(r0, quarter, stride=SUBLANE_STRIDE), :] = mid_ref[slab, pl.ds(r0 * quarter, quarter), :]


def _combine_weights(l1_ref, l2_ref, l3_ref, ln2_ref, ln3_ref, lmid_ref):
    _to_token_order(l2_ref, ln2_ref, 0, slice(None), lmid_ref)
    _to_token_order(l3_ref, ln3_ref, 0, slice(None), lmid_ref)
    l1 = l1_ref[...]
    l2 = ln2_ref[0]
    l3 = ln3_ref[0]
    mx = jnp.maximum(jnp.maximum(l1, l2), l3)
    e1 = jnp.exp(l1 - mx)
    e2 = jnp.exp(l2 - mx)
    e3 = jnp.exp(l3 - mx)
    tot = e1 + e2 + e3
    return e1 / tot, e2 / tot, e3 / tot


def _combine_head(h, weights, o1_ref, o2_ref, o3_ref, za_ref, on2_ref, on3_ref, omid_ref, ya_ref):
    hs = slice(h * HEAD_DIM, (h + 1) * HEAD_DIM)
    _to_token_order(o2_ref, on2_ref, h, hs, omid_ref)
    _to_token_order(o3_ref, on3_ref, h, hs, omid_ref)
    w1, w2, w3 = weights
    o = (w1[:, h:h + 1] * o1_ref[:, hs].astype(jnp.float32)
         + w2[:, h:h + 1] * on2_ref[h]
         + w3[:, h:h + 1] * on3_ref[h])
    ya_ref[:, hs] = (o * za_ref[:, hs].astype(jnp.float32)).astype(ya_ref.dtype)


def _pool_window_sums(i, u_ref, halo_ref, ext_ref, s2_ref, s4_ref, s8_ref, *, tm):
    rows = HALO + tm
    ext_ref[0:HALO, :] = jnp.where(i > 0, halo_ref[...], 0.0)
    ext_ref[HALO:, :] = u_ref[...]
    g = POOL_GROUP
    s2_ref[8:rows, :] = ext_ref[8:rows, :] + ext_ref[7:rows - 1, :]
    s4_ref[16:rows, :] = s2_ref[16:rows, g:] + s2_ref[14:rows - 2, g:]
    s8_ref[24:rows, :] = s4_ref[24:rows, g:] + s4_ref[20:rows - 4, g:]


def _pool_group(k, i, zp_ref, w_pool_ref, pool_scale_ref, ext_ref, s2_ref, s4_ref, s8_ref, yp_ref, *, tm):
    rows = HALO + tm
    g = POOL_GROUP
    w = POOL_WINDOWS[k]
    if k < 3:
        window_sum = (s2_ref, s4_ref, s8_ref)[k][HALO:rows, 0:g]
    else:
        window_sum = s8_ref[HALO:rows, g:] + s8_ref[HALO - 8:rows - 8, g:]
    cs = slice(k * g, (k + 1) * g)
    pos = i * tm + lax.broadcasted_iota(jnp.int32, (tm, 1), 0)
    cnt = jnp.minimum(pos + 1, w).astype(jnp.float32)
    p = window_sum / cnt - ext_ref[HALO:rows, cs]
    y = jnp.dot(p.astype(jnp.bfloat16), w_pool_ref[k], preferred_element_type=jnp.float32)
    y = y * pool_scale_ref[:, cs] * zp_ref[:, cs].astype(jnp.float32)
    yp_ref[:, cs] = y.astype(yp_ref.dtype)


def _layer_norm_rows(r_ref, gamma_ref, beta_ref):
    r = r_ref[...]
    mu = jnp.mean(r, axis=-1, keepdims=True)
    rc = r - mu
    var = jnp.mean(rc * rc, axis=-1, keepdims=True)
    r_ref[...] = rc * lax.rsqrt(var + LN_EPS) * gamma_ref[...] + beta_ref[...]


def _merge_kernel(o1_ref, o2_ref, o3_ref, l1_ref, l2_ref, l3_ref, za_ref, zp_ref, u_ref, halo_ref,
                  g_attn_ref, g_pool_ref, x_ref, w_pool_ref, pool_scale_ref, wpa_ref, wpp_ref, w_out_ref,
                  gamma_ref, beta_ref, out_ref, ext_ref, s2_ref, s4_ref, s8_ref, on2_ref, on3_ref, omid_ref,
                  ln2_ref, ln3_ref, lmid_ref,
                  ya0_ref, yp0_ref, ya1_ref, yp1_ref, merged_ref, *, tm, n_tiles, tiles_per_seq):
    s = pl.program_id(0)
    i_prep = jnp.minimum(s, n_tiles - 1) % tiles_per_seq

    @pl.when(s == 0)
    def _():
        ya1_ref[...] = jnp.zeros_like(ya1_ref)
        yp1_ref[...] = jnp.zeros_like(yp1_ref)

    def step(ya_w, yp_w, ya_r, yp_r):
        chunks = [slice(c * MERGE_CHUNK, (c + 1) * MERGE_CHUNK) for c in range(D_MODEL // MERGE_CHUNK)]
        ya = ya_r[...]
        yp = yp_r[...]
        weights = _combine_weights(l1_ref, l2_ref, l3_ref, ln2_ref, ln3_ref, lmid_ref)
        _pool_window_sums(i_prep, u_ref, halo_ref, ext_ref, s2_ref, s4_ref, s8_ref, tm=tm)
        for c, cols in enumerate(chunks):
            pa = jnp.dot(ya, wpa_ref[:, cols], preferred_element_type=jnp.float32)
            pp = jnp.dot(yp, wpp_ref[:, cols], preferred_element_type=jnp.float32)
            merged_ref[:, cols] = (g_attn_ref[:, cols].astype(jnp.float32) * pa
                                   + g_pool_ref[:, cols].astype(jnp.float32) * pp).astype(merged_ref.dtype)
            _combine_head(c, weights, o1_ref, o2_ref, o3_ref, za_ref, on2_ref, on3_ref, omid_ref, ya_w)
            if c % 2 == 1:
                _pool_group(c // 2, i_prep, zp_ref, w_pool_ref, pool_scale_ref, ext_ref, s2_ref, s4_ref, s8_ref,
                            yp_w, tm=tm)
        merged = merged_ref[...]
        for cols in chunks:
            out = jnp.dot(merged, w_out_ref[:, cols], preferred_element_type=jnp.float32)
            out_ref[:, cols] = DEEPNORM_ALPHA * x_ref[:, cols] + out
        _layer_norm_rows(out_ref, gamma_ref, beta_ref)

    @pl.when(s % 2 == 0)
    def _():
        step(ya0_ref, yp0_ref, ya1_ref, yp1_ref)

    @pl.when(s % 2 == 1)
    def _():
        step(ya1_ref, yp1_ref, ya0_ref, yp0_ref)


def _merge(os_, lses, gz3, u3, x, w_pool_bf, pool_scale2, wpa_bf, wpp_bf, w_out_bf, gamma2, beta2, tm=256):
    b, s, _ = x.shape
    tiles_per_seq = s // tm
    n_tiles = b * tiles_per_seq

    def prep_tile(step):
        t = jnp.minimum(step, n_tiles - 1)
        return t // tiles_per_seq, t % tiles_per_seq

    def proj_tile(step):
        t = jnp.maximum(step - 1, 0)
        return t // tiles_per_seq, t % tiles_per_seq

    def tok(which, width, tile=0):
        return pl.BlockSpec((None, tm, width), lambda st: (*which(st), tile))

    def grouped(arr):
        dilation, width = arr.shape[1], arr.shape[3]

        def index(st):
            bb, i = prep_tile(st)
            return bb, 0, i, 0
        if dilation == 1:
            return pl.BlockSpec((None, None, tm, width), index)
        return pl.BlockSpec((None, dilation, tm // dilation, width), index)

    def halo_index(st):
        bb, i = prep_tile(st)
        return bb, jnp.maximum(i * (tm // HALO) - 1, 0), 0

    def const(shape):
        nd = len(shape)
        return pl.BlockSpec(shape, lambda st: (0,) * nd, pipeline_mode=pl.Buffered(1))

    in_specs = [
        grouped(os_[0]), grouped(os_[1]), grouped(os_[2]),
        grouped(lses[0]), grouped(lses[1]), grouped(lses[2]),
        tok(prep_tile, COL_TILE, GZ_TILE_ZATTN), tok(prep_tile, COL_TILE, GZ_TILE_ZPOOL),
        tok(prep_tile, POOL_WIDTH),
        pl.BlockSpec((None, HALO, POOL_WIDTH), halo_index),
        tok(proj_tile, D_MODEL, 0), tok(proj_tile, D_MODEL, 1),
        tok(proj_tile, D_MODEL),
        const((len(POOL_WINDOWS), POOL_GROUP, POOL_GROUP)),
        const((1, POOL_WIDTH)),
        const((ATTN_WIDTH, D_MODEL)),
        const((POOL_WIDTH, D_MODEL)),
        const((D_MODEL, D_MODEL)),
        const((1, D_MODEL)),
        const((1, D_MODEL)),
    ]
    f32 = jnp.float32
    return pl.pallas_call(
        functools.partial(_merge_kernel, tm=tm, n_tiles=n_tiles, tiles_per_seq=tiles_per_seq),
        grid=(n_tiles + 1,),
        in_specs=in_specs,
        out_specs=tok(proj_tile, D_MODEL),
        out_shape=jax.ShapeDtypeStruct((b, s, D_MODEL), f32),
        scratch_shapes=[
            pltpu.VMEM((HALO + tm, POOL_WIDTH), f32),
            pltpu.VMEM((HALO + tm, POOL_WIDTH), f32),
            pltpu.VMEM((HALO + tm, POOL_WIDTH - POOL_GROUP), f32),
            pltpu.VMEM((HALO + tm, POOL_WIDTH - 2 * POOL_GROUP), f32),
            pltpu.VMEM((N_HEADS, tm, HEAD_DIM), f32),
            pltpu.VMEM((N_HEADS, tm, HEAD_DIM), f32),
            pltpu.VMEM((N_HEADS, tm, HEAD_DIM), f32),
            pltpu.VMEM((1, tm, LANES), f32),
            pltpu.VMEM((1, tm, LANES), f32),
            pltpu.VMEM((1, tm, LANES), f32),
            pltpu.VMEM((tm, ATTN_WIDTH), jnp.bfloat16),
            pltpu.VMEM((tm, POOL_WIDTH), jnp.bfloat16),
            pltpu.VMEM((tm, ATTN_WIDTH), jnp.bfloat16),
            pltpu.VMEM((tm, POOL_WIDTH), jnp.bfloat16),
            pltpu.VMEM((tm, D_MODEL), jnp.bfloat16),
        ],
        compiler_params=pltpu.CompilerParams(
            dimension_semantics=("arbitrary",),
            vmem_limit_bytes=MERGE_VMEM_LIMIT),
        name="merge_out_ln",
    )(os_[0], os_[1], os_[2], lses[0], lses[1], lses[2], gz3, gz3, u3, u3, gz3, gz3, x,
      w_pool_bf, pool_scale2, wpa_bf, wpp_bf, w_out_bf, gamma2, beta2)


def _layer(x, w_in, b_gate, w_pool, pool_scale, w_proj_attn, w_proj_pool, w_out, ln_gamma, ln_beta):
    b, s, d = x.shape
    u2, x2 = _project_u(x.reshape(b * s, d), w_in)
    later_weights = (w_pool.reshape(-1, POOL_GROUP), w_proj_attn, w_proj_pool, w_out)
    last = N_GROUPS - 1
    qkvs = []
    for g, (window, dilation) in enumerate(DILATED_GROUPS):
        assert window // dilation == ATTN_BLOCK
        qkv, rounded = _project_qkv(x2, w_in, b, g, dilation, later_weights if g == 0 else (),
                                    round_gz_weights=(g == last))
        qkvs.append(qkv)
        if g == 0:
            w_pool_bf, wpa_bf, wpp_bf, w_out_bf = rounded
        if g == last:
            (w_gz_bf,) = rounded
    gz2, os_, lses = _gates_z_and_attention(x2, w_gz_bf, b_gate.reshape(1, -1), qkvs)
    gz3 = gz2.reshape(b, s, GZ_TILES * COL_TILE)
    u3 = u2.reshape(b, s, POOL_WIDTH)
    return _merge(os_, lses, gz3, u3, x, w_pool_bf.reshape(w_pool.shape), pool_scale.reshape(1, -1),
                  wpa_bf, wpp_bf, w_out_bf, ln_gamma.reshape(1, -1), ln_beta.reshape(1, -1))


def kernel(x, w_in, b_gate, w_pool, pool_scale, w_proj_attn, w_proj_pool, w_out, ln_gamma, ln_beta):
    depth = w_in.shape[0]
    for layer in range(depth):
        x = _layer(x, w_in[layer], b_gate[layer], w_pool[layer], pool_scale[layer],
                   w_proj_attn[layer], w_proj_pool[layer], w_out[layer],
                   ln_gamma[layer], ln_beta[layer])
    return x
```

```python
import functools

import jax
import jax.numpy as jnp
from jax import lax
from jax.experimental import pallas as pl
from jax.experimental.pallas import tpu as pltpu

D_MODEL = 2048
HEAD_DIM = 128
N_HEADS = 8
ATTN_WIDTH = N_HEADS * HEAD_DIM
DILATED_GROUPS = ((128, 1), (512, 4), (2048, 16))
N_GROUPS = len(DILATED_GROUPS)
POOL_WINDOWS = (2, 4, 8, 16)
POOL_WIDTH = D_MODEL // 2
POOL_GROUP = POOL_WIDTH // len(POOL_WINDOWS)
DEEPNORM_ALPHA = 2.0 ** 0.25
LN_EPS = 1e-5
NEG_INF = -1e30
LOG2_E = 1.4426950408889634

LANES = 128
MIB = 1024 * 1024
SUBLANE_STRIDE = 4
VMEM_BYTES_V7X = 64 * MIB
PROJ_VMEM_LIMIT = VMEM_BYTES_V7X - 10 * MIB
MERGE_VMEM_LIMIT = VMEM_BYTES_V7X - 8 * MIB

COL_TILE = 1024
W_TILE_ZATTN = 9
W_TILE_U = 10
W_TILE_ZPOOL = 11
W_TILE_GATES = 12
GZ_TILES = 6
GZ_TILE_ZATTN = 4
GZ_TILE_ZPOOL = 5
GZ_ROW_TILES = 16
ATTN_STEPS = 32

ATTN_BLOCK = 128
MERGE_CHUNK = 256
HALO = 32


def _cast_weight_tile(first_step, w_ref, wbf_ref):
    @pl.when(first_step)
    def _():
        wbf_ref[...] = w_ref[...].astype(wbf_ref.dtype)


def _proj_qkv_kernel(x_ref, w_ref, *refs, dilation, n_side, n_stream):
    n_extra = n_side + n_stream
    extra_in, o_ref, extra_out = refs[:n_extra], refs[n_extra], refs[n_extra + 1:2 * n_extra + 1]
    wbf_ref, scratch = refs[2 * n_extra + 1], refs[2 * n_extra + 2:]
    _cast_weight_tile(jnp.logical_and(pl.program_id(1) == 0, pl.program_id(2) == 0), w_ref, wbf_ref)
    if n_side:
        @pl.when(pl.program_id(0) == 0)
        def _():
            for src, dst in zip(extra_in[:n_side], extra_out[:n_side]):
                dst[...] = src[...].astype(dst.dtype)
    for src, dst in zip(extra_in[n_side:], extra_out[n_side:]):
        dst[...] = src[...].astype(dst.dtype)

    acc = jnp.dot(x_ref[...], wbf_ref[...], preferred_element_type=jnp.float32)
    if dilation == 1:
        o_ref[0] = acc.astype(o_ref.dtype)
        return
    acc_ref = scratch[0]
    slabs, tm = acc_ref.shape[0], acc_ref.shape[1]
    for c in range(slabs):
        acc_ref[c] = acc[:, c * LANES:(c + 1) * LANES]
    if dilation == SUBLANE_STRIDE:
        for r in range(dilation):
            for c in range(slabs):
                o_ref[r, :, c * LANES:(c + 1) * LANES] = (
                    acc_ref[c, pl.ds(r, tm // dilation, stride=dilation), :].astype(o_ref.dtype))
        return
    assert dilation == SUBLANE_STRIDE * SUBLANE_STRIDE
    mid_ref = scratch[1]
    quarter = tm // SUBLANE_STRIDE
    for r0 in range(SUBLANE_STRIDE):
        for c in range(slabs):
            mid_ref[c, pl.ds(r0 * quarter, quarter), :] = acc_ref[c, pl.ds(r0, quarter, stride=SUBLANE_STRIDE), :]
    for r0 in range(SUBLANE_STRIDE):
        for r1 in range(SUBLANE_STRIDE):
            for c in range(slabs):
                o_ref[SUBLANE_STRIDE * r1 + r0, :, c * LANES:(c + 1) * LANES] = (
                    mid_ref[c, pl.ds(r0 * quarter + r1, tm // dilation, stride=SUBLANE_STRIDE), :]
                    .astype(o_ref.dtype))


def _proj_u_kernel(x_ref, w_ref, u_ref, xbf_ref, wbf_ref):
    _cast_weight_tile(pl.program_id(0) == 0, w_ref, wbf_ref)
    xb = x_ref[...].astype(xbf_ref.dtype)
    xbf_ref[...] = xb
    u_ref[...] = jnp.dot(xb, wbf_ref[...], preferred_element_type=jnp.float32)


def _project_u(x2, w_in, tm=512):
    m = x2.shape[0]
    return pl.pallas_call(
        _proj_u_kernel,
        grid=(m // tm,),
        in_specs=[
            pl.BlockSpec((tm, D_MODEL), lambda i: (i, 0)),
            pl.BlockSpec((D_MODEL, COL_TILE), lambda i: (0, W_TILE_U), pipeline_mode=pl.Buffered(1)),
        ],
        out_specs=[
            pl.BlockSpec((tm, COL_TILE), lambda i: (i, 0)),
            pl.BlockSpec((tm, D_MODEL), lambda i: (i, 0)),
        ],
        out_shape=[
            jax.ShapeDtypeStruct((m, POOL_WIDTH), jnp.float32),
            jax.ShapeDtypeStruct((m, D_MODEL), jnp.bfloat16),
        ],
        scratch_shapes=[pltpu.VMEM((D_MODEL, COL_TILE), jnp.bfloat16)],
        compiler_params=pltpu.CompilerParams(
            dimension_semantics=("arbitrary",),
            vmem_limit_bytes=PROJ_VMEM_LIMIT),
        name="proj_u",
    )(x2, w_in)


def _gz_w_tile(j):
    return jnp.where(j < GZ_TILE_ZATTN, W_TILE_GATES + j, jnp.where(j == GZ_TILE_ZATTN, W_TILE_ZATTN, W_TILE_ZPOOL))


def _project_qkv(x2, w_in, batch, group, dilation, side_weights=(), round_gz_weights=False, tm=1024):
    m = x2.shape[0]
    tiles_per_batch = m // batch // tm
    row_tiles = batch * tiles_per_batch
    sub = tm // dilation
    n_row_scratch = {1: 0, SUBLANE_STRIDE: 1, SUBLANE_STRIDE * SUBLANE_STRIDE: 2}[dilation]

    def side_index(j, b, i):
        return jnp.where(j == 0, b * tiles_per_batch + i, row_tiles - 1), 0

    in_extra = [pl.BlockSpec((w.shape[0] // row_tiles, w.shape[1]), side_index) for w in side_weights]
    out_extra = list(in_extra)
    extra_shapes = [jax.ShapeDtypeStruct(w.shape, jnp.bfloat16) for w in side_weights]
    extra_operands = list(side_weights)
    if round_gz_weights:
        blocks_per_tile = 3 * row_tiles // GZ_TILES
        assert blocks_per_tile * GZ_TILES == 3 * row_tiles

        def step(j, b, i):
            return (j * batch + b) * tiles_per_batch + i

        in_extra.append(pl.BlockSpec(
            (D_MODEL // blocks_per_tile, COL_TILE),
            lambda j, b, i: (step(j, b, i) % blocks_per_tile, _gz_w_tile(step(j, b, i) // blocks_per_tile))))
        out_extra.append(pl.BlockSpec(
            (D_MODEL // blocks_per_tile, COL_TILE),
            lambda j, b, i: (step(j, b, i) % blocks_per_tile, step(j, b, i) // blocks_per_tile)))
        extra_shapes.append(jax.ShapeDtypeStruct((D_MODEL, GZ_TILES * COL_TILE), jnp.bfloat16))
        extra_operands.append(w_in)
    results = pl.pallas_call(
        functools.partial(_proj_qkv_kernel, dilation=dilation, n_side=len(side_weights),
                          n_stream=int(round_gz_weights)),
        grid=(3, batch, tiles_per_batch),
        in_specs=[
            pl.BlockSpec((tm, D_MODEL), lambda j, b, i: (b * tiles_per_batch + i, 0)),
            pl.BlockSpec((D_MODEL, COL_TILE), lambda j, b, i: (0, N_GROUPS * j + group)),
        ] + in_extra,
        out_specs=[pl.BlockSpec((None, None, dilation, sub, COL_TILE), lambda j, b, i: (j, b, 0, i, 0))]
        + out_extra,
        out_shape=[jax.ShapeDtypeStruct((3, batch, dilation, m // batch // dilation, ATTN_WIDTH), jnp.bfloat16)]
        + extra_shapes,
        scratch_shapes=[pltpu.VMEM((D_MODEL, COL_TILE), jnp.bfloat16)]
        + [pltpu.VMEM((COL_TILE // LANES, tm, LANES), jnp.float32)] * n_row_scratch,
        compiler_params=pltpu.CompilerParams(
            dimension_semantics=("arbitrary", "arbitrary", "arbitrary"),
            vmem_limit_bytes=PROJ_VMEM_LIMIT),
        name=f"proj_qkv_g{group}",
    )(x2, w_in, *extra_operands)
    return results[0], results[1:]


def _attention_block(blk, first_chunk, q_ref, k_ref, v_ref, o_ref, lse_ref, kc_ref, vc_ref):
    bq = ATTN_BLOCK
    scale = HEAD_DIM ** -0.5
    row = lax.broadcasted_iota(jnp.int32, (bq, 2 * bq), 0)
    col = lax.broadcasted_iota(jnp.int32, (bq, 2 * bq), 1)
    dist = bq + row - col
    mask = jnp.logical_and(dist >= 0, dist <= bq)
    r0 = blk * bq
    if blk == 0:
        mask = jnp.logical_and(mask, col >= jnp.where(first_chunk, bq, 0))

        def kcat_of(hs):
            return jnp.concatenate([kc_ref[:, hs], k_ref[0:bq, hs]], axis=0)

        def vcat_of(hs):
            return jnp.concatenate([vc_ref[:, hs], v_ref[0:bq, hs]], axis=0)
    else:
        def kcat_of(hs):
            return k_ref[r0 - bq:r0 + bq, hs]

        def vcat_of(hs):
            return v_ref[r0 - bq:r0 + bq, hs]

    heads = [slice(h * HEAD_DIM, (h + 1) * HEAD_DIM) for h in range(N_HEADS)]
    ss = [jnp.where(mask,
                    lax.dot_general(q_ref[r0:r0 + bq, hs], kcat_of(hs), (((1,), (1,)), ((), ())),
                                    preferred_element_type=jnp.float32),
                    NEG_INF) for hs in heads]
    ms = [jnp.max(s, axis=-1, keepdims=True) for s in ss]

    def finish():
        es = [jnp.exp2((s - m) * (scale * LOG2_E)) for s, m in zip(ss, ms)]
        dens = [jnp.sum(e, axis=-1, keepdims=True) for e in es]
        outs = [jnp.dot(e.astype(o_ref.dtype), vcat_of(hs), preferred_element_type=jnp.float32)
                for e, hs in zip(es, heads)]
        lane = lax.broadcasted_iota(jnp.int32, (bq, LANES), 1)
        m_tile = jnp.zeros((bq, LANES), jnp.float32)
        den_tile = jnp.ones((bq, LANES), jnp.float32)
        for h in range(N_HEADS):
            m_tile = jnp.where(lane == h, ms[h], m_tile)
            den_tile = jnp.where(lane == h, dens[h], den_tile)
        lse_ref[r0:r0 + bq, :] = m_tile * scale + jnp.log(den_tile)
        inv_tile = 1.0 / den_tile
        for h, hs in enumerate(heads):
            o_ref[r0:r0 + bq, hs] = (outs[h] * inv_tile[:, h:h + 1]).astype(o_ref.dtype)
    return finish


def _gz_attn_kernel(x_ref, w_ref, b_ref, *refs, dilations):
    n_groups = len(dilations)
    qkv_refs = refs[:3 * n_groups]
    gz_ref = refs[3 * n_groups]
    attn_out_refs = refs[3 * n_groups + 1:5 * n_groups + 1]
    kc_ref, vc_ref = refs[5 * n_groups + 1:]
    s = pl.program_id(0)
    chunk_cols = COL_TILE // 4

    for g, dilation in enumerate(dilations):
        q_ref, k_ref, v_ref = qkv_refs[3 * g:3 * g + 3]
        o_ref, lse_ref = attn_out_refs[2 * g:2 * g + 2]
        chunks_per_subseq = GZ_ROW_TILES // dilation
        is_gate = 2 * g + 1 < GZ_TILE_ZATTN

        def gz_chunk(c, is_gate=is_gate):
            cols = slice(c * chunk_cols, (c + 1) * chunk_cols)
            acc = jnp.dot(x_ref[...], w_ref[:, cols], preferred_element_type=jnp.float32)
            t = acc + b_ref[:, cols] if is_gate else acc
            sg = 0.5 * jnp.tanh(0.5 * t) + 0.5
            gz_ref[:, cols] = (sg if is_gate else sg * t).astype(gz_ref.dtype)

        @pl.when(s // ATTN_STEPS == g)
        def _(q_ref=q_ref, k_ref=k_ref, v_ref=v_ref, o_ref=o_ref, lse_ref=lse_ref,
              chunks_per_subseq=chunks_per_subseq, gz_chunk=gz_chunk):
            first_chunk = s % chunks_per_subseq == 0

            @pl.when(first_chunk)
            def _():
                kc_ref[...] = jnp.zeros_like(kc_ref)
                vc_ref[...] = jnp.zeros_like(vc_ref)

            gz_chunk(0)
            finish0 = _attention_block(0, first_chunk, q_ref, k_ref, v_ref, o_ref, lse_ref, kc_ref, vc_ref)
            gz_chunk(1)
            finish0()
            gz_chunk(2)
            finish1 = _attention_block(1, first_chunk, q_ref, k_ref, v_ref, o_ref, lse_ref, kc_ref, vc_ref)
            gz_chunk(3)
            finish1()
            kc_ref[...] = k_ref[ATTN_BLOCK:2 * ATTN_BLOCK, :]
            vc_ref[...] = v_ref[ATTN_BLOCK:2 * ATTN_BLOCK, :]


def _gates_z_and_attention(x2, w_gz_bf, b_gate2, qkvs):
    m = x2.shape[0]
    tm = m // GZ_ROW_TILES
    rows = 2 * ATTN_BLOCK
    n_groups = len(qkvs)
    assert GZ_TILES * GZ_ROW_TILES == n_groups * ATTN_STEPS
    dilations = tuple(q.shape[2] for q in qkvs)

    def chunk_of(g, qkv):
        _, batch, dilation, sub_len, _ = qkv.shape
        chunks_per_subseq = sub_len // rows
        per_batch = dilation * chunks_per_subseq
        assert batch * per_batch == ATTN_STEPS and GZ_ROW_TILES % dilation == 0

        def index(step):
            a = jnp.clip(step - g * ATTN_STEPS, 0, ATTN_STEPS - 1)
            rc = a % per_batch
            return a // per_batch, rc // chunks_per_subseq, rc % chunks_per_subseq
        return index

    in_specs = [
        pl.BlockSpec((tm, D_MODEL), lambda st: (st % GZ_ROW_TILES, 0)),
        pl.BlockSpec((D_MODEL, COL_TILE), lambda st: (0, st // GZ_ROW_TILES)),
        pl.BlockSpec((1, COL_TILE), lambda st: (0, jnp.minimum(st // GZ_ROW_TILES, GZ_TILE_ZATTN - 1))),
    ]
    out_specs = [pl.BlockSpec((tm, COL_TILE), lambda st: (st % GZ_ROW_TILES, st // GZ_ROW_TILES))]
    out_shape = [jax.ShapeDtypeStruct((m, GZ_TILES * COL_TILE), jnp.bfloat16)]
    operands = [x2, w_gz_bf, b_gate2]
    for g, qkv in enumerate(qkvs):
        _, batch, dilation, sub_len, _ = qkv.shape
        index = chunk_of(g, qkv)
        for t in range(3):
            in_specs.append(pl.BlockSpec((None, None, None, rows, ATTN_WIDTH),
                                         lambda st, t=t, index=index: (t, *index(st), 0)))
            operands.append(qkv)
        out_specs.append(pl.BlockSpec((None, None, rows, ATTN_WIDTH), lambda st, index=index: (*index(st), 0)))
        out_specs.append(pl.BlockSpec((None, None, rows, LANES), lambda st, index=index: (*index(st), 0)))
        out_shape.append(jax.ShapeDtypeStruct((batch, dilation, sub_len, ATTN_WIDTH), jnp.bfloat16))
        out_shape.append(jax.ShapeDtypeStruct((batch, dilation, sub_len, LANES), jnp.float32))
    results = pl.pallas_call(
        functools.partial(_gz_attn_kernel, dilations=dilations),
        grid=(GZ_TILES * GZ_ROW_TILES,),
        in_specs=in_specs,
        out_specs=out_specs,
        out_shape=out_shape,
        scratch_shapes=[
            pltpu.VMEM((ATTN_BLOCK, ATTN_WIDTH), jnp.bfloat16),
            pltpu.VMEM((ATTN_BLOCK, ATTN_WIDTH), jnp.bfloat16),
        ],
        compiler_params=pltpu.CompilerParams(
            dimension_semantics=("arbitrary",),
            vmem_limit_bytes=PROJ_VMEM_LIMIT),
        name="gates_z_attention",
    )(*operands)
    gz = results[0]
    os_ = [results[1 + 2 * g] for g in range(n_groups)]
    lses = [results[2 + 2 * g] for g in range(n_groups)]
    return gz, os_, lses


def _to_token_order(src_ref, dst_ref, slab, src_cols, mid_ref):
    dilation, sub = src_ref.shape[0], src_ref.shape[1]
    if dilation == SUBLANE_STRIDE:
        for r in range(dilation):
            dst_ref[slab, pl.ds(r, sub, stride=dilation), :] = src_ref[r, :, src_cols].astype(dst_ref.dtype)
        return
    assert dilation == SUBLANE_STRIDE * SUBLANE_STRIDE
    quarter = sub * SUBLANE_STRIDE
    for r in range(dilation):
        r1, r0 = divmod(r, SUBLANE_STRIDE)
        mid_ref[slab, pl.ds(r0 * quarter + r1, sub, stride=SUBLANE_STRIDE), :] = (
            src_ref[r, :, src_cols].astype(mid_ref.dtype))
    for r0 in range(SUBLANE_STRIDE):
        dst_ref[slab, pl.ds(r0, quarter, stride=SUBLANE_STRIDE), :] = mid_ref[slab, pl.ds(r0 * quarter, quarter), :]


def _combine_weights(l1_ref, l2_ref, l3_ref, ln2_ref, ln3_ref, lmid_ref):
    _to_token_order(l2_ref, ln2_ref, 0, slice(None), lmid_ref)
    _to_token_order(l3_ref, ln3_ref, 0, slice(None), lmid_ref)
    l1 = l1_ref[...]
    l2 = ln2_ref[0]
    l3 = ln3_ref[0]
    mx = jnp.maximum(jnp.maximum(l1, l2), l3)
    e1 = jnp.exp(l1 - mx)
    e2 = jnp.exp(l2 - mx)
    e3 = jnp.exp(l3 - mx)
    tot = e1 + e2 + e3
    return e1 / tot, e2 / tot, e3 / tot


def _combine_head(h, weights, o1_ref, o2_ref, o3_ref, za_ref, on2_ref, on3_ref, omid_ref, ya_ref):
    hs = slice(h * HEAD_DIM, (h + 1) * HEAD_DIM)
    _to_token_order(o2_ref, on2_ref, h, hs, omid_ref)
    _to_token_order(o3_ref, on3_ref, h, hs, omid_ref)
    w1, w2, w3 = weights
    o = (w1[:, h:h + 1] * o1_ref[:, hs].astype(jnp.float32)
         + w2[:, h:h + 1] * on2_ref[h]
         + w3[:, h:h + 1] * on3_ref[h])
    ya_ref[:, hs] = (o * za_ref[:, hs].astype(jnp.float32)).astype(ya_ref.dtype)


def _pool_window_sums(i, u_ref, halo_ref, ext_ref, s2_ref, s4_ref, s8_ref, *, tm):
    rows = HALO + tm
    ext_ref[0:HALO, :] = jnp.where(i > 0, halo_ref[...], 0.0)
    ext_ref[HALO:, :] = u_ref[...]
    g = POOL_GROUP
    s2_ref[8:rows, :] = ext_ref[8:rows, :] + ext_ref[7:rows - 1, :]
    s4_ref[16:rows, :] = s2_ref[16:rows, g:] + s2_ref[14:rows - 2, g:]
    s8_ref[24:rows, :] = s4_ref[24:rows, g:] + s4_ref[20:rows - 4, g:]


def _pool_group(k, i, zp_ref, w_pool_ref, pool_scale_ref, ext_ref, s2_ref, s4_ref, s8_ref, yp_ref, *, tm):
    rows = HALO + tm
    g = POOL_GROUP
    w = POOL_WINDOWS[k]
    if k < 3:
        window_sum = (s2_ref, s4_ref, s8_ref)[k][HALO:rows, 0:g]
    else:
        window_sum = s8_ref[HALO:rows, g:] + s8_ref[HALO - 8:rows - 8, g:]
    cs = slice(k * g, (k + 1) * g)
    pos = i * tm + lax.broadcasted_iota(jnp.int32, (tm, 1), 0)
    cnt = jnp.minimum(pos + 1, w).astype(jnp.float32)
    p = window_sum / cnt - ext_ref[HALO:rows, cs]
    y = jnp.dot(p.astype(jnp.bfloat16), w_pool_ref[k], preferred_element_type=jnp.float32)
    y = y * pool_scale_ref[:, cs] * zp_ref[:, cs].astype(jnp.float32)
    yp_ref[:, cs] = y.astype(yp_ref.dtype)


def _layer_norm_rows(r_ref, gamma_ref, beta_ref):
    r = r_ref[...]
    mu = jnp.mean(r, axis=-1, keepdims=True)
    rc = r - mu
    var = jnp.mean(rc * rc, axis=-1, keepdims=True)
    r_ref[...] = rc * lax.rsqrt(var + LN_EPS) * gamma_ref[...] + beta_ref[...]


def _merge_kernel(o1_ref, o2_ref, o3_ref, l1_ref, l2_ref, l3_ref, za_ref, zp_ref, u_ref, halo_ref,
                  g_attn_ref, g_pool_ref, x_ref, w_pool_ref, pool_scale_ref, wpa_ref, wpp_ref, w_out_ref,
                  gamma_ref, beta_ref, out_ref, ext_ref, s2_ref, s4_ref, s8_ref, on2_ref, on3_ref, omid_ref,
                  ln2_ref, ln3_ref, lmid_ref,
                  ya0_ref, yp0_ref, ya1_ref, yp1_ref, merged_ref, *, tm, n_tiles, tiles_per_seq):
    s = pl.program_id(0)
    i_prep = jnp.minimum(s, n_tiles - 1) % tiles_per_seq

    @pl.when(s == 0)
    def _():
        ya1_ref[...] = jnp.zeros_like(ya1_ref)
        yp1_ref[...] = jnp.zeros_like(yp1_ref)

    def step(ya_w, yp_w, ya_r, yp_r):
        chunks = [slice(c * MERGE_CHUNK, (c + 1) * MERGE_CHUNK) for c in range(D_MODEL // MERGE_CHUNK)]
        ya = ya_r[...]
        yp = yp_r[...]
        weights = _combine_weights(l1_ref, l2_ref, l3_ref, ln2_ref, ln3_ref, lmid_ref)
        _pool_window_sums(i_prep, u_ref, halo_ref, ext_ref, s2_ref, s4_ref, s8_ref, tm=tm)
        for c, cols in enumerate(chunks):
            pa = jnp.dot(ya, wpa_ref[:, cols], preferred_element_type=jnp.float32)
            pp = jnp.dot(yp, wpp_ref[:, cols], preferred_element_type=jnp.float32)
            merged_ref[:, cols] = (g_attn_ref[:, cols].astype(jnp.float32) * pa
                                   + g_pool_ref[:, cols].astype(jnp.float32) * pp).astype(merged_ref.dtype)
            _combine_head(c, weights, o1_ref, o2_ref, o3_ref, za_ref, on2_ref, on3_ref, omid_ref, ya_w)
            if c % 2 == 1:
                _pool_group(c // 2, i_prep, zp_ref, w_pool_ref, pool_scale_ref, ext_ref, s2_ref, s4_ref, s8_ref,
                            yp_w, tm=tm)
        merged = merged_ref[...]
        for cols in chunks:
            out = jnp.dot(merged, w_out_ref[:, cols], preferred_element_type=jnp.float32)
            out_ref[:, cols] = DEEPNORM_ALPHA * x_ref[:, cols] + out
        _layer_norm_rows(out_ref, gamma_ref, beta_ref)

    @pl.when(s % 2 == 0)
    def _():
        step(ya0_ref, yp0_ref, ya1_ref, yp1_ref)

    @pl.when(s % 2 == 1)
    def _():
        step(ya1_ref, yp1_ref, ya0_ref, yp0_ref)


def _merge(os_, lses, gz3, u3, x, w_pool_bf, pool_scale2, wpa_bf, wpp_bf, w_out_bf, gamma2, beta2, tm=256):
    b, s, _ = x.shape
    tiles_per_seq = s // tm
    n_tiles = b * tiles_per_seq

    def prep_tile(step):
        t = jnp.minimum(step, n_tiles - 1)
        return t // tiles_per_seq, t % tiles_per_seq

    def proj_tile(step):
        t = jnp.maximum(step - 1, 0)
        return t // tiles_per_seq, t % tiles_per_seq

    def tok(which, width, tile=0):
        return pl.BlockSpec((None, tm, width), lambda st: (*which(st), tile))

    def grouped(arr):
        dilation, width = arr.shape[1], arr.shape[3]

        def index(st):
            bb, i = prep_tile(st)
            return bb, 0, i, 0
        if dilation == 1:
            return pl.BlockSpec((None, None, tm, width), index)
        return pl.BlockSpec((None, dilation, tm // dilation, width), index)

    def halo_index(st):
        bb, i = prep_tile(st)
        return bb, jnp.maximum(i * (tm // HALO) - 1, 0), 0

    def const(shape):
        nd = len(shape)
        return pl.BlockSpec(shape, lambda st: (0,) * nd, pipeline_mode=pl.Buffered(1))

    in_specs = [
        grouped(os_[0]), grouped(os_[1]), grouped(os_[2]),
        grouped(lses[0]), grouped(lses[1]), grouped(lses[2]),
        tok(prep_tile, COL_TILE, GZ_TILE_ZATTN), tok(prep_tile, COL_TILE, GZ_TILE_ZPOOL),
        tok(prep_tile, POOL_WIDTH),
        pl.BlockSpec((None, HALO, POOL_WIDTH), halo_index),
        tok(proj_tile, D_MODEL, 0), tok(proj_tile, D_MODEL, 1),
        tok(proj_tile, D_MODEL),
        const((len(POOL_WINDOWS), POOL_GROUP, POOL_GROUP)),
        const((1, POOL_WIDTH)),
        const((ATTN_WIDTH, D_MODEL)),
        const((POOL_WIDTH, D_MODEL)),
        const((D_MODEL, D_MODEL)),
        const((1, D_MODEL)),
        const((1, D_MODEL)),
    ]
    f32 = jnp.float32
    return pl.pallas_call(
        functools.partial(_merge_kernel, tm=tm, n_tiles=n_tiles, tiles_per_seq=tiles_per_seq),
        grid=(n_tiles + 1,),
        in_specs=in_specs,
        out_specs=tok(proj_tile, D_MODEL),
        out_shape=jax.ShapeDtypeStruct((b, s, D_MODEL), f32),
        scratch_shapes=[
            pltpu.VMEM((HALO + tm, POOL_WIDTH), f32),
            pltpu.VMEM((HALO + tm, POOL_WIDTH), f32),
            pltpu.VMEM((HALO + tm, POOL_WIDTH - POOL_GROUP), f32),
            pltpu.VMEM((HALO + tm, POOL_WIDTH - 2 * POOL_GROUP), f32),
            pltpu.VMEM((N_HEADS, tm, HEAD_DIM), f32),
            pltpu.VMEM((N_HEADS, tm, HEAD_DIM), f32),
            pltpu.VMEM((N_HEADS, tm, HEAD_DIM), f32),
            pltpu.VMEM((1, tm, LANES), f32),
            pltpu.VMEM((1, tm, LANES), f32),
            pltpu.VMEM((1, tm, LANES), f32),
            pltpu.VMEM((tm, ATTN_WIDTH), jnp.bfloat16),
            pltpu.VMEM((tm, POOL_WIDTH), jnp.bfloat16),
            pltpu.VMEM((tm, ATTN_WIDTH), jnp.bfloat16),
            pltpu.VMEM((tm, POOL_WIDTH), jnp.bfloat16),
            pltpu.VMEM((tm, D_MODEL), jnp.bfloat16),
        ],
        compiler_params=pltpu.CompilerParams(
            dimension_semantics=("arbitrary",),
            vmem_limit_bytes=MERGE_VMEM_LIMIT),
        name="merge_out_ln",
    )(os_[0], os_[1], os_[2], lses[0], lses[1], lses[2], gz3, gz3, u3, u3, gz3, gz3, x,
      w_pool_bf, pool_scale2, wpa_bf, wpp_bf, w_out_bf, gamma2, beta2)


def _layer(x, w_in, b_gate, w_pool, pool_scale, w_proj_attn, w_proj_pool, w_out, ln_gamma, ln_beta):
    b, s, d = x.shape
    u2, x2 = _project_u(x.reshape(b * s, d), w_in)
    later_weights = (w_pool.reshape(-1, POOL_GROUP), w_proj_attn, w_proj_pool, w_out)
    last = N_GROUPS - 1
    qkvs = []
    for g, (window, dilation) in enumerate(DILATED_GROUPS):
        assert window // dilation == ATTN_BLOCK
        qkv, rounded = _project_qkv(x2, w_in, b, g, dilation, later_weights if g == 0 else (),
                                    round_gz_weights=(g == last))
        qkvs.append(qkv)
        if g == 0:
            w_pool_bf, wpa_bf, wpp_bf, w_out_bf = rounded
        if g == last:
            (w_gz_bf,) = rounded
    gz2, os_, lses = _gates_z_and_attention(x2, w_gz_bf, b_gate.reshape(1, -1), qkvs)
    gz3 = gz2.reshape(b, s, GZ_TILES * COL_TILE)
    u3 = u2.reshape(b, s, POOL_WIDTH)
    return _merge(os_, lses, gz3, u3, x, w_pool_bf.reshape(w_pool.shape), pool_scale.reshape(1, -1),
                  wpa_bf, wpp_bf, w_out_bf, ln_gamma.reshape(1, -1), ln_beta.reshape(1, -1))


def kernel(x, w_in, b_gate, w_pool, pool_scale, w_proj_attn, w_proj_pool, w_out, ln_gamma, ln_beta):
    depth = w_in.shape[0]
    for layer in range(depth):
        x = _layer(x, w_in[layer], b_gate[layer], w_pool[layer], pool_scale[layer],
                   w_proj_attn[layer], w_proj_pool[layer], w_out[layer],
                   ln_gamma[layer], ln_beta[layer])
    return x
```

```python
import functools

import jax
import jax.numpy as jnp
from jax import lax
from jax.experimental import pallas as pl
from jax.experimental.pallas import tpu as pltpu

D_MODEL = 2048
HEAD_DIM = 128
N_HEADS = 8
ATTN_WIDTH = N_HEADS * HEAD_DIM
DILATED_GROUPS = ((128, 1), (512, 4), (2048, 16))
N_GROUPS = len(DILATED_GROUPS)
POOL_WINDOWS = (2, 4, 8, 16)
POOL_WIDTH = D_MODEL // 2
POOL_GROUP = POOL_WIDTH // len(POOL_WINDOWS)
DEEPNORM_ALPHA = 2.0 ** 0.25
LN_EPS = 1e-5
NEG_INF = -1e30
LOG2_E = 1.4426950408889634

LANES = 128
MIB = 1024 * 1024
SUBLANE_STRIDE = 4
VMEM_BYTES_V7X = 64 * MIB
PROJ_VMEM_LIMIT = VMEM_BYTES_V7X - 10 * MIB
MERGE_VMEM_LIMIT = VMEM_BYTES_V7X - 8 * MIB

COL_TILE = 1024
W_TILE_ZATTN = 9
W_TILE_U = 10
W_TILE_ZPOOL = 11
W_TILE_GATES = 12
GZ_TILES = 6
GZ_TILE_ZATTN = 4
GZ_TILE_ZPOOL = 5
GZ_ROW_TILES = 16
ATTN_STEPS = 32

ATTN_BLOCK = 128
MERGE_CHUNK = 256
HALO = 32


def _cast_weight_tile(first_step, w_ref, wbf_ref):
    @pl.when(first_step)
    def _():
        wbf_ref[...] = w_ref[...].astype(wbf_ref.dtype)


def _proj_qkv_kernel(x_ref, w_ref, *refs, dilation, n_side, n_stream):
    n_extra = n_side + n_stream
    extra_in, o_ref, extra_out = refs[:n_extra], refs[n_extra], refs[n_extra + 1:2 * n_extra + 1]
    wbf_ref, scratch = refs[2 * n_extra + 1], refs[2 * n_extra + 2:]
    _cast_weight_tile(jnp.logical_and(pl.program_id(1) == 0, pl.program_id(2) == 0), w_ref, wbf_ref)
    if n_side:
        @pl.when(pl.program_id(0) == 0)
        def _():
            for src, dst in zip(extra_in[:n_side], extra_out[:n_side]):
                dst[...] = src[...].astype(dst.dtype)
    for src, dst in zip(extra_in[n_side:], extra_out[n_side:]):
        dst[...] = src[...].astype(dst.dtype)

    acc = jnp.dot(x_ref[...], wbf_ref[...], preferred_element_type=jnp.float32)
    if dilation == 1:
        o_ref[0] = acc.astype(o_ref.dtype)
        return
    acc_ref = scratch[0]
    slabs, tm = acc_ref.shape[0], acc_ref.shape[1]
    for c in range(slabs):
        acc_ref[c] = acc[:, c * LANES:(c + 1) * LANES]
    if dilation == SUBLANE_STRIDE:
        for r in range(dilation):
            for c in range(slabs):
                o_ref[r, :, c * LANES:(c + 1) * LANES] = (
                    acc_ref[c, pl.ds(r, tm // dilation, stride=dilation), :].astype(o_ref.dtype))
        return
    assert dilation == SUBLANE_STRIDE * SUBLANE_STRIDE
    mid_ref = scratch[1]
    quarter = tm // SUBLANE_STRIDE
    for r0 in range(SUBLANE_STRIDE):
        for c in range(slabs):
            mid_ref[c, pl.ds(r0 * quarter, quarter), :] = acc_ref[c, pl.ds(r0, quarter, stride=SUBLANE_STRIDE), :]
    for r0 in range(SUBLANE_STRIDE):
        for r1 in range(SUBLANE_STRIDE):
            for c in range(slabs):
                o_ref[SUBLANE_STRIDE * r1 + r0, :, c * LANES:(c + 1) * LANES] = (
                    mid_ref[c, pl.ds(r0 * quarter + r1, tm // dilation, stride=SUBLANE_STRIDE), :]
                    .astype(o_ref.dtype))


def _proj_u_kernel(x_ref, w_ref, u_ref, xbf_ref, wbf_ref):
    _cast_weight_tile(pl.program_id(0) == 0, w_ref, wbf_ref)
    xb = x_ref[...].astype(xbf_ref.dtype)
    xbf_ref[...] = xb
    u_ref[...] = jnp.dot(xb, wbf_ref[...], preferred_element_type=jnp.float32)


def _project_u(x2, w_in, tm=512):
    m = x2.shape[0]
    return pl.pallas_call(
        _proj_u_kernel,
        grid=(m // tm,),
        in_specs=[
            pl.BlockSpec((tm, D_MODEL), lambda i: (i, 0)),
            pl.BlockSpec((D_MODEL, COL_TILE), lambda i: (0, W_TILE_U), pipeline_mode=pl.Buffered(1)),
        ],
        out_specs=[
            pl.BlockSpec((tm, COL_TILE), lambda i: (i, 0)),
            pl.BlockSpec((tm, D_MODEL), lambda i: (i, 0)),
        ],
        out_shape=[
            jax.ShapeDtypeStruct((m, POOL_WIDTH), jnp.float32),
            jax.ShapeDtypeStruct((m, D_MODEL), jnp.bfloat16),
        ],
        scratch_shapes=[pltpu.VMEM((D_MODEL, COL_TILE), jnp.bfloat16)],
        compiler_params=pltpu.CompilerParams(
            dimension_semantics=("arbitrary",),
            vmem_limit_bytes=PROJ_VMEM_LIMIT),
        name="proj_u",
    )(x2, w_in)


def _gz_w_tile(j):
    return jnp.where(j < GZ_TILE_ZATTN, W_TILE_GATES + j, jnp.where(j == GZ_TILE_ZATTN, W_TILE_ZATTN, W_TILE_ZPOOL))


def _project_qkv(x2, w_in, batch, group, dilation, side_weights=(), round_gz_weights=False, tm=1024):
    m = x2.shape[0]
    tiles_per_batch = m // batch // tm
    row_tiles = batch * tiles_per_batch
    sub = tm // dilation
    n_row_scratch = {1: 0, SUBLANE_STRIDE: 1, SUBLANE_STRIDE * SUBLANE_STRIDE: 2}[dilation]

    def side_index(j, b, i):
        return jnp.where(j == 0, b * tiles_per_batch + i, row_tiles - 1), 0

    in_extra = [pl.BlockSpec((w.shape[0] // row_tiles, w.shape[1]), side_index) for w in side_weights]
    out_extra = list(in_extra)
    extra_shapes = [jax.ShapeDtypeStruct(w.shape, jnp.bfloat16) for w in side_weights]
    extra_operands = list(side_weights)
    if round_gz_weights:
        blocks_per_tile = 3 * row_tiles // GZ_TILES
        assert blocks_per_tile * GZ_TILES == 3 * row_tiles

        def step(j, b, i):
            return (j * batch + b) * tiles_per_batch + i

        in_extra.append(pl.BlockSpec(
            (D_MODEL // blocks_per_tile, COL_TILE),
            lambda j, b, i: (step(j, b, i) % blocks_per_tile, _gz_w_tile(step(j, b, i) // blocks_per_tile))))
        out_extra.append(pl.BlockSpec(
            (D_MODEL // blocks_per_tile, COL_TILE),
            lambda j, b, i: (step(j, b, i) % blocks_per_tile, step(j, b, i) // blocks_per_tile)))
        extra_shapes.append(jax.ShapeDtypeStruct((D_MODEL, GZ_TILES * COL_TILE), jnp.bfloat16))
        extra_operands.append(w_in)
    results = pl.pallas_call(
        functools.partial(_proj_qkv_kernel, dilation=dilation, n_side=len(side_weights),
                          n_stream=int(round_gz_weights)),
        grid=(3, batch, tiles_per_batch),
        in_specs=[
            pl.BlockSpec((tm, D_MODEL), lambda j, b, i: (b * tiles_per_batch + i, 0)),
            pl.BlockSpec((D_MODEL, COL_TILE), lambda j, b, i: (0, N_GROUPS * j + group)),
        ] + in_extra,
        out_specs=[pl.BlockSpec((None, None, dilation, sub, COL_TILE), lambda j, b, i: (j, b, 0, i, 0))]
        + out_extra,
        out_shape=[jax.ShapeDtypeStruct((3, batch, dilation, m // batch // dilation, ATTN_WIDTH), jnp.bfloat16)]
        + extra_shapes,
        scratch_shapes=[pltpu.VMEM((D_MODEL, COL_TILE), jnp.bfloat16)]
        + [pltpu.VMEM((COL_TILE // LANES, tm, LANES), jnp.float32)] * n_row_scratch,
        compiler_params=pltpu.CompilerParams(
            dimension_semantics=("arbitrary", "arbitrary", "arbitrary"),
            vmem_limit_bytes=PROJ_VMEM_LIMIT),
        name=f"proj_qkv_g{group}",
    )(x2, w_in, *extra_operands)
    return results[0], results[1:]


def _attention_block(blk, first_chunk, q_ref, k_ref, v_ref, o_ref, lse_ref, kc_ref, vc_ref):
    bq = ATTN_BLOCK
    scale = HEAD_DIM ** -0.5
    row = lax.broadcasted_iota(jnp.int32, (bq, 2 * bq), 0)
    col = lax.broadcasted_iota(jnp.int32, (bq, 2 * bq), 1)
    dist = bq + row - col
    mask = jnp.logical_and(dist >= 0, dist <= bq)
    r0 = blk * bq
    if blk == 0:
        mask = jnp.logical_and(mask, col >= jnp.where(first_chunk, bq, 0))

        def kcat_of(hs):
            return jnp.concatenate([kc_ref[:, hs], k_ref[0:bq, hs]], axis=0)

        def vcat_of(hs):
            return jnp.concatenate([vc_ref[:, hs], v_ref[0:bq, hs]], axis=0)
    else:
        def kcat_of(hs):
            return k_ref[r0 - bq:r0 + bq, hs]

        def vcat_of(hs):
            return v_ref[r0 - bq:r0 + bq, hs]

    heads = [slice(h * HEAD_DIM, (h + 1) * HEAD_DIM) for h in range(N_HEADS)]
    ss = [jnp.where(mask,
                    jnp.dot(q_ref[r0:r0 + bq, hs], kcat_of(hs).astype(jnp.float32).T.astype(jnp.bfloat16),
                            preferred_element_type=jnp.float32),
                    NEG_INF) for hs in heads]
    ms = [jnp.max(s, axis=-1, keepdims=True) for s in ss]

    def finish():
        es = [jnp.exp2((s - m) * (scale * LOG2_E)) for s, m in zip(ss, ms)]
        dens = [jnp.sum(e, axis=-1, keepdims=True) for e in es]
        outs = [jnp.dot(e.astype(o_ref.dtype), vcat_of(hs), preferred_element_type=jnp.float32)
                for e, hs in zip(es, heads)]
        lane = lax.broadcasted_iota(jnp.int32, (bq, LANES), 1)
        lse_tile = jnp.zeros((bq, LANES), jnp.float32)
        for h, hs in enumerate(heads):
            o_ref[r0:r0 + bq, hs] = (outs[h] / dens[h]).astype(o_ref.dtype)
            lse_tile = jnp.where(lane == h, ms[h] * scale + jnp.log(dens[h]), lse_tile)
        lse_ref[r0:r0 + bq, :] = lse_tile
    return finish


def _gz_attn_kernel(x_ref, w_ref, b_ref, *refs, dilations):
    n_groups = len(dilations)
    qkv_refs = refs[:3 * n_groups]
    gz_ref = refs[3 * n_groups]
    attn_out_refs = refs[3 * n_groups + 1:5 * n_groups + 1]
    kc_ref, vc_ref = refs[5 * n_groups + 1:]
    s = pl.program_id(0)
    chunk_cols = COL_TILE // 4

    for g, dilation in enumerate(dilations):
        q_ref, k_ref, v_ref = qkv_refs[3 * g:3 * g + 3]
        o_ref, lse_ref = attn_out_refs[2 * g:2 * g + 2]
        chunks_per_subseq = GZ_ROW_TILES // dilation
        is_gate = 2 * g + 1 < GZ_TILE_ZATTN

        def gz_chunk(c, is_gate=is_gate):
            cols = slice(c * chunk_cols, (c + 1) * chunk_cols)
            acc = jnp.dot(x_ref[...], w_ref[:, cols], preferred_element_type=jnp.float32)
            t = acc + b_ref[:, cols] if is_gate else acc
            sg = 0.5 * jnp.tanh(0.5 * t) + 0.5
            gz_ref[:, cols] = (sg if is_gate else sg * t).astype(gz_ref.dtype)

        @pl.when(s // ATTN_STEPS == g)
        def _(q_ref=q_ref, k_ref=k_ref, v_ref=v_ref, o_ref=o_ref, lse_ref=lse_ref,
              chunks_per_subseq=chunks_per_subseq, gz_chunk=gz_chunk):
            first_chunk = s % chunks_per_subseq == 0

            @pl.when(first_chunk)
            def _():
                kc_ref[...] = jnp.zeros_like(kc_ref)
                vc_ref[...] = jnp.zeros_like(vc_ref)

            gz_chunk(0)
            finish0 = _attention_block(0, first_chunk, q_ref, k_ref, v_ref, o_ref, lse_ref, kc_ref, vc_ref)
            gz_chunk(1)
            finish0()
            gz_chunk(2)
            finish1 = _attention_block(1, first_chunk, q_ref, k_ref, v_ref, o_ref, lse_ref, kc_ref, vc_ref)
            gz_chunk(3)
            finish1()
            kc_ref[...] = k_ref[ATTN_BLOCK:2 * ATTN_BLOCK, :]
            vc_ref[...] = v_ref[ATTN_BLOCK:2 * ATTN_BLOCK, :]


def _gates_z_and_attention(x2, w_gz_bf, b_gate2, qkvs):
    m = x2.shape[0]
    tm = m // GZ_ROW_TILES
    rows = 2 * ATTN_BLOCK
    n_groups = len(qkvs)
    assert GZ_TILES * GZ_ROW_TILES == n_groups * ATTN_STEPS
    dilations = tuple(q.shape[2] for q in qkvs)

    def chunk_of(g, qkv):
        _, batch, dilation, sub_len, _ = qkv.shape
        chunks_per_subseq = sub_len // rows
        per_batch = dilation * chunks_per_subseq
        assert batch * per_batch == ATTN_STEPS and GZ_ROW_TILES % dilation == 0

        def index(step):
            a = jnp.clip(step - g * ATTN_STEPS, 0, ATTN_STEPS - 1)
            rc = a % per_batch
            return a // per_batch, rc // chunks_per_subseq, rc % chunks_per_subseq
        return index

    in_specs = [
        pl.BlockSpec((tm, D_MODEL), lambda st: (st % GZ_ROW_TILES, 0)),
        pl.BlockSpec((D_MODEL, COL_TILE), lambda st: (0, st // GZ_ROW_TILES)),
        pl.BlockSpec((1, COL_TILE), lambda st: (0, jnp.minimum(st // GZ_ROW_TILES, GZ_TILE_ZATTN - 1))),
    ]
    out_specs = [pl.BlockSpec((tm, COL_TILE), lambda st: (st % GZ_ROW_TILES, st // GZ_ROW_TILES))]
    out_shape = [jax.ShapeDtypeStruct((m, GZ_TILES * COL_TILE), jnp.bfloat16)]
    operands = [x2, w_gz_bf, b_gate2]
    for g, qkv in enumerate(qkvs):
        _, batch, dilation, sub_len, _ = qkv.shape
        index = chunk_of(g, qkv)
        for t in range(3):
            in_specs.append(pl.BlockSpec((None, None, None, rows, ATTN_WIDTH),
                                         lambda st, t=t, index=index: (t, *index(st), 0)))
            operands.append(qkv)
        out_specs.append(pl.BlockSpec((None, None, rows, ATTN_WIDTH), lambda st, index=index: (*index(st), 0)))
        out_specs.append(pl.BlockSpec((None, None, rows, LANES), lambda st, index=index: (*index(st), 0)))
        out_shape.append(jax.ShapeDtypeStruct((batch, dilation, sub_len, ATTN_WIDTH), jnp.bfloat16))
        out_shape.append(jax.ShapeDtypeStruct((batch, dilation, sub_len, LANES), jnp.float32))
    results = pl.pallas_call(
        functools.partial(_gz_attn_kernel, dilations=dilations),
        grid=(GZ_TILES * GZ_ROW_TILES,),
        in_specs=in_specs,
        out_specs=out_specs,
        out_shape=out_shape,
        scratch_shapes=[
            pltpu.VMEM((ATTN_BLOCK, ATTN_WIDTH), jnp.bfloat16),
            pltpu.VMEM((ATTN_BLOCK, ATTN_WIDTH), jnp.bfloat16),
        ],
        compiler_params=pltpu.CompilerParams(
            dimension_semantics=("arbitrary",),
            vmem_limit_bytes=PROJ_VMEM_LIMIT),
        name="gates_z_attention",
    )(*operands)
    gz = results[0]
    os_ = [results[1 + 2 * g] for g in range(n_groups)]
    lses = [results[2 + 2 * g] for g in range(n_groups)]
    return gz, os_, lses


def _to_token_order(src_ref, dst_ref, slab, src_cols, mid_ref):
    dilation, sub = src_ref.shape[0], src_ref.shape[1]
    if dilation == SUBLANE_STRIDE:
        for r in range(dilation):
            dst_ref[slab, pl.ds(r, sub, stride=dilation), :] = src_ref[r, :, src_cols].astype(dst_ref.dtype)
        return
    assert dilation == SUBLANE_STRIDE * SUBLANE_STRIDE
    quarter = sub * SUBLANE_STRIDE
    for r in range(dilation):
        r1, r0 = divmod(r, SUBLANE_STRIDE)
        mid_ref[slab, pl.ds(r0 * quarter + r1, sub, stride=SUBLANE_STRIDE), :] = (
            src_ref[r, :, src_cols].astype(mid_ref.dtype))
    for r0 in range(SUBLANE_STRIDE):
        dst_ref[slab, pl.ds(r0, quarter, stride=SUBLANE_STRIDE), :] = mid_ref[slab, pl.ds(r0 * quarter, quarter), :]


def _combine_weights(l1_ref, l2_ref, l3_ref, ln2_ref, ln3_ref, lmid_ref):
    _to_token_order(l2_ref, ln2_ref, 0, slice(None), lmid_ref)
    _to_token_order(l3_ref, ln3_ref, 0, slice(None), lmid_ref)
    l1 = l1_ref[...]
    l2 = ln2_ref[0]
    l3 = ln3_ref[0]
    mx = jnp.maximum(jnp.maximum(l1, l2), l3)
    e1 = jnp.exp(l1 - mx)
    e2 = jnp.exp(l2 - mx)
    e3 = jnp.exp(l3 - mx)
    tot = e1 + e2 + e3
    return e1 / tot, e2 / tot, e3 / tot


def _combine_head(h, weights, o1_ref, o2_ref, o3_ref, za_ref, on2_ref, on3_ref, omid_ref, ya_ref):
    hs = slice(h * HEAD_DIM, (h + 1) * HEAD_DIM)
    _to_token_order(o2_ref, on2_ref, h, hs, omid_ref)
    _to_token_order(o3_ref, on3_ref, h, hs, omid_ref)
    w1, w2, w3 = weights
    o = (w1[:, h:h + 1] * o1_ref[:, hs].astype(jnp.float32)
         + w2[:, h:h + 1] * on2_ref[h]
         + w3[:, h:h + 1] * on3_ref[h])
    ya_ref[:, hs] = (o * za_ref[:, hs].astype(jnp.float32)).astype(ya_ref.dtype)


def _pool_window_sums(i, u_ref, halo_ref, ext_ref, s2_ref, s4_ref, s8_ref, *, tm):
    rows = HALO + tm
    ext_ref[0:HALO, :] = jnp.where(i > 0, halo_ref[...], 0.0)
    ext_ref[HALO:, :] = u_ref[...]
    g = POOL_GROUP
    s2_ref[8:rows, :] = ext_ref[8:rows, :] + ext_ref[7:rows - 1, :]
    s4_ref[16:rows, :] = s2_ref[16:rows, g:] + s2_ref[14:rows - 2, g:]
    s8_ref[24:rows, :] = s4_ref[24:rows, g:] + s4_ref[20:rows - 4, g:]


def _pool_group(k, i, zp_ref, w_pool_ref, pool_scale_ref, ext_ref, s2_ref, s4_ref, s8_ref, yp_ref, *, tm):
    rows = HALO + tm
    g = POOL_GROUP
    w = POOL_WINDOWS[k]
    if k < 3:
        window_sum = (s2_ref, s4_ref, s8_ref)[k][HALO:rows, 0:g]
    else:
        window_sum = s8_ref[HALO:rows, g:] + s8_ref[HALO - 8:rows - 8, g:]
    cs = slice(k * g, (k + 1) * g)
    pos = i * tm + lax.broadcasted_iota(jnp.int32, (tm, 1), 0)
    cnt = jnp.minimum(pos + 1, w).astype(jnp.float32)
    p = window_sum / cnt - ext_ref[HALO:rows, cs]
    y = jnp.dot(p.astype(jnp.bfloat16), w_pool_ref[k], preferred_element_type=jnp.float32)
    y = y * pool_scale_ref[:, cs] * zp_ref[:, cs].astype(jnp.float32)
    yp_ref[:, cs] = y.astype(yp_ref.dtype)


def _layer_norm_rows(r_ref, gamma_ref, beta_ref):
    r = r_ref[...]
    mu = jnp.mean(r, axis=-1, keepdims=True)
    rc = r - mu
    var = jnp.mean(rc * rc, axis=-1, keepdims=True)
    r_ref[...] = rc * lax.rsqrt(var + LN_EPS) * gamma_ref[...] + beta_ref[...]


def _merge_kernel(o1_ref, o2_ref, o3_ref, l1_ref, l2_ref, l3_ref, za_ref, zp_ref, u_ref, halo_ref,
                  g_attn_ref, g_pool_ref, x_ref, w_pool_ref, pool_scale_ref, wpa_ref, wpp_ref, w_out_ref,
                  gamma_ref, beta_ref, out_ref, ext_ref, s2_ref, s4_ref, s8_ref, on2_ref, on3_ref, omid_ref,
                  ln2_ref, ln3_ref, lmid_ref,
                  ya0_ref, yp0_ref, ya1_ref, yp1_ref, merged_ref, *, tm, n_tiles, tiles_per_seq):
    s = pl.program_id(0)
    i_prep = jnp.minimum(s, n_tiles - 1) % tiles_per_seq

    @pl.when(s == 0)
    def _():
        ya1_ref[...] = jnp.zeros_like(ya1_ref)
        yp1_ref[...] = jnp.zeros_like(yp1_ref)

    def step(ya_w, yp_w, ya_r, yp_r):
        chunks = [slice(c * MERGE_CHUNK, (c + 1) * MERGE_CHUNK) for c in range(D_MODEL // MERGE_CHUNK)]
        ya = ya_r[...]
        yp = yp_r[...]
        weights = _combine_weights(l1_ref, l2_ref, l3_ref, ln2_ref, ln3_ref, lmid_ref)
        _pool_window_sums(i_prep, u_ref, halo_ref, ext_ref, s2_ref, s4_ref, s8_ref, tm=tm)
        for c, cols in enumerate(chunks):
            pa = jnp.dot(ya, wpa_ref[:, cols], preferred_element_type=jnp.float32)
            pp = jnp.dot(yp, wpp_ref[:, cols], preferred_element_type=jnp.float32)
            merged_ref[:, cols] = (g_attn_ref[:, cols].astype(jnp.float32) * pa
                                   + g_pool_ref[:, cols].astype(jnp.float32) * pp).astype(merged_ref.dtype)
            _combine_head(c, weights, o1_ref, o2_ref, o3_ref, za_ref, on2_ref, on3_ref, omid_ref, ya_w)
            if c % 2 == 1:
                _pool_group(c // 2, i_prep, zp_ref, w_pool_ref, pool_scale_ref, ext_ref, s2_ref, s4_ref, s8_ref,
                            yp_w, tm=tm)
        merged = merged_ref[...]
        for cols in chunks:
            out = jnp.dot(merged, w_out_ref[:, cols], preferred_element_type=jnp.float32)
            out_ref[:, cols] = DEEPNORM_ALPHA * x_ref[:, cols] + out
        _layer_norm_rows(out_ref, gamma_ref, beta_ref)

    @pl.when(s % 2 == 0)
    def _():
        step(ya0_ref, yp0_ref, ya1_ref, yp1_ref)

    @pl.when(s % 2 == 1)
    def _():
        step(ya1_ref, yp1_ref, ya0_ref, yp0_ref)


def _merge(os_, lses, gz3, u3, x, w_pool_bf, pool_scale2, wpa_bf, wpp_bf, w_out_bf, gamma2, beta2, tm=256):
    b, s, _ = x.shape
    tiles_per_seq = s // tm
    n_tiles = b * tiles_per_seq

    def prep_tile(step):
        t = jnp.minimum(step, n_tiles - 1)
        return t // tiles_per_seq, t % tiles_per_seq

    def proj_tile(step):
        t = jnp.maximum(step - 1, 0)
        return t // tiles_per_seq, t % tiles_per_seq

    def tok(which, width, tile=0):
        return pl.BlockSpec((None, tm, width), lambda st: (*which(st), tile))

    def grouped(arr):
        dilation, width = arr.shape[1], arr.shape[3]

        def index(st):
            bb, i = prep_tile(st)
            return bb, 0, i, 0
        if dilation == 1:
            return pl.BlockSpec((None, None, tm, width), index)
        return pl.BlockSpec((None, dilation, tm // dilation, width), index)

    def halo_index(st):
        bb, i = prep_tile(st)
        return bb, jnp.maximum(i * (tm // HALO) - 1, 0), 0

    def const(shape):
        nd = len(shape)
        return pl.BlockSpec(shape, lambda st: (0,) * nd, pipeline_mode=pl.Buffered(1))

    in_specs = [
        grouped(os_[0]), grouped(os_[1]), grouped(os_[2]),
        grouped(lses[0]), grouped(lses[1]), grouped(lses[2]),
        tok(prep_tile, COL_TILE, GZ_TILE_ZATTN), tok(prep_tile, COL_TILE, GZ_TILE_ZPOOL),
        tok(prep_tile, POOL_WIDTH),
        pl.BlockSpec((None, HALO, POOL_WIDTH), halo_index),
        tok(proj_tile, D_MODEL, 0), tok(proj_tile, D_MODEL, 1),
        tok(proj_tile, D_MODEL),
        const((len(POOL_WINDOWS), POOL_GROUP, POOL_GROUP)),
        const((1, POOL_WIDTH)),
        const((ATTN_WIDTH, D_MODEL)),
        const((POOL_WIDTH, D_MODEL)),
        const((D_MODEL, D_MODEL)),
        const((1, D_MODEL)),
        const((1, D_MODEL)),
    ]
    f32 = jnp.float32
    return pl.pallas_call(
        functools.partial(_merge_kernel, tm=tm, n_tiles=n_tiles, tiles_per_seq=tiles_per_seq),
        grid=(n_tiles + 1,),
        in_specs=in_specs,
        out_specs=tok(proj_tile, D_MODEL),
        out_shape=jax.ShapeDtypeStruct((b, s, D_MODEL), f32),
        scratch_shapes=[
            pltpu.VMEM((HALO + tm, POOL_WIDTH), f32),
            pltpu.VMEM((HALO + tm, POOL_WIDTH), f32),
            pltpu.VMEM((HALO + tm, POOL_WIDTH - POOL_GROUP), f32),
            pltpu.VMEM((HALO + tm, POOL_WIDTH - 2 * POOL_GROUP), f32),
            pltpu.VMEM((N_HEADS, tm, HEAD_DIM), f32),
            pltpu.VMEM((N_HEADS, tm, HEAD_DIM), f32),
            pltpu.VMEM((N_HEADS, tm, HEAD_DIM), f32),
            pltpu.VMEM((1, tm, LANES), f32),
            pltpu.VMEM((1, tm, LANES), f32),
            pltpu.VMEM((1, tm, LANES), f32),
            pltpu.VMEM((tm, ATTN_WIDTH), jnp.bfloat16),
            pltpu.VMEM((tm, POOL_WIDTH), jnp.bfloat16),
            pltpu.VMEM((tm, ATTN_WIDTH), jnp.bfloat16),
            pltpu.VMEM((tm, POOL_WIDTH), jnp.bfloat16),
            pltpu.VMEM((tm, D_MODEL), jnp.bfloat16),
        ],
        compiler_params=pltpu.CompilerParams(
            dimension_semantics=("arbitrary",),
            vmem_limit_bytes=MERGE_VMEM_LIMIT),
        name="merge_out_ln",
    )(os_[0], os_[1], os_[2], lses[0], lses[1], lses[2], gz3, gz3, u3, u3, gz3, gz3, x,
      w_pool_bf, pool_scale2, wpa_bf, wpp_bf, w_out_bf, gamma2, beta2)


def _layer(x, w_in, b_gate, w_pool, pool_scale, w_proj_attn, w_proj_pool, w_out, ln_gamma, ln_beta):
    b, s, d = x.shape
    u2, x2 = _project_u(x.reshape(b * s, d), w_in)
    later_weights = (w_pool.reshape(-1, POOL_GROUP), w_proj_attn, w_proj_pool, w_out)
    last = N_GROUPS - 1
    qkvs = []
    for g, (window, dilation) in enumerate(DILATED_GROUPS):
        assert window // dilation == ATTN_BLOCK
        qkv, rounded = _project_qkv(x2, w_in, b, g, dilation, later_weights if g == 0 else (),
                                    round_gz_weights=(g == last))
        qkvs.append(qkv)
        if g == 0:
            w_pool_bf, wpa_bf, wpp_bf, w_out_bf = rounded
        if g == last:
            (w_gz_bf,) = rounded
    gz2, os_, lses = _gates_z_and_attention(x2, w_gz_bf, b_gate.reshape(1, -1), qkvs)
    gz3 = gz2.reshape(b, s, GZ_TILES * COL_TILE)
    u3 = u2.reshape(b, s, POOL_WIDTH)
    return _merge(os_, lses, gz3, u3, x, w_pool_bf.reshape(w_pool.shape), pool_scale.reshape(1, -1),
                  wpa_bf, wpp_bf, w_out_bf, ln_gamma.reshape(1, -1), ln_beta.reshape(1, -1))


def kernel(x, w_in, b_gate, w_pool, pool_scale, w_proj_attn, w_proj_pool, w_out, ln_gamma, ln_beta):
    depth = w_in.shape[0]
    for layer in range(depth):
        x = _layer(x, w_in[layer], b_gate[layer], w_pool[layer], pool_scale[layer],
                   w_proj_attn[layer], w_proj_pool[layer], w_out[layer],
                   ln_gamma[layer], ln_beta[layer])
    return x
```

```python
import functools

import jax
import jax.numpy as jnp
from jax import lax
from jax.experimental import pallas as pl
from jax.experimental.pallas import tpu as pltpu

D_MODEL = 2048
HEAD_DIM = 128
N_HEADS = 8
ATTN_WIDTH = N_HEADS * HEAD_DIM
DILATED_GROUPS = ((128, 1), (512, 4), (2048, 16))
N_GROUPS = len(DILATED_GROUPS)
POOL_WINDOWS = (2, 4, 8, 16)
POOL_WIDTH = D_MODEL // 2
POOL_GROUP = POOL_WIDTH // len(POOL_WINDOWS)
DEEPNORM_ALPHA = 2.0 ** 0.25
LN_EPS = 1e-5
NEG_INF = -1e30
LOG2_E = 1.4426950408889634

LANES = 128
MIB = 1024 * 1024
SUBLANE_STRIDE = 4
VMEM_BYTES_V7X = 64 * MIB
PROJ_VMEM_LIMIT = VMEM_BYTES_V7X - 10 * MIB
MERGE_VMEM_LIMIT = VMEM_BYTES_V7X - 8 * MIB
COL_TILE = 1024
W_TILE_ZATTN = 9
W_TILE_U = 10
W_TILE_ZPOOL = 11
W_TILE_GATES = 12
GZ_TILES = 6
GZ_TILE_ZATTN = 4
GZ_TILE_ZPOOL = 5
GZ_ROW_TILES = 16
ATTN_STEPS = 32

ATTN_BLOCK = 128
MERGE_CHUNK = 256
HALO = 32


def _cast_weight_tile(first_step, w_ref, wbf_ref):
    @pl.when(first_step)
    def _():
        wbf_ref[...] = w_ref[...].astype(wbf_ref.dtype)


def _proj_qkv_kernel(x_ref, w_ref, *refs, dilation, n_side, n_stream):
    n_extra = n_side + n_stream
    extra_in, o_ref, extra_out = refs[:n_extra], refs[n_extra], refs[n_extra + 1:2 * n_extra + 1]
    wbf_ref, scratch = refs[2 * n_extra + 1], refs[2 * n_extra + 2:]
    _cast_weight_tile(jnp.logical_and(pl.program_id(1) == 0, pl.program_id(2) == 0), w_ref, wbf_ref)
    if n_side:
        @pl.when(pl.program_id(0) == 0)
        def _():
            for src, dst in zip(extra_in[:n_side], extra_out[:n_side]):
                dst[...] = src[...].astype(dst.dtype)
    for src, dst in zip(extra_in[n_side:], extra_out[n_side:]):
        dst[...] = src[...].astype(dst.dtype)

    acc = jnp.dot(x_ref[...], wbf_ref[...], preferred_element_type=jnp.float32)
    if dilation == 1:
        o_ref[0] = acc.astype(o_ref.dtype)
        return
    acc_ref = scratch[0]
    slabs, tm = acc_ref.shape[0], acc_ref.shape[1]
    for c in range(slabs):
        acc_ref[c] = acc[:, c * LANES:(c + 1) * LANES]
    if dilation == SUBLANE_STRIDE:
        for r in range(dilation):
            for c in range(slabs):
                o_ref[r, :, c * LANES:(c + 1) * LANES] = (
                    acc_ref[c, pl.ds(r, tm // dilation, stride=dilation), :].astype(o_ref.dtype))
        return
    assert dilation == SUBLANE_STRIDE * SUBLANE_STRIDE
    mid_ref = scratch[1]
    quarter = tm // SUBLANE_STRIDE
    for r0 in range(SUBLANE_STRIDE):
        for c in range(slabs):
            mid_ref[c, pl.ds(r0 * quarter, quarter), :] = acc_ref[c, pl.ds(r0, quarter, stride=SUBLANE_STRIDE), :]
    for r0 in range(SUBLANE_STRIDE):
        for r1 in range(SUBLANE_STRIDE):
            for c in range(slabs):
                o_ref[SUBLANE_STRIDE * r1 + r0, :, c * LANES:(c + 1) * LANES] = (
                    mid_ref[c, pl.ds(r0 * quarter + r1, tm // dilation, stride=SUBLANE_STRIDE), :]
                    .astype(o_ref.dtype))


def _proj_u_kernel(x_ref, w_ref, u_ref, xbf_ref, wbf_ref):
    _cast_weight_tile(pl.program_id(0) == 0, w_ref, wbf_ref)
    xb = x_ref[...].astype(xbf_ref.dtype)
    xbf_ref[...] = xb
    u_ref[...] = jnp.dot(xb, wbf_ref[...], preferred_element_type=jnp.float32)


def _project_u(x2, w_in, tm=512):
    m = x2.shape[0]
    return pl.pallas_call(
        _proj_u_kernel,
        grid=(m // tm,),
        in_specs=[
            pl.BlockSpec((tm, D_MODEL), lambda i: (i, 0)),
            pl.BlockSpec((D_MODEL, COL_TILE), lambda i: (0, W_TILE_U), pipeline_mode=pl.Buffered(1)),
        ],
        out_specs=[
            pl.BlockSpec((tm, COL_TILE), lambda i: (i, 0)),
            pl.BlockSpec((tm, D_MODEL), lambda i: (i, 0)),
        ],
        out_shape=[
            jax.ShapeDtypeStruct((m, POOL_WIDTH), jnp.float32),
            jax.ShapeDtypeStruct((m, D_MODEL), jnp.bfloat16),
        ],
        scratch_shapes=[pltpu.VMEM((D_MODEL, COL_TILE), jnp.bfloat16)],
        compiler_params=pltpu.CompilerParams(
            dimension_semantics=("arbitrary",),
            vmem_limit_bytes=PROJ_VMEM_LIMIT),
        name="proj_u",
    )(x2, w_in)


def _gz_w_tile(j):
    return jnp.where(j < GZ_TILE_ZATTN, W_TILE_GATES + j, jnp.where(j == GZ_TILE_ZATTN, W_TILE_ZATTN, W_TILE_ZPOOL))


def _project_qkv(x2, w_in, batch, group, dilation, side_weights=(), round_gz_weights=False, tm=1024):
    m = x2.shape[0]
    tiles_per_batch = m // batch // tm
    row_tiles = batch * tiles_per_batch
    sub = tm // dilation
    n_row_scratch = {1: 0, SUBLANE_STRIDE: 1, SUBLANE_STRIDE * SUBLANE_STRIDE: 2}[dilation]

    def side_index(j, b, i):
        return jnp.where(j == 0, b * tiles_per_batch + i, row_tiles - 1), 0

    in_extra = [pl.BlockSpec((w.shape[0] // row_tiles, w.shape[1]), side_index) for w in side_weights]
    out_extra = list(in_extra)
    extra_shapes = [jax.ShapeDtypeStruct(w.shape, jnp.bfloat16) for w in side_weights]
    extra_operands = list(side_weights)
    if round_gz_weights:
        blocks_per_tile = 3 * row_tiles // GZ_TILES
        assert blocks_per_tile * GZ_TILES == 3 * row_tiles

        def step(j, b, i):
            return (j * batch + b) * tiles_per_batch + i

        in_extra.append(pl.BlockSpec(
            (D_MODEL // blocks_per_tile, COL_TILE),
            lambda j, b, i: (step(j, b, i) % blocks_per_tile, _gz_w_tile(step(j, b, i) // blocks_per_tile))))
        out_extra.append(pl.BlockSpec(
            (D_MODEL // blocks_per_tile, COL_TILE),
            lambda j, b, i: (step(j, b, i) % blocks_per_tile, step(j, b, i) // blocks_per_tile)))
        extra_shapes.append(jax.ShapeDtypeStruct((D_MODEL, GZ_TILES * COL_TILE), jnp.bfloat16))
        extra_operands.append(w_in)
    results = pl.pallas_call(
        functools.partial(_proj_qkv_kernel, dilation=dilation, n_side=len(side_weights),
                          n_stream=int(round_gz_weights)),
        grid=(3, batch, tiles_per_batch),
        in_specs=[
            pl.BlockSpec((tm, D_MODEL), lambda j, b, i: (b * tiles_per_batch + i, 0)),
            pl.BlockSpec((D_MODEL, COL_TILE), lambda j, b, i: (0, N_GROUPS * j + group)),
        ] + in_extra,
        out_specs=[pl.BlockSpec((None, None, dilation, sub, COL_TILE), lambda j, b, i: (j, b, 0, i, 0))]
        + out_extra,
        out_shape=[jax.ShapeDtypeStruct((3, batch, dilation, m // batch // dilation, ATTN_WIDTH), jnp.bfloat16)]
        + extra_shapes,
        scratch_shapes=[pltpu.VMEM((D_MODEL, COL_TILE), jnp.bfloat16)]
        + [pltpu.VMEM((COL_TILE // LANES, tm, LANES), jnp.float32)] * n_row_scratch,
        compiler_params=pltpu.CompilerParams(
            dimension_semantics=("arbitrary", "arbitrary", "arbitrary"),
            vmem_limit_bytes=PROJ_VMEM_LIMIT),
        name=f"proj_qkv_g{group}",
    )(x2, w_in, *extra_operands)
    return results[0], results[1:]


def _attention_block(blk, first_chunk, q_ref, k_ref, v_ref, o_ref, lse_ref, kc_ref, vc_ref):
    bq = ATTN_BLOCK
    scale = HEAD_DIM ** -0.5
    row = lax.broadcasted_iota(jnp.int32, (bq, 2 * bq), 0)
    col = lax.broadcasted_iota(jnp.int32, (bq, 2 * bq), 1)
    dist = bq + row - col
    mask = jnp.logical_and(dist >= 0, dist <= bq)
    r0 = blk * bq
    if blk == 0:
        mask = jnp.logical_and(mask, col >= jnp.where(first_chunk, bq, 0))

        def kcat_of(hs):
            return jnp.concatenate([kc_ref[:, hs], k_ref[0:bq, hs]], axis=0)

        def vcat_of(hs):
            return jnp.concatenate([vc_ref[:, hs], v_ref[0:bq, hs]], axis=0)
    else:
        def kcat_of(hs):
            return k_ref[r0 - bq:r0 + bq, hs]

        def vcat_of(hs):
            return v_ref[r0 - bq:r0 + bq, hs]

    heads = [slice(h * HEAD_DIM, (h + 1) * HEAD_DIM) for h in range(N_HEADS)]
    ss = [jnp.where(mask,
                    lax.dot_general(q_ref[r0:r0 + bq, hs], kcat_of(hs), (((1,), (1,)), ((), ())),
                                    preferred_element_type=jnp.float32),
                    NEG_INF) for hs in heads]
    ms = [jnp.max(s, axis=-1, keepdims=True) for s in ss]

    def finish():
        es = [jnp.exp2((s - m) * (scale * LOG2_E)) for s, m in zip(ss, ms)]
        dens = [jnp.sum(e, axis=-1, keepdims=True) for e in es]
        outs = [jnp.dot(e.astype(o_ref.dtype), vcat_of(hs), preferred_element_type=jnp.float32)
                for e, hs in zip(es, heads)]
        lane = lax.broadcasted_iota(jnp.int32, (bq, LANES), 1)
        lse_tile = jnp.zeros((bq, LANES), jnp.float32)
        for h, hs in enumerate(heads):
            o_ref[r0:r0 + bq, hs] = (outs[h] / dens[h]).astype(o_ref.dtype)
            lse_tile = jnp.where(lane == h, ms[h] * scale + jnp.log(dens[h]), lse_tile)
        lse_ref[r0:r0 + bq, :] = lse_tile
    return finish


def _gz_attn_kernel(x_ref, w_ref, b_ref, *refs, dilations):
    n_groups = len(dilations)
    qkv_refs = refs[:3 * n_groups]
    gz_ref = refs[3 * n_groups]
    attn_out_refs = refs[3 * n_groups + 1:5 * n_groups + 1]
    kc_ref, vc_ref = refs[5 * n_groups + 1:]
    s = pl.program_id(0)
    chunk_cols = COL_TILE // 4

    for g, dilation in enumerate(dilations):
        q_ref, k_ref, v_ref = qkv_refs[3 * g:3 * g + 3]
        o_ref, lse_ref = attn_out_refs[2 * g:2 * g + 2]
        chunks_per_subseq = GZ_ROW_TILES // dilation
        is_gate = 2 * g + 1 < GZ_TILE_ZATTN

        def gz_chunk(c, is_gate=is_gate):
            cols = slice(c * chunk_cols, (c + 1) * chunk_cols)
            acc = jnp.dot(x_ref[...], w_ref[:, cols], preferred_element_type=jnp.float32)
            t = acc + b_ref[:, cols] if is_gate else acc
            sg = 0.5 * jnp.tanh(0.5 * t) + 0.5
            gz_ref[:, cols] = (sg if is_gate else sg * t).astype(gz_ref.dtype)

        @pl.when(s // ATTN_STEPS == g)
        def _(q_ref=q_ref, k_ref=k_ref, v_ref=v_ref, o_ref=o_ref, lse_ref=lse_ref,
              chunks_per_subseq=chunks_per_subseq, gz_chunk=gz_chunk):
            first_chunk = s % chunks_per_subseq == 0

            @pl.when(first_chunk)
            def _():
                kc_ref[...] = jnp.zeros_like(kc_ref)
                vc_ref[...] = jnp.zeros_like(vc_ref)

            gz_chunk(0)
            finish0 = _attention_block(0, first_chunk, q_ref, k_ref, v_ref, o_ref, lse_ref, kc_ref, vc_ref)
            gz_chunk(1)
            finish0()
            gz_chunk(2)
            finish1 = _attention_block(1, first_chunk, q_ref, k_ref, v_ref, o_ref, lse_ref, kc_ref, vc_ref)
            gz_chunk(3)
            finish1()
            kc_ref[...] = k_ref[ATTN_BLOCK:2 * ATTN_BLOCK, :]
            vc_ref[...] = v_ref[ATTN_BLOCK:2 * ATTN_BLOCK, :]


def _gates_z_and_attention(x2, w_gz_bf, b_gate2, qkvs):
    m = x2.shape[0]
    tm = m // GZ_ROW_TILES
    rows = 2 * ATTN_BLOCK
    n_groups = len(qkvs)
    assert GZ_TILES * GZ_ROW_TILES == n_groups * ATTN_STEPS
    dilations = tuple(q.shape[2] for q in qkvs)

    def chunk_of(g, qkv):
        _, batch, dilation, sub_len, _ = qkv.shape
        chunks_per_subseq = sub_len // rows
        per_batch = dilation * chunks_per_subseq
        assert batch * per_batch == ATTN_STEPS and GZ_ROW_TILES % dilation == 0

        def index(step):
            a = jnp.clip(step - g * ATTN_STEPS, 0, ATTN_STEPS - 1)
            rc = a % per_batch
            return a // per_batch, rc // chunks_per_subseq, rc % chunks_per_subseq
        return index

    in_specs = [
        pl.BlockSpec((tm, D_MODEL), lambda st: (st % GZ_ROW_TILES, 0)),
        pl.BlockSpec((D_MODEL, COL_TILE), lambda st: (0, st // GZ_ROW_TILES)),
        pl.BlockSpec((1, COL_TILE), lambda st: (0, jnp.minimum(st // GZ_ROW_TILES, GZ_TILE_ZATTN - 1))),
    ]
    out_specs = [pl.BlockSpec((tm, COL_TILE), lambda st: (st % GZ_ROW_TILES, st // GZ_ROW_TILES))]
    out_shape = [jax.ShapeDtypeStruct((m, GZ_TILES * COL_TILE), jnp.bfloat16)]
    operands = [x2, w_gz_bf, b_gate2]
    for g, qkv in enumerate(qkvs):
        _, batch, dilation, sub_len, _ = qkv.shape
        index = chunk_of(g, qkv)
        for t in range(3):
            in_specs.append(pl.BlockSpec((None, None, None, rows, ATTN_WIDTH),
                                         lambda st, t=t, index=index: (t, *index(st), 0)))
            operands.append(qkv)
        out_specs.append(pl.BlockSpec((None, None, rows, ATTN_WIDTH), lambda st, index=index: (*index(st), 0)))
        out_specs.append(pl.BlockSpec((None, None, rows, LANES), lambda st, index=index: (*index(st), 0)))
        out_shape.append(jax.ShapeDtypeStruct((batch, dilation, sub_len, ATTN_WIDTH), jnp.bfloat16))
        out_shape.append(jax.ShapeDtypeStruct((batch, dilation, sub_len, LANES), jnp.float32))
    results = pl.pallas_call(
        functools.partial(_gz_attn_kernel, dilations=dilations),
        grid=(GZ_TILES * GZ_ROW_TILES,),
        in_specs=in_specs,
        out_specs=out_specs,
        out_shape=out_shape,
        scratch_shapes=[
            pltpu.VMEM((ATTN_BLOCK, ATTN_WIDTH), jnp.bfloat16),
            pltpu.VMEM((ATTN_BLOCK, ATTN_WIDTH), jnp.bfloat16),
        ],
        compiler_params=pltpu.CompilerParams(
            dimension_semantics=("arbitrary",),
            vmem_limit_bytes=PROJ_VMEM_LIMIT),
        name="gates_z_attention",
    )(*operands)
    gz = results[0]
    os_ = [results[1 + 2 * g] for g in range(n_groups)]
    lses = [results[2 + 2 * g] for g in range(n_groups)]
    return gz, os_, lses


def _to_token_order(src_ref, dst_ref, slab, src_cols, mid_ref):
    dilation, sub = src_ref.shape[0], src_ref.shape[1]
    if dilation == SUBLANE_STRIDE:
        for r in range(dilation):
            dst_ref[slab, pl.ds(r, sub, stride=dilation), :] = src_ref[r, :, src_cols].astype(dst_ref.dtype)
        return
    assert dilation == SUBLANE_STRIDE * SUBLANE_STRIDE
    quarter = sub * SUBLANE_STRIDE
    for r in range(dilation):
        r1, r0 = divmod(r, SUBLANE_STRIDE)
        mid_ref[slab, pl.ds(r0 * quarter + r1, sub, stride=SUBLANE_STRIDE), :] = (
            src_ref[r, :, src_cols].astype(mid_ref.dtype))
    for r0 in range(SUBLANE_STRIDE):
        dst_ref[slab, pl.ds(r0, quarter, stride=SUBLANE_STRIDE), :] = mid_ref[slab, pl.ds(r0 * quarter, quarter), :]


def _combine_weights(l1_ref, l2_ref, l3_ref, ln2_ref, ln3_ref, lmid_ref):
    _to_token_order(l2_ref, ln2_ref, 0, slice(None), lmid_ref)
    _to_token_order(l3_ref, ln3_ref, 0, slice(None), lmid_ref)
    l1 = l1_ref[...]
    l2 = ln2_ref[0]
    l3 = ln3_ref[0]
    mx = jnp.maximum(jnp.maximum(l1, l2), l3)
    e1 = jnp.exp(l1 - mx)
    e2 = jnp.exp(l2 - mx)
    e3 = jnp.exp(l3 - mx)
    tot = e1 + e2 + e3
    return e1 / tot, e2 / tot, e3 / tot


def _combine_head(h, weights, o1_ref, o2_ref, o3_ref, za_ref, on2_ref, on3_ref, omid_ref, ya_ref):
    hs = slice(h * HEAD_DIM, (h + 1) * HEAD_DIM)
    _to_token_order(o2_ref, on2_ref, h, hs, omid_ref)
    _to_token_order(o3_ref, on3_ref, h, hs, omid_ref)
    w1, w2, w3 = weights
    o = (w1[:, h:h + 1] * o1_ref[:, hs].astype(jnp.float32)
         + w2[:, h:h + 1] * on2_ref[h]
         + w3[:, h:h + 1] * on3_ref[h])
    ya_ref[:, hs] = (o * za_ref[:, hs].astype(jnp.float32)).astype(ya_ref.dtype)


def _pool_window_sums(i, u_ref, halo_ref, ext_ref, s2_ref, s4_ref, s8_ref, *, tm):
    rows = HALO + tm
    ext_ref[0:HALO, :] = jnp.where(i > 0, halo_ref[...], 0.0)
    ext_ref[HALO:, :] = u_ref[...]
    g = POOL_GROUP
    s2_ref[8:rows, :] = ext_ref[8:rows, :] + ext_ref[7:rows - 1, :]
    s4_ref[16:rows, :] = s2_ref[16:rows, g:] + s2_ref[14:rows - 2, g:]
    s8_ref[24:rows, :] = s4_ref[24:rows, g:] + s4_ref[20:rows - 4, g:]


def _pool_group(k, i, zp_ref, w_pool_ref, pool_scale_ref, ext_ref, s2_ref, s4_ref, s8_ref, yp_ref, *, tm):
    rows = HALO + tm
    g = POOL_GROUP
    w = POOL_WINDOWS[k]
    if k < 3:
        window_sum = (s2_ref, s4_ref, s8_ref)[k][HALO:rows, 0:g]
    else:
        window_sum = s8_ref[HALO:rows, g:] + s8_ref[HALO - 8:rows - 8, g:]
    cs = slice(k * g, (k + 1) * g)
    pos = i * tm + lax.broadcasted_iota(jnp.int32, (tm, 1), 0)
    cnt = jnp.minimum(pos + 1, w).astype(jnp.float32)
    p = window_sum / cnt - ext_ref[HALO:rows, cs]
    y = jnp.dot(p.astype(jnp.bfloat16), w_pool_ref[k], preferred_element_type=jnp.float32)
    y = y * pool_scale_ref[:, cs] * zp_ref[:, cs].astype(jnp.float32)
    yp_ref[:, cs] = y.astype(yp_ref.dtype)


def _layer_norm_rows(r_ref, gamma_ref, beta_ref):
    r = r_ref[...]
    mu = jnp.mean(r, axis=-1, keepdims=True)
    rc = r - mu
    var = jnp.mean(rc * rc, axis=-1, keepdims=True)
    r_ref[...] = rc * lax.rsqrt(var + LN_EPS) * gamma_ref[...] + beta_ref[...]


def _merge_kernel(o1_ref, o2_ref, o3_ref, l1_ref, l2_ref, l3_ref, za_ref, zp_ref, u_ref, halo_ref,
                  g_attn_ref, g_pool_ref, x_ref, w_pool_ref, pool_scale_ref, wpa_ref, wpp_ref, w_out_ref,
                  gamma_ref, beta_ref, out_ref, ext_ref, s2_ref, s4_ref, s8_ref, on2_ref, on3_ref, omid_ref,
                  ln2_ref, ln3_ref, lmid_ref,
                  ya0_ref, yp0_ref, ya1_ref, yp1_ref, merged_ref, *, tm, n_tiles, tiles_per_seq):
    s = pl.program_id(0)
    i_prep = jnp.minimum(s, n_tiles - 1) % tiles_per_seq

    @pl.when(s == 0)
    def _():
        weights = _combine_weights(l1_ref, l2_ref, l3_ref, ln2_ref, ln3_ref, lmid_ref)
        for h in range(N_HEADS):
            _combine_head(h, weights, o1_ref, o2_ref, o3_ref, za_ref, on2_ref, on3_ref, omid_ref, ya0_ref)
        _pool_window_sums(i_prep, u_ref, halo_ref, ext_ref, s2_ref, s4_ref, s8_ref, tm=tm)
        for k in range(len(POOL_WINDOWS)):
            _pool_group(k, i_prep, zp_ref, w_pool_ref, pool_scale_ref, ext_ref, s2_ref, s4_ref, s8_ref,
                        yp0_ref, tm=tm)

    def step(ya_w, yp_w, ya_r, yp_r):
        chunks = [slice(c * MERGE_CHUNK, (c + 1) * MERGE_CHUNK) for c in range(D_MODEL // MERGE_CHUNK)]
        ya = ya_r[...]
        yp = yp_r[...]
        weights = _combine_weights(l1_ref, l2_ref, l3_ref, ln2_ref, ln3_ref, lmid_ref)
        _pool_window_sums(i_prep, u_ref, halo_ref, ext_ref, s2_ref, s4_ref, s8_ref, tm=tm)
        for c, cols in enumerate(chunks):
            pa = jnp.dot(ya, wpa_ref[:, cols], preferred_element_type=jnp.float32)
            pp = jnp.dot(yp, wpp_ref[:, cols], preferred_element_type=jnp.float32)
            merged_ref[:, cols] = (g_attn_ref[:, cols].astype(jnp.float32) * pa
                                   + g_pool_ref[:, cols].astype(jnp.float32) * pp).astype(merged_ref.dtype)
            _combine_head(c, weights, o1_ref, o2_ref, o3_ref, za_ref, on2_ref, on3_ref, omid_ref, ya_w)
            if c % 2 == 1:
                _pool_group(c // 2, i_prep, zp_ref, w_pool_ref, pool_scale_ref, ext_ref, s2_ref, s4_ref, s8_ref,
                            yp_w, tm=tm)
        merged = merged_ref[...]
        for cols in chunks:
            out = jnp.dot(merged, w_out_ref[:, cols], preferred_element_type=jnp.float32)
            out_ref[:, cols] = DEEPNORM_ALPHA * x_ref[:, cols] + out
        _layer_norm_rows(out_ref, gamma_ref, beta_ref)

    @pl.when(jnp.logical_and(s > 0, s % 2 == 0))
    def _():
        step(ya0_ref, yp0_ref, ya1_ref, yp1_ref)

    @pl.when(s % 2 == 1)
    def _():
        step(ya1_ref, yp1_ref, ya0_ref, yp0_ref)


def _merge(os_, lses, gz3, u3, x, w_pool_bf, pool_scale2, wpa_bf, wpp_bf, w_out_bf, gamma2, beta2, tm=256):
    b, s, _ = x.shape
    tiles_per_seq = s // tm
    n_tiles = b * tiles_per_seq

    def prep_tile(step):
        t = jnp.minimum(step, n_tiles - 1)
        return t // tiles_per_seq, t % tiles_per_seq

    def proj_tile(step):
        t = jnp.maximum(step - 1, 0)
        return t // tiles_per_seq, t % tiles_per_seq

    def tok(which, width, tile=0):
        return pl.BlockSpec((None, tm, width), lambda st: (*which(st), tile))

    def grouped(arr):
        dilation, width = arr.shape[1], arr.shape[3]

        def index(st):
            bb, i = prep_tile(st)
            return bb, 0, i, 0
        if dilation == 1:
            return pl.BlockSpec((None, None, tm, width), index)
        return pl.BlockSpec((None, dilation, tm // dilation, width), index)

    def halo_index(st):
        bb, i = prep_tile(st)
        return bb, jnp.maximum(i * (tm // HALO) - 1, 0), 0

    def const(shape):
        nd = len(shape)
        return pl.BlockSpec(shape, lambda st: (0,) * nd, pipeline_mode=pl.Buffered(1))

    in_specs = [
        grouped(os_[0]), grouped(os_[1]), grouped(os_[2]),
        grouped(lses[0]), grouped(lses[1]), grouped(lses[2]),
        tok(prep_tile, COL_TILE, GZ_TILE_ZATTN), tok(prep_tile, COL_TILE, GZ_TILE_ZPOOL),
        tok(prep_tile, POOL_WIDTH),
        pl.BlockSpec((None, HALO, POOL_WIDTH), halo_index),
        tok(proj_tile, D_MODEL, 0), tok(proj_tile, D_MODEL, 1),
        tok(proj_tile, D_MODEL),
        const((len(POOL_WINDOWS), POOL_GROUP, POOL_GROUP)),
        const((1, POOL_WIDTH)),
        const((ATTN_WIDTH, D_MODEL)),
        const((POOL_WIDTH, D_MODEL)),
        const((D_MODEL, D_MODEL)),
        const((1, D_MODEL)),
        const((1, D_MODEL)),
    ]
    f32 = jnp.float32
    return pl.pallas_call(
        functools.partial(_merge_kernel, tm=tm, n_tiles=n_tiles, tiles_per_seq=tiles_per_seq),
        grid=(n_tiles + 1,),
        in_specs=in_specs,
        out_specs=tok(proj_tile, D_MODEL),
        out_shape=jax.ShapeDtypeStruct((b, s, D_MODEL), f32),
        scratch_shapes=[
            pltpu.VMEM((HALO + tm, POOL_WIDTH), f32),
            pltpu.VMEM((HALO + tm, POOL_WIDTH), f32),
            pltpu.VMEM((HALO + tm, POOL_WIDTH - POOL_GROUP), f32),
            pltpu.VMEM((HALO + tm, POOL_WIDTH - 2 * POOL_GROUP), f32),
            pltpu.VMEM((N_HEADS, tm, HEAD_DIM), f32),
            pltpu.VMEM((N_HEADS, tm, HEAD_DIM), f32),
            pltpu.VMEM((N_HEADS, tm, HEAD_DIM), f32),
            pltpu.VMEM((1, tm, LANES), f32),
            pltpu.VMEM((1, tm, LANES), f32),
            pltpu.VMEM((1, tm, LANES), f32),
            pltpu.VMEM((tm, ATTN_WIDTH), jnp.bfloat16),
            pltpu.VMEM((tm, POOL_WIDTH), jnp.bfloat16),
            pltpu.VMEM((tm, ATTN_WIDTH), jnp.bfloat16),
            pltpu.VMEM((tm, POOL_WIDTH), jnp.bfloat16),
            pltpu.VMEM((tm, D_MODEL), jnp.bfloat16),
        ],
        compiler_params=pltpu.CompilerParams(
            dimension_semantics=("arbitrary",),
            vmem_limit_bytes=MERGE_VMEM_LIMIT),
        name="merge_out_ln",
    )(os_[0], os_[1], os_[2], lses[0], lses[1], lses[2], gz3, gz3, u3, u3, gz3, gz3, x,
      w_pool_bf, pool_scale2, wpa_bf, wpp_bf, w_out_bf, gamma2, beta2)


def _layer(x, w_in, b_gate, w_pool, pool_scale, w_proj_attn, w_proj_pool, w_out, ln_gamma, ln_beta):
    b, s, d = x.shape
    u2, x2 = _project_u(x.reshape(b * s, d), w_in)
    later_weights = (w_pool.reshape(-1, POOL_GROUP), w_proj_attn, w_proj_pool, w_out)
    last = N_GROUPS - 1
    qkvs = []
    for g, (window, dilation) in enumerate(DILATED_GROUPS):
        assert window // dilation == ATTN_BLOCK
        qkv, rounded = _project_qkv(x2, w_in, b, g, dilation, later_weights if g == 0 else (),
                                    round_gz_weights=(g == last))
        qkvs.append(qkv)
        if g == 0:
            w_pool_bf, wpa_bf, wpp_bf, w_out_bf = rounded
        if g == last:
            (w_gz_bf,) = rounded
    gz2, os_, lses = _gates_z_and_attention(x2, w_gz_bf, b_gate.reshape(1, -1), qkvs)
    gz3 = gz2.reshape(b, s, GZ_TILES * COL_TILE)
    u3 = u2.reshape(b, s, POOL_WIDTH)
    return _merge(os_, lses, gz3, u3, x, w_pool_bf.reshape(w_pool.shape), pool_scale.reshape(1, -1),
                  wpa_bf, wpp_bf, w_out_bf, ln_gamma.reshape(1, -1), ln_beta.reshape(1, -1))


def kernel(x, w_in, b_gate, w_pool, pool_scale, w_proj_attn, w_proj_pool, w_out, ln_gamma, ln_beta):
    depth = w_in.shape[0]
    for layer in range(depth):
        x = _layer(x, w_in[layer], b_gate[layer], w_pool[layer], pool_scale[layer],
                   w_proj_attn[layer], w_proj_pool[layer], w_out[layer],
                   ln_gamma[layer], ln_beta[layer])
    return x
```
